```python
import math
import jax, jax.numpy as jnp
from jax import lax
import numpy as np

D_MODEL = 4096
BATCH = 8
SEQ = 2048
DEPTH = 1

DIFF_HEADS = 8
DIFF_HEAD_DIM = 128
DIFF_V_DIM = 2 * DIFF_HEAD_DIM
DIFF_WIDTH = DIFF_HEADS * DIFF_V_DIM
MLA_HEADS = 16
MLA_Q_RANK = 768
MLA_KV_RANK = 512
MLA_NOPE_DIM = 128
MLA_ROPE_DIM = 64
MLA_V_DIM = 128
MLA_WIDTH = MLA_HEADS * MLA_V_DIM
ROPE_THETA = 10000.0
N_BRANCHES = 2
Q_DIFF_COLS = 2 * DIFF_HEADS * DIFF_HEAD_DIM
K_DIFF_COLS = 2 * DIFF_HEADS * DIFF_HEAD_DIM
V_DIFF_COLS = DIFF_HEADS * DIFF_V_DIM
GATE_COLS = N_BRANCHES * D_MODEL
IN_COLS = (Q_DIFF_COLS + K_DIFF_COLS + V_DIFF_COLS + MLA_Q_RANK + MLA_KV_RANK
           + MLA_ROPE_DIM + GATE_COLS)
MIX_WIDTH = DIFF_WIDTH + MLA_WIDTH
N_EXPERTS = 64
TOP_K = 6
N_GROUPS = 8
TOPK_GROUPS = 4
EXPERT_DIM = 512
SHARED_DIM = 512
ROUTED_SCALE = 2.5
EXPERT_BLOCK = 128
Q_BLOCK = 128
NORM_EPS = 1e-6
N_MOD = 6

kernel_name = "hybrid_diffattn_mla_moe_adaln"


def rms_norm(x, gain):
    x32 = x.astype(jnp.float32)
    y = x32 * lax.rsqrt(jnp.mean(x32 * x32, axis=-1, keepdims=True) + NORM_EPS)
    return (y * gain.astype(jnp.float32)).astype(x.dtype)


def rotary(t, pos):
    half = t.shape[-1] // 2
    inv_freq = ROPE_THETA ** (-jnp.arange(half, dtype=jnp.float32) / half)
    ang = pos.astype(jnp.float32)[:, None, :, None] * inv_freq
    cos, sin = jnp.cos(ang), jnp.sin(ang)
    t32 = t.astype(jnp.float32)
    t1, t2 = t32[..., :half], t32[..., half:]
    return jnp.concatenate([t1 * cos - t2 * sin, t2 * cos + t1 * sin], axis=-1).astype(t.dtype)


def alibi_slopes(n_heads):
    return 2.0 ** (-8.0 * jnp.arange(1, n_heads + 1, dtype=jnp.float32) / n_heads)


def lambda_init(layer):
    return 0.8 - 0.6 * math.exp(-0.3 * layer)


def to_query_blocks(t):
    b, h, s, d = t.shape
    return t.reshape(b, h, s // Q_BLOCK, Q_BLOCK, d).transpose(2, 0, 1, 3, 4)


def from_query_blocks(t):
    nqb, b, h, qb, d = t.shape
    return t.transpose(1, 2, 0, 3, 4).reshape(b, h, nqb * qb, d)


def diff_attention(q, k, v, pos, lam):
    b, h2, s, d = q.shape
    nqb = s // Q_BLOCK
    scale = d ** -0.5
    slopes = jnp.repeat(alibi_slopes(DIFF_HEADS), 2)
    key_idx = jnp.arange(s)

    def block(args):
        qb, qpos, start = args
        sc = jnp.einsum('bhqd,bhkd->bhqk', qb, k, preferred_element_type=jnp.float32) * scale
        dist = jnp.abs(qpos[:, :, None] - pos[:, None, :]).astype(jnp.float32)
        sc = sc - slopes[None, :, None, None] * dist[:, None]
        causal = (start + jnp.arange(Q_BLOCK))[:, None] >= key_idx[None, :]
        sc = jnp.where(causal[None, None], sc, -jnp.inf)
        p = jax.nn.softmax(sc, axis=-1).reshape(b, DIFF_HEADS, 2, Q_BLOCK, s)
        a = p[:, :, 0] - lam * p[:, :, 1]
        return jnp.einsum('bhqk,bhkv->bhqv', a.astype(v.dtype), v)

    qpos = pos.reshape(b, nqb, Q_BLOCK).transpose(1, 0, 2)
    starts = jnp.arange(nqb) * Q_BLOCK
    out = lax.map(block, (to_query_blocks(q), qpos, starts))
    return from_query_blocks(out)


def mla_attention(q_nope, q_rope, k_nope, k_rope, v):
    s = q_nope.shape[2]
    nqb = s // Q_BLOCK
    scale = (MLA_NOPE_DIM + MLA_ROPE_DIM) ** -0.5
    key_idx = jnp.arange(s)

    def block(args):
        qn, qr, start = args
        sc = (jnp.einsum('bhqd,bhkd->bhqk', qn, k_nope, preferred_element_type=jnp.float32)
              + jnp.einsum('bhqd,bkd->bhqk', qr, k_rope, preferred_element_type=jnp.float32)) * scale
        causal = (start + jnp.arange(Q_BLOCK))[:, None] >= key_idx[None, :]
        sc = jnp.where(causal[None, None], sc, -jnp.inf)
        p = jax.nn.softmax(sc, axis=-1)
        return jnp.einsum('bhqk,bhkv->bhqv', p.astype(v.dtype), v)

    starts = jnp.arange(nqb) * Q_BLOCK
    out = lax.map(block, (to_query_blocks(q_nope), to_query_blocks(q_rope), starts))
    return from_query_blocks(out)


def hybrid_mixer(h, pos, w_in, diff_lambda, diff_subln, mla_q_norm, mla_w_uq,
                 mla_kv_norm, mla_w_ukv, w_out, lam_init):
    b, s, _ = h.shape
    proj = h @ w_in
    splits = np.cumsum([Q_DIFF_COLS, K_DIFF_COLS, V_DIFF_COLS, MLA_Q_RANK, MLA_KV_RANK,
                        MLA_ROPE_DIM]).tolist()
    q_d, k_d, v_d, c_q, c_kv, k_pe, g_logits = jnp.split(proj, splits, axis=-1)

    q_d = q_d.reshape(b, s, 2 * DIFF_HEADS, DIFF_HEAD_DIM).transpose(0, 2, 1, 3)
    k_d = k_d.reshape(b, s, 2 * DIFF_HEADS, DIFF_HEAD_DIM).transpose(0, 2, 1, 3)
    v_d = v_d.reshape(b, s, DIFF_HEADS, DIFF_V_DIM).transpose(0, 2, 1, 3)
    lp = diff_lambda.astype(jnp.float32)
    lam = jnp.exp(jnp.sum(lp[0] * lp[1])) - jnp.exp(jnp.sum(lp[2] * lp[3])) + lam_init
    o_d = diff_attention(q_d, k_d, v_d, pos, lam)
    o_d = rms_norm(o_d, diff_subln) * (1.0 - lam_init)
    o_d = o_d.transpose(0, 2, 1, 3).reshape(b, s, DIFF_WIDTH)

    q = (rms_norm(c_q, mla_q_norm) @ mla_w_uq).reshape(b, s, MLA_HEADS, MLA_NOPE_DIM + MLA_ROPE_DIM)
    q = q.transpose(0, 2, 1, 3)
    q_nope, q_rope = q[..., :MLA_NOPE_DIM], rotary(q[..., MLA_NOPE_DIM:], pos)
    kv = (rms_norm(c_kv, mla_kv_norm) @ mla_w_ukv).reshape(b, s, MLA_HEADS, MLA_NOPE_DIM + MLA_V_DIM)
    kv = kv.transpose(0, 2, 1, 3)
    k_nope, v_m = kv[..., :MLA_NOPE_DIM], kv[..., MLA_NOPE_DIM:]
    k_rope = rotary(k_pe[:, None], pos)[:, 0]
    o_m = mla_attention(q_nope, q_rope, k_nope, k_rope, v_m)
    o_m = o_m.transpose(0, 2, 1, 3).reshape(b, s, MLA_WIDTH)

    g = jax.nn.sigmoid(g_logits.astype(jnp.float32)).astype(h.dtype).reshape(b, s, N_BRANCHES, D_MODEL)
    y_d = o_d @ w_out[:DIFF_WIDTH]
    y_m = o_m @ w_out[DIFF_WIDTH:]
    return g[:, :, 0] * y_d + g[:, :, 1] * y_m


def swiglu(x, w1, w3, w2):
    return (jax.nn.silu(x @ w1) * (x @ w3)) @ w2


def moe_ffn(h, router_w, router_bias, exp_w1, exp_w3, exp_w2, shared_w1, shared_w3, shared_w2):
    b, s, d = h.shape
    n = b * s
    hf = h.reshape(n, d)

    scores = jax.nn.sigmoid(jnp.einsum('nd,de->ne', hf, router_w, preferred_element_type=jnp.float32))
    sel = scores + router_bias.astype(jnp.float32)
    grp_score = lax.top_k(sel.reshape(n, N_GROUPS, N_EXPERTS // N_GROUPS), 2)[0].sum(-1)
    _, top_grp = lax.top_k(grp_score, TOPK_GROUPS)
    grp_mask = jnp.any(top_grp[..., None] == jnp.arange(N_GROUPS), axis=1)
    expert_mask = jnp.repeat(grp_mask, N_EXPERTS // N_GROUPS, axis=1)
    _, idx = lax.top_k(jnp.where(expert_mask, sel, -jnp.inf), TOP_K)
    wts = jnp.take_along_axis(scores, idx, axis=1)
    wts = wts / jnp.sum(wts, axis=-1, keepdims=True) * ROUTED_SCALE

    nk = n * TOP_K
    e_flat = idx.reshape(nk)
    tok_flat = jnp.arange(nk, dtype=jnp.int32) // TOP_K
    w_flat = wts.reshape(nk)
    order = jnp.argsort(e_flat)
    e_sorted = e_flat[order]
    sizes = jnp.bincount(e_flat, length=N_EXPERTS)
    padded = (sizes + EXPERT_BLOCK - 1) // EXPERT_BLOCK * EXPERT_BLOCK
    pad_end = jnp.cumsum(padded)
    pad_start = pad_end - padded
    grp_start = jnp.cumsum(sizes) - sizes
    dest = pad_start[e_sorted] + (jnp.arange(nk) - grp_start[e_sorted])
    n_blocks = -(-nk // EXPERT_BLOCK) + N_EXPERTS
    rows = n_blocks * EXPERT_BLOCK
    row_tok = jnp.zeros((rows,), jnp.int32).at[dest].set(tok_flat[order])
    row_w = jnp.zeros((rows,), jnp.float32).at[dest].set(w_flat[order])
    block_expert = jnp.minimum(
        jnp.searchsorted(pad_end, jnp.arange(n_blocks) * EXPERT_BLOCK, side='right'), N_EXPERTS - 1)

    def expert_block(args):
        e, tok, wt = args
        yb = swiglu(hf[tok], exp_w1[e], exp_w3[e], exp_w2[e])
        return yb * wt[:, None].astype(yb.dtype)

    out = lax.map(expert_block, (block_expert, row_tok.reshape(n_blocks, EXPERT_BLOCK),
                                 row_w.reshape(n_blocks, EXPERT_BLOCK)))
    routed = jax.ops.segment_sum(out.reshape(rows, d), row_tok, num_segments=n)
    shared = swiglu(hf, shared_w1, shared_w3, shared_w2)
    return (routed + shared).reshape(b, s, d)


def setup_inputs(seed: int = 0) -> dict:
    key = jax.random.key(seed)
    ks = jax.random.split(key, 24)

    def nrm(k, shape, scale):
        return jax.random.normal(k, shape, jnp.float32) * scale

    d = D_MODEL
    return {
        "x": nrm(ks[0], (BATCH, SEQ, d), 1.0),
        "c": nrm(ks[1], (BATCH, d), 1.0),
        "positions": jnp.arange(SEQ, dtype=jnp.int32)[None, :]
                     + jax.random.randint(ks[2], (BATCH, 1), 0, 1024, dtype=jnp.int32),
        "w_ada": nrm(ks[3], (DEPTH, d, N_MOD * d), 0.5 * d ** -0.5),
        "b_ada": nrm(ks[4], (DEPTH, N_MOD * d), 0.02),
        "norm_attn": 1.0 + nrm(ks[5], (DEPTH, d), 0.02),
        "w_in": nrm(ks[6], (DEPTH, d, IN_COLS), d ** -0.5),
        "diff_lambda": nrm(ks[7], (DEPTH, 4, DIFF_HEAD_DIM), 0.1),
        "diff_subln": 1.0 + nrm(ks[8], (DEPTH, DIFF_V_DIM), 0.02),
        "mla_q_norm": 1.0 + nrm(ks[9], (DEPTH, MLA_Q_RANK), 0.02),
        "mla_w_uq": nrm(ks[10], (DEPTH, MLA_Q_RANK, MLA_HEADS * (MLA_NOPE_DIM + MLA_ROPE_DIM)),
                         MLA_Q_RANK ** -0.5),
        "mla_kv_norm": 1.0 + nrm(ks[11], (DEPTH, MLA_KV_RANK), 0.02),
        "mla_w_ukv": nrm(ks[12], (DEPTH, MLA_KV_RANK, MLA_HEADS * (MLA_NOPE_DIM + MLA_V_DIM)),
                          MLA_KV_RANK ** -0.5),
        "w_out": nrm(ks[13], (DEPTH, MIX_WIDTH, d), MIX_WIDTH ** -0.5),
        "norm_ffn": 1.0 + nrm(ks[14], (DEPTH, d), 0.02),
        "router_w": nrm(ks[15], (DEPTH, d, N_EXPERTS), d ** -0.5),
        "router_bias": nrm(ks[16], (DEPTH, N_EXPERTS), 0.01),
        "exp_w1": nrm(ks[17], (DEPTH, N_EXPERTS, d, EXPERT_DIM), d ** -0.5),
        "exp_w3": nrm(ks[18], (DEPTH, N_EXPERTS, d, EXPERT_DIM), d ** -0.5),
        "exp_w2": nrm(ks[19], (DEPTH, N_EXPERTS, EXPERT_DIM, d), EXPERT_DIM ** -0.5),
        "shared_w1": nrm(ks[20], (DEPTH, d, SHARED_DIM), d ** -0.5),
        "shared_w3": nrm(ks[21], (DEPTH, d, SHARED_DIM), d ** -0.5),
        "shared_w2": nrm(ks[22], (DEPTH, SHARED_DIM, d), SHARED_DIM ** -0.5),
        "final_norm": 1.0 + nrm(ks[23], (d,), 0.02),
    }


def reference(x, c, positions, w_ada, b_ada, norm_attn, w_in, diff_lambda, diff_subln,
              mla_q_norm, mla_w_uq, mla_kv_norm, mla_w_ukv, w_out, norm_ffn, router_w,
              router_bias, exp_w1, exp_w3, exp_w2, shared_w1, shared_w3, shared_w2, final_norm):
    for l in range(DEPTH):
        mod = jnp.einsum('bd,de->be', jax.nn.silu(c), w_ada[l]) + b_ada[l]
        sh_a, sc_a, gt_a, sh_f, sc_f, gt_f = [m[:, None, :] for m in jnp.split(mod, N_MOD, axis=-1)]

        h = rms_norm(x, norm_attn[l]) * (1.0 + sc_a) + sh_a
        x = x + gt_a * hybrid_mixer(h, positions, w_in[l], diff_lambda[l], diff_subln[l],
                                    mla_q_norm[l], mla_w_uq[l], mla_kv_norm[l], mla_w_ukv[l],
                                    w_out[l], lambda_init(l))

        h = rms_norm(x, norm_ffn[l]) * (1.0 + sc_f) + sh_f
        x = x + gt_f * moe_ffn(h, router_w[l], router_bias[l], exp_w1[l], exp_w3[l], exp_w2[l],
                               shared_w1[l], shared_w3[l], shared_w2[l])
    return rms_norm(x, final_norm)
```

```python
import functools
import math

import jax
import jax.numpy as jnp
from jax import lax
from jax.experimental import pallas as pl
from jax.experimental.pallas import tpu as pltpu

F32 = jnp.float32
BF16 = jnp.bfloat16
U32 = jnp.uint32
I32 = jnp.int32

DIFF_HEADS = 8
DIFF_HEAD_DIM = 128
MLA_HEADS = 16
MLA_NOPE_DIM = 128
MLA_ROPE_DIM = 64
MLA_V_DIM = 128
ROPE_THETA = 10000.0
TOP_K = 6
N_GROUPS = 8
TOPK_GROUPS = 4
ROUTED_SCALE = 2.5
NORM_EPS = 1e-6
N_MOD = 6

LANES = 128
SUBLANES = 8
MLA_QK_PAD = 256
EXPERT_ROWS = 256
VMEM_LIMIT = 56 * 1024 * 1024


def _cparams(sem):
    return pltpu.CompilerParams(dimension_semantics=sem, vmem_limit_bytes=VMEM_LIMIT)


def _tile(n, pref):
    t = min(n, pref)
    assert n % t == 0, (n, pref)
    return t


def _silu(a):
    return a * jax.nn.sigmoid(a)


def _ada_kernel(c_ref, w_ref, b_ref, o_ref):
    c = c_ref[...]
    a = _silu(c).astype(BF16)
    o_ref[...] = jnp.dot(a, w_ref[...].astype(BF16), preferred_element_type=F32) + b_ref[...]


def _ada(c, w, b):
    bsz, d = c.shape
    n = w.shape[1]
    tn = _tile(n, 512)
    return pl.pallas_call(
        _ada_kernel,
        grid=(n // tn,),
        in_specs=[pl.BlockSpec((bsz, d), lambda j: (0, 0)),
                  pl.BlockSpec((d, tn), lambda j: (0, j)),
                  pl.BlockSpec((1, tn), lambda j: (0, j))],
        out_specs=pl.BlockSpec((bsz, tn), lambda j: (0, j)),
        out_shape=jax.ShapeDtypeStruct((bsz, n), F32),
        compiler_params=_cparams(("arbitrary",)),
    )(c, w, b.reshape(1, n))


def _norm_mod_kernel(x_ref, g_ref, sc_ref, sh_ref, o_ref):
    x = x_ref[0]
    ms = jnp.mean(x * x, axis=-1, keepdims=True)
    y = x * lax.rsqrt(ms + NORM_EPS) * g_ref[...]
    o_ref[0] = (y * (1.0 + sc_ref[0, 0]) + sh_ref[0, 0]).astype(o_ref.dtype)


def _norm_mod(x, gain, mod, sc_idx, sh_idx):
    bsz, s, d = x.shape
    tm = _tile(s, 512)
    return pl.pallas_call(
        _norm_mod_kernel,
        grid=(bsz, s // tm),
        in_specs=[pl.BlockSpec((1, tm, d), lambda b, i: (b, i, 0)),
                  pl.BlockSpec((1, d), lambda b, i: (0, 0)),
                  pl.BlockSpec((1, 1, 1, d), lambda b, i: (sc_idx, b, 0, 0)),
                  pl.BlockSpec((1, 1, 1, d), lambda b, i: (sh_idx, b, 0, 0))],
        out_specs=pl.BlockSpec((1, tm, d), lambda b, i: (b, i, 0)),
        out_shape=jax.ShapeDtypeStruct((bsz, s, d), BF16),
        compiler_params=_cparams(("arbitrary", "arbitrary")),
    )(x, gain.reshape(1, d), mod, mod)


def _mm_kernel(a_ref, b_ref, o_ref, *, sigmoid):
    acc = jnp.dot(a_ref[...], b_ref[...], preferred_element_type=F32)
    if sigmoid:
        acc = jax.nn.sigmoid(acc)
    o_ref[...] = acc.astype(o_ref.dtype)


def _matmul(a, b, out_dtype, *, sigmoid=False, tm_pref=1024, tn_pref=512):
    m, k = a.shape
    n = b.shape[1]
    tm, tn = _tile(m, tm_pref), _tile(n, tn_pref)
    return pl.pallas_call(
        functools.partial(_mm_kernel, sigmoid=sigmoid),
        grid=(m // tm, n // tn),
        in_specs=[pl.BlockSpec((tm, k), lambda i, j: (i, 0)),
                  pl.BlockSpec((k, tn), lambda i, j: (0, j))],
        out_specs=pl.BlockSpec((tm, tn), lambda i, j: (i, j)),
        out_shape=jax.ShapeDtypeStruct((m, n), out_dtype),
        compiler_params=_cparams(("arbitrary", "arbitrary")),
    )(a, b)


def _latent_kernel(a_ref, b_ref, cq_ref, ckv_ref, kpe_ref, *, q_rank, kv_rank):
    acc = jnp.dot(a_ref[...], b_ref[...], preferred_element_type=F32)
    cq_ref[...] = acc[:, :q_rank].astype(cq_ref.dtype)
    ckv_ref[...] = acc[:, q_rank:q_rank + kv_rank].astype(ckv_ref.dtype)
    kpe_ref[...] = acc[:, q_rank + kv_rank:]


def _latent_proj(h, w_lat, q_rank, kv_rank):
    m, k = h.shape
    n = w_lat.shape[1]
    tm = _tile(m, 512)
    return pl.pallas_call(
        functools.partial(_latent_kernel, q_rank=q_rank, kv_rank=kv_rank),
        grid=(m // tm,),
        in_specs=[pl.BlockSpec((tm, k), lambda i: (i, 0)),
                  pl.BlockSpec((k, n), lambda i: (0, 0))],
        out_specs=[pl.BlockSpec((tm, q_rank), lambda i: (i, 0)),
                   pl.BlockSpec((tm, kv_rank), lambda i: (i, 0)),
                   pl.BlockSpec((tm, LANES), lambda i: (i, 0))],
        out_shape=[jax.ShapeDtypeStruct((m, q_rank), BF16),
                   jax.ShapeDtypeStruct((m, kv_rank), BF16),
                   jax.ShapeDtypeStruct((m, LANES), F32)],
        compiler_params=_cparams(("arbitrary",)),
    )(h, w_lat)


def _rope_table_kernel(pos_ref, inv_ref, c_ref, s1_ref, s2_ref):
    half = MLA_ROPE_DIM // 2
    ang = pos_ref[0].astype(F32) * inv_ref[...]
    cos, sin = jnp.cos(ang), jnp.sin(ang)
    lane = lax.broadcasted_iota(I32, ang.shape, 1)
    c_ref[0] = jnp.where(lane < 2 * half, cos, 0.0)
    s1_ref[0] = jnp.where(lane < half, -sin, 0.0)
    s2_ref[0] = jnp.where((lane >= half) & (lane < 2 * half), sin, 0.0)


def _rope_tables(positions):
    bsz, s = positions.shape
    half = MLA_ROPE_DIM // 2
    inv = ROPE_THETA ** (-(jnp.arange(LANES, dtype=F32) % half) / half)
    ts = _tile(s, 512)
    spec = pl.BlockSpec((1, ts, LANES), lambda b, i: (b, i, 0))
    shp = jax.ShapeDtypeStruct((bsz, s, LANES), F32)
    return pl.pallas_call(
        _rope_table_kernel,
        grid=(bsz, s // ts),
        in_specs=[pl.BlockSpec((1, ts, 1), lambda b, i: (b, i, 0)),
                  pl.BlockSpec((1, LANES), lambda b, i: (0, 0))],
        out_specs=[spec, spec, spec],
        out_shape=[shp, shp, shp],
        compiler_params=_cparams(("arbitrary", "arbitrary")),
    )(positions.reshape(bsz, s, 1), inv.reshape(1, LANES))


def _rotate(r, c, s1, s2):
    half = MLA_ROPE_DIM // 2
    return r * c + pltpu.roll(r, LANES - half, 1) * s1 + pltpu.roll(r, half, 1) * s2


def _rms(x, gain):
    ms = jnp.mean(x * x, axis=-1, keepdims=True)
    return x * lax.rsqrt(ms + NORM_EPS) * gain


def _mla_q_kernel(cq_ref, g_ref, w_ref, c_ref, s1_ref, s2_ref, o_ref, *, heads, scale):
    y = _rms(cq_ref[0].astype(F32), g_ref[...]).astype(BF16)
    q = jnp.dot(y, w_ref[...], preferred_element_type=F32)
    c, s1, s2 = c_ref[0], s1_ref[0], s2_ref[0]
    for h in range(heads):
        base = h * MLA_QK_PAD
        o_ref[0, :, base:base + LANES] = (q[:, base:base + LANES] * scale).astype(BF16)
        r = q[:, base + LANES:base + 2 * LANES]
        o_ref[0, :, base + LANES:base + 2 * LANES] = (_rotate(r, c, s1, s2) * scale).astype(BF16)


def _mla_q(cq, gain, w_q, tabs, scale):
    bsz, s, qr = cq.shape
    heads = MLA_HEADS
    n = heads * MLA_QK_PAD
    tm = _tile(s, 512)
    tab_spec = pl.BlockSpec((1, tm, LANES), lambda b, i: (b, i, 0))
    return pl.pallas_call(
        functools.partial(_mla_q_kernel, heads=heads, scale=scale),
        grid=(bsz, s // tm),
        in_specs=[pl.BlockSpec((1, tm, qr), lambda b, i: (b, i, 0)),
                  pl.BlockSpec((1, qr), lambda b, i: (0, 0)),
                  pl.BlockSpec((qr, n), lambda b, i: (0, 0)),
                  tab_spec, tab_spec, tab_spec],
        out_specs=pl.BlockSpec((1, tm, n), lambda b, i: (b, i, 0)),
        out_shape=jax.ShapeDtypeStruct((bsz, s, n), BF16),
        compiler_params=_cparams(("arbitrary", "arbitrary")),
    )(cq, gain.reshape(1, qr), w_q, *tabs)


def _mla_kv_kernel(ckv_ref, g_ref, wk_ref, wv_ref, kpe_ref, c_ref, s1_ref, s2_ref, k_ref, v_ref, *, heads):
    y = _rms(ckv_ref[0].astype(F32), g_ref[...]).astype(BF16)
    kn = jnp.dot(y, wk_ref[...], preferred_element_type=F32)
    v_ref[0] = jnp.dot(y, wv_ref[...], preferred_element_type=F32).astype(BF16)
    kr = _rotate(kpe_ref[0], c_ref[0], s1_ref[0], s2_ref[0]).astype(BF16)
    for h in range(heads):
        base = h * MLA_QK_PAD
        k_ref[0, :, base:base + LANES] = kn[:, h * LANES:(h + 1) * LANES].astype(BF16)
        k_ref[0, :, base + LANES:base + 2 * LANES] = kr


def _mla_kv(ckv, gain, w_k, w_v, kpe, tabs):
    bsz, s, kvr = ckv.shape
    heads = MLA_HEADS
    tm = _tile(s, 512)
    tab_spec = pl.BlockSpec((1, tm, LANES), lambda b, i: (b, i, 0))
    return pl.pallas_call(
        functools.partial(_mla_kv_kernel, heads=heads),
        grid=(bsz, s // tm),
        in_specs=[pl.BlockSpec((1, tm, kvr), lambda b, i: (b, i, 0)),
                  pl.BlockSpec((1, kvr), lambda b, i: (0, 0)),
                  pl.BlockSpec((kvr, heads * MLA_NOPE_DIM), lambda b, i: (0, 0)),
                  pl.BlockSpec((kvr, heads * MLA_V_DIM), lambda b, i: (0, 0)),
                  tab_spec, tab_spec, tab_spec, tab_spec],
        out_specs=[pl.BlockSpec((1, tm, heads * MLA_QK_PAD), lambda b, i: (b, i, 0)),
                   pl.BlockSpec((1, tm, heads * MLA_V_DIM), lambda b, i: (b, i, 0))],
        out_shape=[jax.ShapeDtypeStruct((bsz, s, heads * MLA_QK_PAD), BF16),
                   jax.ShapeDtypeStruct((bsz, s, heads * MLA_V_DIM), BF16)],
        compiler_params=_cparams(("arbitrary", "arbitrary")),
    )(ckv, gain.reshape(1, kvr), w_k, w_v, kpe, *tabs)


def _qk(q, k):
    return lax.dot_general(q, k, (((1,), (1,)), ((), ())), preferred_element_type=F32)


def _causal_mask(s, qi, ki, tq, tk):
    row = qi * tq + lax.broadcasted_iota(I32, s.shape, 0)
    col = ki * tk + lax.broadcasted_iota(I32, s.shape, 1)
    return jnp.where(row >= col, s, -jnp.inf)


def _online_softmax_step(s, v, m_scr, l_scr, acc_scr):
    m_prev = m_scr[...]
    m_new = jnp.maximum(m_prev, jnp.max(s, axis=-1, keepdims=True))
    alpha = jnp.exp(m_prev - m_new)
    p = jnp.exp(s - m_new)
    l_scr[...] = alpha * l_scr[...] + jnp.sum(p, axis=-1, keepdims=True)
    acc_scr[...] = alpha * acc_scr[...] + jnp.dot(p.astype(BF16), v, preferred_element_type=F32)
    m_scr[...] = m_new


def _mla_attn_kernel(q_ref, k_ref, v_ref, o_ref, m_scr, l_scr, acc_scr, *, tq, tk):
    qi, ki = pl.program_id(2), pl.program_id(3)

    @pl.when(ki == 0)
    def _():
        m_scr[...] = jnp.full(m_scr.shape, -jnp.inf, F32)
        l_scr[...] = jnp.zeros(l_scr.shape, F32)
        acc_scr[...] = jnp.zeros(acc_scr.shape, F32)

    def step(masked):
        s = _qk(q_ref[0], k_ref[0])
        if masked:
            s = _causal_mask(s, qi, ki, tq, tk)
        _online_softmax_step(s, v_ref[0], m_scr, l_scr, acc_scr)

    pl.when(ki < qi)(lambda: step(False))
    pl.when(ki == qi)(lambda: step(True))

    @pl.when(ki == pl.num_programs(3) - 1)
    def _():
        o_ref[0] = (acc_scr[...] / l_scr[...]).astype(o_ref.dtype)


def _mla_attention(q, k, v):
    bsz, s, _ = q.shape
    heads = MLA_HEADS
    t = _tile(s, 512)
    nq = s // t
    return pl.pallas_call(
        functools.partial(_mla_attn_kernel, tq=t, tk=t),
        grid=(bsz, heads, nq, nq),
        in_specs=[pl.BlockSpec((1, t, MLA_QK_PAD), lambda b, h, i, j: (b, i, h)),
                  pl.BlockSpec((1, t, MLA_QK_PAD), lambda b, h, i, j: (b, jnp.minimum(i, j), h)),
                  pl.BlockSpec((1, t, MLA_V_DIM), lambda b, h, i, j: (b, jnp.minimum(i, j), h))],
        out_specs=pl.BlockSpec((1, t, MLA_V_DIM), lambda b, h, i, j: (b, i, h)),
        out_shape=jax.ShapeDtypeStruct((bsz, s, heads * MLA_V_DIM), BF16),
        scratch_shapes=[pltpu.VMEM((t, 1), F32), pltpu.VMEM((t, 1), F32),
                        pltpu.VMEM((t, MLA_V_DIM), F32)],
        compiler_params=_cparams(("arbitrary", "arbitrary", "arbitrary", "arbitrary")),
    )(q, k, v)


def _diff_attn_kernel(q_ref, k_ref, v_ref, qpos_ref, kpos_ref, slope_ref, lam_ref, subln_ref, o_ref,
                      m1, l1, a1, m2, l2, a2, *, tq, tk, scale, lam_init):
    qi, ki = pl.program_id(2), pl.program_id(3)
    d = DIFF_HEAD_DIM

    @pl.when(ki == 0)
    def _():
        for m_scr, l_scr, a_scr in ((m1, l1, a1), (m2, l2, a2)):
            m_scr[...] = jnp.full(m_scr.shape, -jnp.inf, F32)
            l_scr[...] = jnp.zeros(l_scr.shape, F32)
            a_scr[...] = jnp.zeros(a_scr.shape, F32)

    def step(masked):
        q, k, v = q_ref[0], k_ref[0], v_ref[0]
        bias = slope_ref[0, :, 0:1] * jnp.abs(qpos_ref[0] - kpos_ref[0])
        for (lo, m_scr, l_scr, a_scr) in ((0, m1, l1, a1), (d, m2, l2, a2)):
            s = _qk(q[:, lo:lo + d], k[:, lo:lo + d]) * scale - bias
            if masked:
                s = _causal_mask(s, qi, ki, tq, tk)
            _online_softmax_step(s, v, m_scr, l_scr, a_scr)

    pl.when(ki < qi)(lambda: step(False))
    pl.when(ki == qi)(lambda: step(True))

    @pl.when(ki == pl.num_programs(3) - 1)
    def _():
        lp = lam_ref[...]
        e1 = jnp.exp(jnp.sum(lp[0:1] * lp[1:2], axis=-1, keepdims=True))
        e2 = jnp.exp(jnp.sum(lp[2:3] * lp[3:4], axis=-1, keepdims=True))
        lam = e1 - e2 + lam_init
        o = a1[...] / l1[...] - lam * (a2[...] / l2[...])
        o_ref[0] = (_rms(o, subln_ref[...]) * (1.0 - lam_init)).astype(o_ref.dtype)


def _diff_attention(qkv, positions, diff_lambda, subln, lam_init):
    bsz, s, _ = qkv.shape
    heads = DIFF_HEADS
    dv = 2 * DIFF_HEAD_DIM
    t = _tile(s, 512)
    nq = s // t
    slopes = 2.0 ** (-8.0 * jnp.arange(1, heads + 1, dtype=F32) / heads)
    slopes = jnp.broadcast_to(slopes[:, None, None], (heads, 1, LANES))
    posf = positions.astype(F32)
    return pl.pallas_call(
        functools.partial(_diff_attn_kernel, tq=t, tk=t, scale=DIFF_HEAD_DIM ** -0.5, lam_init=lam_init),
        grid=(bsz, heads, nq, nq),
        in_specs=[pl.BlockSpec((1, t, dv), lambda b, h, i, j: (b, i, h)),
                  pl.BlockSpec((1, t, dv), lambda b, h, i, j: (b, jnp.minimum(i, j), heads + h)),
                  pl.BlockSpec((1, t, dv), lambda b, h, i, j: (b, jnp.minimum(i, j), 2 * heads + h)),
                  pl.BlockSpec((1, t, 1), lambda b, h, i, j: (b, i, 0)),
                  pl.BlockSpec((1, 1, t), lambda b, h, i, j: (b, 0, jnp.minimum(i, j))),
                  pl.BlockSpec((1, 1, LANES), lambda b, h, i, j: (h, 0, 0)),
                  pl.BlockSpec((4, DIFF_HEAD_DIM), lambda b, h, i, j: (0, 0)),
                  pl.BlockSpec((1, dv), lambda b, h, i, j: (0, 0))],
        out_specs=pl.BlockSpec((1, t, dv), lambda b, h, i, j: (b, i, h)),
        out_shape=jax.ShapeDtypeStruct((bsz, s, heads * dv), BF16),
        scratch_shapes=[pltpu.VMEM((t, 1), F32), pltpu.VMEM((t, 1), F32), pltpu.VMEM((t, dv), F32),
                        pltpu.VMEM((t, 1), F32), pltpu.VMEM((t, 1), F32), pltpu.VMEM((t, dv), F32)],
        compiler_params=_cparams(("arbitrary", "arbitrary", "arbitrary", "arbitrary")),
    )(qkv, qkv, qkv, posf.reshape(bsz, s, 1), posf.reshape(bsz, 1, s), slopes, diff_lambda,
      subln.reshape(1, dv))


def _merge_kernel(od_ref, om_ref, wd_ref, wm_ref, g0_ref, g1_ref, x_ref, gt_ref, o_ref):
    yd = jnp.dot(od_ref[0], wd_ref[...], preferred_element_type=F32)
    ym = jnp.dot(om_ref[0], wm_ref[...], preferred_element_type=F32)
    y = g0_ref[0].astype(F32) * yd + g1_ref[0].astype(F32) * ym
    o_ref[0] = x_ref[0] + gt_ref[0, 0] * y


def _merge(o_d, o_m, w_d, w_m, gates, x, mod, gt_idx):
    bsz, s, d = x.shape
    kd, km = o_d.shape[-1], o_m.shape[-1]
    tm, tn = _tile(s, 512), _tile(d, 512)
    nj = d // tn
    return pl.pallas_call(
        _merge_kernel,
        grid=(bsz, s // tm, nj),
        in_specs=[pl.BlockSpec((1, tm, kd), lambda b, i, j: (b, i, 0)),
                  pl.BlockSpec((1, tm, km), lambda b, i, j: (b, i, 0)),
                  pl.BlockSpec((kd, tn), lambda b, i, j: (0, j)),
                  pl.BlockSpec((km, tn), lambda b, i, j: (0, j)),
                  pl.BlockSpec((1, tm, tn), lambda b, i, j: (b, i, j)),
                  pl.BlockSpec((1, tm, tn), lambda b, i, j: (b, i, j + nj)),
                  pl.BlockSpec((1, tm, tn), lambda b, i, j: (b, i, j)),
                  pl.BlockSpec((1, 1, 1, tn), lambda b, i, j: (gt_idx, b, 0, j))],
        out_specs=pl.BlockSpec((1, tm, tn), lambda b, i, j: (b, i, j)),
        out_shape=jax.ShapeDtypeStruct((bsz, s, d), F32),
        compiler_params=_cparams(("arbitrary", "arbitrary", "arbitrary")),
    )(o_d, o_m, w_d, w_m, gates, gates, x, mod)


def _pack_rows(y):
    half = y.shape[1] // 2
    bits = lax.bitcast_convert_type(y.astype(BF16).astype(F32), U32)
    return (bits[:, half:] & jnp.uint32(0xFFFF0000)) | (bits[:, :half] >> 16)


def _unpack_rows(w):
    lo = lax.bitcast_convert_type(w << 16, F32)
    hi = lax.bitcast_convert_type(w & jnp.uint32(0xFFFF0000), F32)
    return lo, hi


def _ffn_norm_kernel(x_ref, g_ref, sc_ref, sh_ref, rw_ref, h_ref, hp_ref, lt_ref, *, nch):
    x = x_ref[0]
    y = _rms(x, g_ref[...]) * (1.0 + sc_ref[0, 0]) + sh_ref[0, 0]
    hb = y.astype(BF16)
    h_ref[0] = hb
    lt_ref[...] = lax.dot_general(rw_ref[...], hb, (((1,), (1,)), ((), ())), preferred_element_type=F32)
    packed = _pack_rows(y)
    for c in range(nch):
        hp_ref[pl.ds(c, x.shape[0], stride=nch), :] = packed[:, c * LANES:(c + 1) * LANES]


def _ffn_norm(x, gain, mod, sc_idx, sh_idx, router_wt):
    bsz, s, d = x.shape
    e = router_wt.shape[0]
    nch = d // (2 * LANES)
    tm = _tile(s, 256)
    ns = s // tm
    return pl.pallas_call(
        functools.partial(_ffn_norm_kernel, nch=nch),
        grid=(bsz, ns),
        in_specs=[pl.BlockSpec((1, tm, d), lambda b, i: (b, i, 0)),
                  pl.BlockSpec((1, d), lambda b, i: (0, 0)),
                  pl.BlockSpec((1, 1, 1, d), lambda b, i: (sc_idx, b, 0, 0)),
                  pl.BlockSpec((1, 1, 1, d), lambda b, i: (sh_idx, b, 0, 0)),
                  pl.BlockSpec((e, d), lambda b, i: (0, 0))],
        out_specs=[pl.BlockSpec((1, tm, d), lambda b, i: (b, i, 0)),
                   pl.BlockSpec((tm * nch, LANES), lambda b, i: (b * ns + i, 0)),
                   pl.BlockSpec((e, tm), lambda b, i: (0, b * ns + i))],
        out_shape=[jax.ShapeDtypeStruct((bsz, s, d), BF16),
                   jax.ShapeDtypeStruct((bsz * s * nch, LANES), U32),
                   jax.ShapeDtypeStruct((e, bsz * s), F32)],
        compiler_params=_cparams(("arbitrary", "arbitrary")),
    )(x, gain.reshape(1, d), mod, mod, router_wt)


def _first_index(hit, iota, axis, size):
    return jnp.min(jnp.where(hit, iota, size), axis=axis, keepdims=True)


def _route_kernel(lt_ref, bias_ref, idx_ref, wt_ref, rank_ref, sizes_ref, cnt_scr, *, n_exp, tn):
    i = pl.program_id(0)
    gsz = n_exp // N_GROUPS

    @pl.when(i == 0)
    def _():
        cnt_scr[...] = jnp.zeros(cnt_scr.shape, F32)

    scores = jax.nn.sigmoid(lt_ref[...])
    sel = scores + bias_ref[...]
    sel3 = sel.reshape(N_GROUPS, gsz, tn)
    j3 = lax.broadcasted_iota(I32, sel3.shape, 1)
    top1 = jnp.max(sel3, axis=1, keepdims=True)
    first = _first_index(sel3 == top1, j3, 1, gsz)
    top2 = jnp.max(jnp.where(j3 == first, -jnp.inf, sel3), axis=1, keepdims=True)
    gscore = (top1 + top2).reshape(N_GROUPS, tn)

    giota = lax.broadcasted_iota(I32, gscore.shape, 0)
    gmask = jnp.zeros(gscore.shape, jnp.bool_)
    for _ in range(TOPK_GROUPS):
        best = jnp.max(gscore, axis=0, keepdims=True)
        gi = _first_index(gscore == best, giota, 0, N_GROUPS)
        hit = giota == gi
        gmask = gmask | hit
        gscore = jnp.where(hit, -jnp.inf, gscore)

    emask = jnp.broadcast_to(gmask.reshape(N_GROUPS, 1, tn), sel3.shape)
    cand = jnp.where(emask, sel3, -jnp.inf).reshape(n_exp, tn)
    eiota = lax.broadcasted_iota(I32, cand.shape, 0)
    hits, idxs, vals = [], [], []
    for _ in range(TOP_K):
        best = jnp.max(cand, axis=0, keepdims=True)
        ei = _first_index(cand == best, eiota, 0, n_exp)
        hit = eiota == ei
        hits.append(hit)
        idxs.append(ei)
        vals.append(jnp.sum(jnp.where(hit, scores, 0.0), axis=0, keepdims=True))
        cand = jnp.where(hit, -jnp.inf, cand)
    total = functools.reduce(lambda a, b: a + b, vals)

    chosen = functools.reduce(lambda a, b: a | b, hits)
    onehot = jnp.where(chosen, 1.0, 0.0)
    r = lax.broadcasted_iota(I32, (tn, tn), 0)
    c = lax.broadcasted_iota(I32, (tn, tn), 1)
    upper = jnp.where(r < c, 1.0, 0.0).astype(BF16)
    before = cnt_scr[...] + jnp.dot(onehot.astype(BF16), upper, preferred_element_type=F32)
    cnt_new = cnt_scr[...] + jnp.sum(onehot, axis=1, keepdims=True)
    cnt_scr[...] = cnt_new

    pad = SUBLANES - TOP_K
    ranks = [jnp.sum(jnp.where(h, before, 0.0), axis=0, keepdims=True).astype(I32) for h in hits]
    zi = [jnp.zeros((pad, tn), I32)]
    idx_ref[...] = jnp.concatenate(idxs + zi, axis=0)
    rank_ref[...] = jnp.concatenate(ranks + zi, axis=0)
    wt_ref[...] = jnp.concatenate([v / total * ROUTED_SCALE for v in vals] + [jnp.zeros((pad, tn), F32)], axis=0)
    sizes_ref[...] = jnp.broadcast_to(cnt_new, sizes_ref.shape).astype(I32)


def _route(logits_t, bias):
    n_exp, n = logits_t.shape
    tn = _tile(n, 512)
    row_spec = pl.BlockSpec((SUBLANES, tn), lambda i: (0, i))
    return pl.pallas_call(
        functools.partial(_route_kernel, n_exp=n_exp, tn=tn),
        grid=(n // tn,),
        in_specs=[pl.BlockSpec((n_exp, tn), lambda i: (0, i)),
                  pl.BlockSpec((n_exp, 1), lambda i: (0, 0))],
        out_specs=[row_spec, row_spec, row_spec, pl.BlockSpec((n_exp, LANES), lambda i: (0, 0))],
        out_shape=[jax.ShapeDtypeStruct((SUBLANES, n), I32),
                   jax.ShapeDtypeStruct((SUBLANES, n), F32),
                   jax.ShapeDtypeStruct((SUBLANES, n), I32),
                   jax.ShapeDtypeStruct((n_exp, LANES), I32)],
        scratch_shapes=[pltpu.VMEM((n_exp, 1), F32)],
        compiler_params=_cparams(("arbitrary",)),
    )(logits_t, bias.reshape(n_exp, 1))


def _dispatch_kernel(dest_ref, seg_ref, hp_ref, xs_ref, zero_scr, sem, zsem, *, n_tok, tc, nch, n_exp):
    i = pl.program_id(0)

    def row_copy(t, slot):
        return pltpu.make_async_copy(hp_ref.at[pl.ds(pl.multiple_of(t * nch, nch), nch)],
                                     xs_ref.at[pl.ds(pl.multiple_of(slot * nch, nch), nch)], sem)

    def zero_copy(slot):
        return pltpu.make_async_copy(zero_scr, xs_ref.at[pl.ds(pl.multiple_of(slot * nch, nch), nch)], zsem)

    @pl.when(i == 0)
    def _():
        zero_scr[...] = jnp.zeros(zero_scr.shape, U32)

        def per_expert(e, carry):
            lo, hi = seg_ref[e], seg_ref[n_exp + e]

            def start(slot, c):
                zero_copy(slot).start()
                return c
            lax.fori_loop(lo, hi, start, 0)

            def wait(slot, c):
                zero_copy(slot).wait()
                return c
            lax.fori_loop(lo, hi, wait, 0)
            return carry
        lax.fori_loop(0, n_exp, per_expert, 0)

    base = i * tc

    def start(j, c):
        t = base + j
        for k in range(TOP_K):
            row_copy(t, dest_ref[k * n_tok + t]).start()
        return c
    lax.fori_loop(0, tc, start, 0)

    def wait(j, c):
        for k in range(TOP_K):
            row_copy(0, 0).wait()
        return c
    lax.fori_loop(0, tc, wait, 0)


def _dispatch(dest, seg, hp, rows, nch, n_exp):
    n_tok = dest.shape[0] // TOP_K
    tc = _tile(n_tok, 256)
    return pl.pallas_call(
        functools.partial(_dispatch_kernel, n_tok=n_tok, tc=tc, nch=nch, n_exp=n_exp),
        grid_spec=pltpu.PrefetchScalarGridSpec(
            num_scalar_prefetch=2,
            grid=(n_tok // tc,),
            in_specs=[pl.BlockSpec(memory_space=pl.ANY)],
            out_specs=pl.BlockSpec(memory_space=pl.ANY),
            scratch_shapes=[pltpu.VMEM((nch, LANES), U32), pltpu.SemaphoreType.DMA, pltpu.SemaphoreType.DMA]),
        out_shape=jax.ShapeDtypeStruct((rows * nch, LANES), U32),
        compiler_params=_cparams(("arbitrary",)),
    )(dest, seg, hp)


def _load_rows(ref, tm, nch):
    los, his = [], []
    for c in range(nch):
        lo, hi = _unpack_rows(ref[pl.ds(c, tm, stride=nch), :])
        los.append(lo.astype(BF16))
        his.append(hi.astype(BF16))
    return jnp.concatenate(los + his, axis=1)


def _expert_kernel(be_ref, nact_ref, xs_ref, w1_ref, w3_ref, w2_ref, ys_ref, *, tm, nch):
    @pl.when(pl.program_id(0) < nact_ref[0])
    def _():
        x = _load_rows(xs_ref, tm, nch)
        a = jnp.dot(x, w1_ref[0], preferred_element_type=F32)
        b = jnp.dot(x, w3_ref[0], preferred_element_type=F32)
        y = jnp.dot((_silu(a) * b).astype(BF16), w2_ref[0], preferred_element_type=F32)
        packed = _pack_rows(y)
        for c in range(nch):
            ys_ref[pl.ds(c, tm, stride=nch), :] = packed[:, c * LANES:(c + 1) * LANES]


def _experts(block_expert, nact, xs, w1, w3, w2, nch):
    n_exp, d, f = w1.shape
    tm = EXPERT_ROWS
    nb = xs.shape[0] // (tm * nch)
    blk = lambda i, be, na: (jnp.minimum(i, na[0] - 1), 0)
    wsel = lambda i, be, na: (be[jnp.minimum(i, na[0] - 1)], 0, 0)
    return pl.pallas_call(
        functools.partial(_expert_kernel, tm=tm, nch=nch),
        grid_spec=pltpu.PrefetchScalarGridSpec(
            num_scalar_prefetch=2,
            grid=(nb,),
            in_specs=[pl.BlockSpec((tm * nch, LANES), blk),
                      pl.BlockSpec((1, d, f), wsel),
                      pl.BlockSpec((1, d, f), wsel),
                      pl.BlockSpec((1, f, d), wsel)],
            out_specs=pl.BlockSpec((tm * nch, LANES), blk)),
        out_shape=jax.ShapeDtypeStruct(xs.shape, U32),
        compiler_params=_cparams(("arbitrary",)),
    )(block_expert, nact, xs, w1, w3, w2)


def _combine_kernel(dest_ref, ys_ref, wt_ref, h_ref, w1_ref, w3_ref, w2_ref, x_ref, gt_ref, fn_ref, o_ref,
                    ybuf, sem, *, n_tok, tm, nch, ns, final):
    b, i = pl.program_id(0), pl.program_id(1)
    base = (b * ns + i) * tm

    def row_copy(slot, k, j):
        return pltpu.make_async_copy(ys_ref.at[pl.ds(pl.multiple_of(slot * nch, nch), nch)],
                                     ybuf.at[k, pl.ds(pl.multiple_of(j * nch, nch), nch)], sem)

    def start(j, c):
        for k in range(TOP_K):
            row_copy(dest_ref[k * n_tok + base + j], k, j).start()
        return c
    lax.fori_loop(0, tm, start, 0)

    h = h_ref[0]
    a = jnp.dot(h, w1_ref[...], preferred_element_type=F32)
    g = jnp.dot(h, w3_ref[...], preferred_element_type=F32)
    ffn = jnp.dot((_silu(a) * g).astype(BF16), w2_ref[...], preferred_element_type=F32)

    def wait(j, c):
        for k in range(TOP_K):
            row_copy(0, k, 0).wait()
        return c
    lax.fori_loop(0, tm, wait, 0)

    wts = wt_ref[...]
    los = [None] * nch
    his = [None] * nch
    for k in range(TOP_K):
        wk = wts[:, k:k + 1]
        for c in range(nch):
            lo, hi = _unpack_rows(ybuf[k, pl.ds(c, tm, stride=nch), :])
            los[c] = wk * lo if k == 0 else los[c] + wk * lo
            his[c] = wk * hi if k == 0 else his[c] + wk * hi
    routed = jnp.concatenate(los + his, axis=1)
    y = x_ref[0] + gt_ref[0, 0] * (routed + ffn)
    o_ref[0] = _rms(y, fn_ref[...]) if final else y


def _combine(dest, ys, wts, h, w1, w3, w2, x, mod, gt_idx, final_norm, nch, final):
    bsz, s, d = x.shape
    f = w1.shape[1]
    n_tok = bsz * s
    tm = _tile(s, 128)
    ns = s // tm
    return pl.pallas_call(
        functools.partial(_combine_kernel, n_tok=n_tok, tm=tm, nch=nch, ns=ns, final=final),
        grid_spec=pltpu.PrefetchScalarGridSpec(
            num_scalar_prefetch=1,
            grid=(bsz, ns),
            in_specs=[pl.BlockSpec(memory_space=pl.ANY),
                      pl.BlockSpec((tm, SUBLANES), lambda b, i, dst: (b * ns + i, 0)),
                      pl.BlockSpec((1, tm, d), lambda b, i, dst: (b, i, 0)),
                      pl.BlockSpec((d, f), lambda b, i, dst: (0, 0)),
                      pl.BlockSpec((d, f), lambda b, i, dst: (0, 0)),
                      pl.BlockSpec((f, d), lambda b, i, dst: (0, 0)),
                      pl.BlockSpec((1, tm, d), lambda b, i, dst: (b, i, 0)),
                      pl.BlockSpec((1, 1, 1, d), lambda b, i, dst: (gt_idx, b, 0, 0)),
                      pl.BlockSpec((1, d), lambda b, i, dst: (0, 0))],
            out_specs=pl.BlockSpec((1, tm, d), lambda b, i, dst: (b, i, 0)),
            scratch_shapes=[pltpu.VMEM((TOP_K, tm * nch, LANES), U32), pltpu.SemaphoreType.DMA]),
        out_shape=jax.ShapeDtypeStruct((bsz, s, d), F32),
        compiler_params=_cparams(("arbitrary", "arbitrary")),
    )(dest, ys, wts, h, w1, w3, w2, x, mod, final_norm.reshape(1, d))


def _mla_weights(w_uq, w_ukv):
    heads = MLA_HEADS
    qr, kvr = w_uq.shape[0], w_ukv.shape[0]
    wq = w_uq.reshape(qr, heads, MLA_NOPE_DIM + MLA_ROPE_DIM)
    wq = jnp.pad(wq, ((0, 0), (0, 0), (0, MLA_QK_PAD - MLA_NOPE_DIM - MLA_ROPE_DIM)))
    wkv = w_ukv.reshape(kvr, heads, MLA_NOPE_DIM + MLA_V_DIM)
    wk = wkv[:, :, :MLA_NOPE_DIM].reshape(kvr, heads * MLA_NOPE_DIM)
    wv = wkv[:, :, MLA_NOPE_DIM:].reshape(kvr, heads * MLA_V_DIM)
    return (wq.reshape(qr, heads * MLA_QK_PAD).astype(BF16), wk.astype(BF16), wv.astype(BF16))


def _layer(x, mod, positions, tabs, l, norm_attn, w_in, diff_lambda, diff_subln, mla_q_norm, mla_w_uq,
           mla_kv_norm, mla_w_ukv, w_out, norm_ffn, router_w, router_bias, exp_w1, exp_w3, exp_w2,
           shared_w1, shared_w3, shared_w2, final_norm, final):
    bsz, s, d = x.shape
    n_tok = bsz * s
    q_rank, kv_rank = mla_w_uq.shape[0], mla_w_ukv.shape[0]
    qk_cols = 2 * DIFF_HEADS * DIFF_HEAD_DIM
    v_cols = DIFF_HEADS * 2 * DIFF_HEAD_DIM
    qkv_cols = 2 * qk_cols + v_cols
    lat_cols = q_rank + kv_rank + MLA_ROPE_DIM
    lam_init = 0.8 - 0.6 * math.exp(-0.3 * l)

    h = _norm_mod(x, norm_attn, mod, 1, 0).reshape(n_tok, d)
    w_qkv = w_in[:, :qkv_cols].astype(BF16)
    w_lat = jnp.pad(w_in[:, qkv_cols:qkv_cols + lat_cols], ((0, 0), (0, LANES - MLA_ROPE_DIM))).astype(BF16)
    w_gate = w_in[:, qkv_cols + lat_cols:].astype(BF16)
    qkv = _matmul(h, w_qkv, BF16).reshape(bsz, s, qkv_cols)
    cq, ckv, kpe = _latent_proj(h, w_lat, q_rank, kv_rank)
    gates = _matmul(h, w_gate, BF16, sigmoid=True).reshape(bsz, s, 2 * d)

    o_d = _diff_attention(qkv, positions, diff_lambda, diff_subln, lam_init)

    wq, wk, wv = _mla_weights(mla_w_uq, mla_w_ukv)
    scale = (MLA_NOPE_DIM + MLA_ROPE_DIM) ** -0.5
    q_m = _mla_q(cq.reshape(bsz, s, q_rank), mla_q_norm, wq, tabs, scale)
    k_m, v_m = _mla_kv(ckv.reshape(bsz, s, kv_rank), mla_kv_norm, wk, wv, kpe.reshape(bsz, s, LANES), tabs)
    o_m = _mla_attention(q_m, k_m, v_m)

    w_o = w_out.astype(BF16)
    x = _merge(o_d, o_m, w_o[:v_cols], w_o[v_cols:], gates, x, mod, 2)

    n_exp = router_w.shape[1]
    nch = d // (2 * LANES)
    h2, hp, logits_t = _ffn_norm(x, norm_ffn, mod, 4, 3, router_w.T.astype(BF16))
    idx_t, wts_t, rank_t, sizes = _route(logits_t, router_bias)

    blk = EXPERT_ROWS
    sizes = sizes[:, 0]
    padded = (sizes + blk - 1) // blk * blk
    pad_end = jnp.cumsum(padded)
    pad_start = pad_end - padded
    onehot = idx_t[:TOP_K, :, None] == jnp.arange(n_exp, dtype=I32)
    dest = (jnp.sum(jnp.where(onehot, pad_start, 0), axis=-1) + rank_t[:TOP_K]).astype(I32).reshape(-1)
    n_blocks = -(-n_tok * TOP_K // blk) + n_exp
    block_expert = jnp.minimum(
        jnp.searchsorted(pad_end, jnp.arange(n_blocks, dtype=I32) * blk, side='right'), n_exp - 1).astype(I32)
    nact = (pad_end[-1:] // blk).astype(I32)
    seg = jnp.concatenate([pad_start + sizes, pad_end]).astype(I32)

    xs = _dispatch(dest, seg, hp, n_blocks * blk, nch, n_exp)
    ys = _experts(block_expert, nact, xs, exp_w1.astype(BF16), exp_w3.astype(BF16), exp_w2.astype(BF16), nch)
    return _combine(dest, ys, wts_t.T, h2, shared_w1.astype(BF16), shared_w3.astype(BF16),
                    shared_w2.astype(BF16), x, mod, 5, final_norm, nch, final)


def kernel(x, c, positions, w_ada, b_ada, norm_attn, w_in, diff_lambda, diff_subln, mla_q_norm, mla_w_uq,
           mla_kv_norm, mla_w_ukv, w_out, norm_ffn, router_w, router_bias, exp_w1, exp_w3, exp_w2,
           shared_w1, shared_w3, shared_w2, final_norm):
    bsz, s, d = x.shape
    depth = w_ada.shape[0]
    tabs = _rope_tables(positions)
    for l in range(depth):
        mod = _ada(c, w_ada[l], b_ada[l])
        mod = mod.reshape(bsz, N_MOD, 1, d).transpose(1, 0, 2, 3)
        x = _layer(x, mod, positions, tabs, l, norm_attn[l], w_in[l], diff_lambda[l], diff_subln[l],
                   mla_q_norm[l], mla_w_uq[l], mla_kv_norm[l], mla_w_ukv[l], w_out[l], norm_ffn[l],
                   router_w[l], router_bias[l], exp_w1[l], exp_w3[l], exp_w2[l],
                   shared_w1[l], shared_w3[l], shared_w2[l], final_norm, l == depth - 1)
    return x
```

```python
import functools
import math

import jax
import jax.numpy as jnp
from jax import lax
from jax.experimental import pallas as pl
from jax.experimental.pallas import tpu as pltpu

F32 = jnp.float32
BF16 = jnp.bfloat16
U32 = jnp.uint32
I32 = jnp.int32

DIFF_HEADS = 8
DIFF_HEAD_DIM = 128
MLA_HEADS = 16
MLA_NOPE_DIM = 128
MLA_ROPE_DIM = 64
MLA_V_DIM = 128
ROPE_THETA = 10000.0
TOP_K = 6
N_GROUPS = 8
TOPK_GROUPS = 4
ROUTED_SCALE = 2.5
NORM_EPS = 1e-6
N_MOD = 6

LANES = 128
SUBLANES = 8
MLA_QK_PAD = 256
EXPERT_ROWS = 256
VMEM_LIMIT = 56 * 1024 * 1024


def _cparams(sem):
    return pltpu.CompilerParams(dimension_semantics=sem, vmem_limit_bytes=VMEM_LIMIT)


def _tile(n, pref):
    t = min(n, pref)
    assert n % t == 0, (n, pref)
    return t


def _silu(a):
    return a * jax.nn.sigmoid(a)


def _ada_kernel(c_ref, w_ref, b_ref, o_ref):
    c = c_ref[...]
    a = _silu(c).astype(BF16)
    o_ref[...] = jnp.dot(a, w_ref[...].astype(BF16), preferred_element_type=F32) + b_ref[...]


def _ada(c, w, b):
    bsz, d = c.shape
    n = w.shape[1]
    tn = _tile(n, 512)
    return pl.pallas_call(
        _ada_kernel,
        grid=(n // tn,),
        in_specs=[pl.BlockSpec((bsz, d), lambda j: (0, 0)),
                  pl.BlockSpec((d, tn), lambda j: (0, j)),
                  pl.BlockSpec((1, tn), lambda j: (0, j))],
        out_specs=pl.BlockSpec((bsz, tn), lambda j: (0, j)),
        out_shape=jax.ShapeDtypeStruct((bsz, n), F32),
        compiler_params=_cparams(("arbitrary",)),
        name="ada_mod",
    )(c, w, b.reshape(1, n))


def _norm_mod_kernel(x_ref, g_ref, sc_ref, sh_ref, o_ref):
    x = x_ref[0]
    ms = jnp.mean(x * x, axis=-1, keepdims=True)
    y = x * lax.rsqrt(ms + NORM_EPS) * g_ref[...]
    o_ref[0] = (y * (1.0 + sc_ref[0, 0]) + sh_ref[0, 0]).astype(o_ref.dtype)


def _norm_mod(x, gain, mod, sc_idx, sh_idx):
    bsz, s, d = x.shape
    tm = _tile(s, 512)
    return pl.pallas_call(
        _norm_mod_kernel,
        grid=(bsz, s // tm),
        in_specs=[pl.BlockSpec((1, tm, d), lambda b, i: (b, i, 0)),
                  pl.BlockSpec((1, d), lambda b, i: (0, 0)),
                  pl.BlockSpec((1, 1, 1, d), lambda b, i: (sc_idx, b, 0, 0)),
                  pl.BlockSpec((1, 1, 1, d), lambda b, i: (sh_idx, b, 0, 0))],
        out_specs=pl.BlockSpec((1, tm, d), lambda b, i: (b, i, 0)),
        out_shape=jax.ShapeDtypeStruct((bsz, s, d), BF16),
        compiler_params=_cparams(("arbitrary", "arbitrary")),
        name="norm_mod",
    )(x, gain.reshape(1, d), mod, mod)


def _mm_kernel(a_ref, b_ref, o_ref, *, sigmoid):
    acc = jnp.dot(a_ref[...], b_ref[...], preferred_element_type=F32)
    if sigmoid:
        acc = jax.nn.sigmoid(acc)
    o_ref[...] = acc.astype(o_ref.dtype)


def _matmul(a, b, out_dtype, *, name, sigmoid=False, tm_pref=1024, tn_pref=512):
    m, k = a.shape
    n = b.shape[1]
    tm, tn = _tile(m, tm_pref), _tile(n, tn_pref)
    return pl.pallas_call(
        functools.partial(_mm_kernel, sigmoid=sigmoid),
        grid=(m // tm, n // tn),
        in_specs=[pl.BlockSpec((tm, k), lambda i, j: (i, 0)),
                  pl.BlockSpec((k, tn), lambda i, j: (0, j))],
        out_specs=pl.BlockSpec((tm, tn), lambda i, j: (i, j)),
        out_shape=jax.ShapeDtypeStruct((m, n), out_dtype),
        compiler_params=_cparams(("arbitrary", "arbitrary")),
        name=name,
    )(a, b)


def _latent_kernel(a_ref, b_ref, cq_ref, ckv_ref, kpe_ref, *, q_rank, kv_rank):
    acc = jnp.dot(a_ref[...], b_ref[...], preferred_element_type=F32)
    cq_ref[...] = acc[:, :q_rank].astype(cq_ref.dtype)
    ckv_ref[...] = acc[:, q_rank:q_rank + kv_rank].astype(ckv_ref.dtype)
    kpe_ref[...] = acc[:, q_rank + kv_rank:]


def _latent_proj(h, w_lat, q_rank, kv_rank):
    m, k = h.shape
    n = w_lat.shape[1]
    tm = _tile(m, 512)
    return pl.pallas_call(
        functools.partial(_latent_kernel, q_rank=q_rank, kv_rank=kv_rank),
        grid=(m // tm,),
        in_specs=[pl.BlockSpec((tm, k), lambda i: (i, 0)),
                  pl.BlockSpec((k, n), lambda i: (0, 0))],
        out_specs=[pl.BlockSpec((tm, q_rank), lambda i: (i, 0)),
                   pl.BlockSpec((tm, kv_rank), lambda i: (i, 0)),
                   pl.BlockSpec((tm, LANES), lambda i: (i, 0))],
        out_shape=[jax.ShapeDtypeStruct((m, q_rank), BF16),
                   jax.ShapeDtypeStruct((m, kv_rank), BF16),
                   jax.ShapeDtypeStruct((m, LANES), F32)],
        compiler_params=_cparams(("arbitrary",)),
        name="latent_proj",
    )(h, w_lat)


def _rope_table_kernel(pos_ref, inv_ref, c_ref, s1_ref, s2_ref):
    half = MLA_ROPE_DIM // 2
    ang = pos_ref[0].astype(F32) * inv_ref[...]
    cos, sin = jnp.cos(ang), jnp.sin(ang)
    lane = lax.broadcasted_iota(I32, ang.shape, 1)
    c_ref[0] = jnp.where(lane < 2 * half, cos, 0.0)
    s1_ref[0] = jnp.where(lane < half, -sin, 0.0)
    s2_ref[0] = jnp.where((lane >= half) & (lane < 2 * half), sin, 0.0)


def _rope_tables(positions):
    bsz, s = positions.shape
    half = MLA_ROPE_DIM // 2
    inv = ROPE_THETA ** (-(jnp.arange(LANES, dtype=F32) % half) / half)
    ts = _tile(s, 512)
    spec = pl.BlockSpec((1, ts, LANES), lambda b, i: (b, i, 0))
    shp = jax.ShapeDtypeStruct((bsz, s, LANES), F32)
    return pl.pallas_call(
        _rope_table_kernel,
        grid=(bsz, s // ts),
        in_specs=[pl.BlockSpec((1, ts, 1), lambda b, i: (b, i, 0)),
                  pl.BlockSpec((1, LANES), lambda b, i: (0, 0))],
        out_specs=[spec, spec, spec],
        out_shape=[shp, shp, shp],
        compiler_params=_cparams(("arbitrary", "arbitrary")),
        name="rope_tables",
    )(positions.reshape(bsz, s, 1), inv.reshape(1, LANES))


def _rotate(r, c, s1, s2):
    half = MLA_ROPE_DIM // 2
    return r * c + pltpu.roll(r, LANES - half, 1) * s1 + pltpu.roll(r, half, 1) * s2


def _rms(x, gain):
    ms = jnp.mean(x * x, axis=-1, keepdims=True)
    return x * lax.rsqrt(ms + NORM_EPS) * gain


def _mla_q_kernel(cq_ref, g_ref, w_ref, c_ref, s1_ref, s2_ref, o_ref, *, heads, scale):
    y = _rms(cq_ref[0].astype(F32), g_ref[...]).astype(BF16)
    q = jnp.dot(y, w_ref[...], preferred_element_type=F32)
    c, s1, s2 = c_ref[0], s1_ref[0], s2_ref[0]
    for h in range(heads):
        base = h * MLA_QK_PAD
        o_ref[0, :, base:base + LANES] = (q[:, base:base + LANES] * scale).astype(BF16)
        r = q[:, base + LANES:base + 2 * LANES]
        o_ref[0, :, base + LANES:base + 2 * LANES] = (_rotate(r, c, s1, s2) * scale).astype(BF16)


def _mla_q(cq, gain, w_q, tabs, scale):
    bsz, s, qr = cq.shape
    heads = MLA_HEADS
    n = heads * MLA_QK_PAD
    tm = _tile(s, 512)
    tab_spec = pl.BlockSpec((1, tm, LANES), lambda b, i: (b, i, 0))
    return pl.pallas_call(
        functools.partial(_mla_q_kernel, heads=heads, scale=scale),
        grid=(bsz, s // tm),
        in_specs=[pl.BlockSpec((1, tm, qr), lambda b, i: (b, i, 0)),
                  pl.BlockSpec((1, qr), lambda b, i: (0, 0)),
                  pl.BlockSpec((qr, n), lambda b, i: (0, 0)),
                  tab_spec, tab_spec, tab_spec],
        out_specs=pl.BlockSpec((1, tm, n), lambda b, i: (b, i, 0)),
        out_shape=jax.ShapeDtypeStruct((bsz, s, n), BF16),
        compiler_params=_cparams(("arbitrary", "arbitrary")),
        name="mla_q_prep",
    )(cq, gain.reshape(1, qr), w_q, *tabs)


def _mla_kv_kernel(ckv_ref, g_ref, wk_ref, wv_ref, kpe_ref, c_ref, s1_ref, s2_ref, k_ref, v_ref, *, heads):
    y = _rms(ckv_ref[0].astype(F32), g_ref[...]).astype(BF16)
    kn = jnp.dot(y, wk_ref[...], preferred_element_type=F32)
    v_ref[0] = jnp.dot(y, wv_ref[...], preferred_element_type=F32).astype(BF16)
    kr = _rotate(kpe_ref[0], c_ref[0], s1_ref[0], s2_ref[0]).astype(BF16)
    for h in range(heads):
        base = h * MLA_QK_PAD
        k_ref[0, :, base:base + LANES] = kn[:, h * LANES:(h + 1) * LANES].astype(BF16)
        k_ref[0, :, base + LANES:base + 2 * LANES] = kr


def _mla_kv(ckv, gain, w_k, w_v, kpe, tabs):
    bsz, s, kvr = ckv.shape
    heads = MLA_HEADS
    tm = _tile(s, 512)
    tab_spec = pl.BlockSpec((1, tm, LANES), lambda b, i: (b, i, 0))
    return pl.pallas_call(
        functools.partial(_mla_kv_kernel, heads=heads),
        grid=(bsz, s // tm),
        in_specs=[pl.BlockSpec((1, tm, kvr), lambda b, i: (b, i, 0)),
                  pl.BlockSpec((1, kvr), lambda b, i: (0, 0)),
                  pl.BlockSpec((kvr, heads * MLA_NOPE_DIM), lambda b, i: (0, 0)),
                  pl.BlockSpec((kvr, heads * MLA_V_DIM), lambda b, i: (0, 0)),
                  tab_spec, tab_spec, tab_spec, tab_spec],
        out_specs=[pl.BlockSpec((1, tm, heads * MLA_QK_PAD), lambda b, i: (b, i, 0)),
                   pl.BlockSpec((1, tm, heads * MLA_V_DIM), lambda b, i: (b, i, 0))],
        out_shape=[jax.ShapeDtypeStruct((bsz, s, heads * MLA_QK_PAD), BF16),
                   jax.ShapeDtypeStruct((bsz, s, heads * MLA_V_DIM), BF16)],
        compiler_params=_cparams(("arbitrary", "arbitrary")),
        name="mla_kv_prep",
    )(ckv, gain.reshape(1, kvr), w_k, w_v, kpe, *tabs)


def _qk(q, k):
    return lax.dot_general(q, k, (((1,), (1,)), ((), ())), preferred_element_type=F32)


def _causal_mask(s, qi, ki, tq, tk):
    row = qi * tq + lax.broadcasted_iota(I32, s.shape, 0)
    col = ki * tk + lax.broadcasted_iota(I32, s.shape, 1)
    return jnp.where(row >= col, s, -jnp.inf)


def _online_softmax_step(s, v, m_scr, l_scr, acc_scr):
    m_prev = m_scr[...]
    m_new = jnp.maximum(m_prev, jnp.max(s, axis=-1, keepdims=True))
    alpha = jnp.exp(m_prev - m_new)
    p = jnp.exp(s - m_new)
    l_scr[...] = alpha * l_scr[...] + jnp.sum(p, axis=-1, keepdims=True)
    acc_scr[...] = alpha * acc_scr[...] + jnp.dot(p.astype(BF16), v, preferred_element_type=F32)
    m_scr[...] = m_new


def _mla_attn_kernel(q_ref, k_ref, v_ref, o_ref, m_scr, l_scr, acc_scr, *, tq, tk):
    qi, ki = pl.program_id(2), pl.program_id(3)

    @pl.when(ki == 0)
    def _():
        m_scr[...] = jnp.full(m_scr.shape, -jnp.inf, F32)
        l_scr[...] = jnp.zeros(l_scr.shape, F32)
        acc_scr[...] = jnp.zeros(acc_scr.shape, F32)

    def step(masked):
        s = _qk(q_ref[0], k_ref[0])
        if masked:
            s = _causal_mask(s, qi, ki, tq, tk)
        _online_softmax_step(s, v_ref[0], m_scr, l_scr, acc_scr)

    pl.when(ki < qi)(lambda: step(False))
    pl.when(ki == qi)(lambda: step(True))

    @pl.when(ki == pl.num_programs(3) - 1)
    def _():
        o_ref[0] = (acc_scr[...] / l_scr[...]).astype(o_ref.dtype)


def _mla_attention(q, k, v):
    bsz, s, _ = q.shape
    heads = MLA_HEADS
    t = _tile(s, 512)
    nq = s // t
    return pl.pallas_call(
        functools.partial(_mla_attn_kernel, tq=t, tk=t),
        grid=(bsz, heads, nq, nq),
        in_specs=[pl.BlockSpec((1, t, MLA_QK_PAD), lambda b, h, i, j: (b, i, h)),
                  pl.BlockSpec((1, t, MLA_QK_PAD), lambda b, h, i, j: (b, jnp.minimum(i, j), h)),
                  pl.BlockSpec((1, t, MLA_V_DIM), lambda b, h, i, j: (b, jnp.minimum(i, j), h))],
        out_specs=pl.BlockSpec((1, t, MLA_V_DIM), lambda b, h, i, j: (b, i, h)),
        out_shape=jax.ShapeDtypeStruct((bsz, s, heads * MLA_V_DIM), BF16),
        scratch_shapes=[pltpu.VMEM((t, 1), F32), pltpu.VMEM((t, 1), F32),
                        pltpu.VMEM((t, MLA_V_DIM), F32)],
        compiler_params=_cparams(("arbitrary", "arbitrary", "arbitrary", "arbitrary")),
        name="mla_attn",
    )(q, k, v)


def _diff_attn_kernel(q_ref, k_ref, v_ref, qpos_ref, kpos_ref, slope_ref, lam_ref, subln_ref, o_ref,
                      m1, l1, a1, m2, l2, a2, *, tq, tk, scale, lam_init):
    qi, ki = pl.program_id(2), pl.program_id(3)
    d = DIFF_HEAD_DIM

    @pl.when(ki == 0)
    def _():
        for m_scr, l_scr, a_scr in ((m1, l1, a1), (m2, l2, a2)):
            m_scr[...] = jnp.full(m_scr.shape, -jnp.inf, F32)
            l_scr[...] = jnp.zeros(l_scr.shape, F32)
            a_scr[...] = jnp.zeros(a_scr.shape, F32)

    def step(masked):
        q, k, v = q_ref[0], k_ref[0], v_ref[0]
        bias = slope_ref[0, :, 0:1] * jnp.abs(qpos_ref[0] - kpos_ref[0])
        for (lo, m_scr, l_scr, a_scr) in ((0, m1, l1, a1), (d, m2, l2, a2)):
            s = _qk(q[:, lo:lo + d], k[:, lo:lo + d]) * scale - bias
            if masked:
                s = _causal_mask(s, qi, ki, tq, tk)
            _online_softmax_step(s, v, m_scr, l_scr, a_scr)

    pl.when(ki < qi)(lambda: step(False))
    pl.when(ki == qi)(lambda: step(True))

    @pl.when(ki == pl.num_programs(3) - 1)
    def _():
        lp = lam_ref[...]
        e1 = jnp.exp(jnp.sum(lp[0:1] * lp[1:2], axis=-1, keepdims=True))
        e2 = jnp.exp(jnp.sum(lp[2:3] * lp[3:4], axis=-1, keepdims=True))
        lam = e1 - e2 + lam_init
        o = a1[...] / l1[...] - lam * (a2[...] / l2[...])
        o_ref[0] = (_rms(o, subln_ref[...]) * (1.0 - lam_init)).astype(o_ref.dtype)


def _diff_attention(qkv, positions, diff_lambda, subln, lam_init):
    bsz, s, _ = qkv.shape
    heads = DIFF_HEADS
    dv = 2 * DIFF_HEAD_DIM
    t = _tile(s, 512)
    nq = s // t
    slopes = 2.0 ** (-8.0 * jnp.arange(1, heads + 1, dtype=F32) / heads)
    slopes = jnp.broadcast_to(slopes[:, None, None], (heads, 1, LANES))
    posf = positions.astype(F32)
    return pl.pallas_call(
        functools.partial(_diff_attn_kernel, tq=t, tk=t, scale=DIFF_HEAD_DIM ** -0.5, lam_init=lam_init),
        grid=(bsz, heads, nq, nq),
        in_specs=[pl.BlockSpec((1, t, dv), lambda b, h, i, j: (b, i, h)),
                  pl.BlockSpec((1, t, dv), lambda b, h, i, j: (b, jnp.minimum(i, j), heads + h)),
                  pl.BlockSpec((1, t, dv), lambda b, h, i, j: (b, jnp.minimum(i, j), 2 * heads + h)),
                  pl.BlockSpec((1, t, 1), lambda b, h, i, j: (b, i, 0)),
                  pl.BlockSpec((1, 1, t), lambda b, h, i, j: (b, 0, jnp.minimum(i, j))),
                  pl.BlockSpec((1, 1, LANES), lambda b, h, i, j: (h, 0, 0)),
                  pl.BlockSpec((4, DIFF_HEAD_DIM), lambda b, h, i, j: (0, 0)),
                  pl.BlockSpec((1, dv), lambda b, h, i, j: (0, 0))],
        out_specs=pl.BlockSpec((1, t, dv), lambda b, h, i, j: (b, i, h)),
        out_shape=jax.ShapeDtypeStruct((bsz, s, heads * dv), BF16),
        scratch_shapes=[pltpu.VMEM((t, 1), F32), pltpu.VMEM((t, 1), F32), pltpu.VMEM((t, dv), F32),
                        pltpu.VMEM((t, 1), F32), pltpu.VMEM((t, 1), F32), pltpu.VMEM((t, dv), F32)],
        compiler_params=_cparams(("arbitrary", "arbitrary", "arbitrary", "arbitrary")),
        name="diff_attn",
    )(qkv, qkv, qkv, posf.reshape(bsz, s, 1), posf.reshape(bsz, 1, s), slopes, diff_lambda,
      subln.reshape(1, dv))


def _merge_kernel(od_ref, om_ref, wd_ref, wm_ref, g0_ref, g1_ref, x_ref, gt_ref, o_ref):
    yd = jnp.dot(od_ref[0], wd_ref[...], preferred_element_type=F32)
    ym = jnp.dot(om_ref[0], wm_ref[...], preferred_element_type=F32)
    y = g0_ref[0].astype(F32) * yd + g1_ref[0].astype(F32) * ym
    o_ref[0] = x_ref[0] + gt_ref[0, 0] * y


def _merge(o_d, o_m, w_d, w_m, gates, x, mod, gt_idx):
    bsz, s, d = x.shape
    kd, km = o_d.shape[-1], o_m.shape[-1]
    tm, tn = _tile(s, 512), _tile(d, 512)
    nj = d // tn
    return pl.pallas_call(
        _merge_kernel,
        grid=(bsz, s // tm, nj),
        in_specs=[pl.BlockSpec((1, tm, kd), lambda b, i, j: (b, i, 0)),
                  pl.BlockSpec((1, tm, km), lambda b, i, j: (b, i, 0)),
                  pl.BlockSpec((kd, tn), lambda b, i, j: (0, j)),
                  pl.BlockSpec((km, tn), lambda b, i, j: (0, j)),
                  pl.BlockSpec((1, tm, tn), lambda b, i, j: (b, i, j)),
                  pl.BlockSpec((1, tm, tn), lambda b, i, j: (b, i, j + nj)),
                  pl.BlockSpec((1, tm, tn), lambda b, i, j: (b, i, j)),
                  pl.BlockSpec((1, 1, 1, tn), lambda b, i, j: (gt_idx, b, 0, j))],
        out_specs=pl.BlockSpec((1, tm, tn), lambda b, i, j: (b, i, j)),
        out_shape=jax.ShapeDtypeStruct((bsz, s, d), F32),
        compiler_params=_cparams(("arbitrary", "arbitrary", "arbitrary")),
        name="out_merge",
    )(o_d, o_m, w_d, w_m, gates, gates, x, mod)


def _pack_rows(y):
    half = y.shape[1] // 2
    bits = lax.bitcast_convert_type(y.astype(BF16).astype(F32), U32)
    return (bits[:, half:] & jnp.uint32(0xFFFF0000)) | (bits[:, :half] >> 16)


def _unpack_rows(w):
    lo = lax.bitcast_convert_type(w << 16, F32)
    hi = lax.bitcast_convert_type(w & jnp.uint32(0xFFFF0000), F32)
    return lo, hi


def _ffn_norm_kernel(x_ref, g_ref, sc_ref, sh_ref, rw_ref, h_ref, hp_ref, lt_ref, *, nch):
    x = x_ref[0]
    y = _rms(x, g_ref[...]) * (1.0 + sc_ref[0, 0]) + sh_ref[0, 0]
    hb = y.astype(BF16)
    h_ref[0] = hb
    lt_ref[...] = lax.dot_general(rw_ref[...], hb, (((1,), (1,)), ((), ())), preferred_element_type=F32)
    packed = _pack_rows(y)
    for c in range(nch):
        hp_ref[pl.ds(c, x.shape[0], stride=nch), :] = packed[:, c * LANES:(c + 1) * LANES]


def _ffn_norm(x, gain, mod, sc_idx, sh_idx, router_wt):
    bsz, s, d = x.shape
    e = router_wt.shape[0]
    nch = d // (2 * LANES)
    tm = _tile(s, 256)
    ns = s // tm
    return pl.pallas_call(
        functools.partial(_ffn_norm_kernel, nch=nch),
        grid=(bsz, ns),
        in_specs=[pl.BlockSpec((1, tm, d), lambda b, i: (b, i, 0)),
                  pl.BlockSpec((1, d), lambda b, i: (0, 0)),
                  pl.BlockSpec((1, 1, 1, d), lambda b, i: (sc_idx, b, 0, 0)),
                  pl.BlockSpec((1, 1, 1, d), lambda b, i: (sh_idx, b, 0, 0)),
                  pl.BlockSpec((e, d), lambda b, i: (0, 0))],
        out_specs=[pl.BlockSpec((1, tm, d), lambda b, i: (b, i, 0)),
                   pl.BlockSpec((tm * nch, LANES), lambda b, i: (b * ns + i, 0)),
                   pl.BlockSpec((e, tm), lambda b, i: (0, b * ns + i))],
        out_shape=[jax.ShapeDtypeStruct((bsz, s, d), BF16),
                   jax.ShapeDtypeStruct((bsz * s * nch, LANES), U32),
                   jax.ShapeDtypeStruct((e, bsz * s), F32)],
        compiler_params=_cparams(("arbitrary", "arbitrary")),
        name="ffn_norm_router",
    )(x, gain.reshape(1, d), mod, mod, router_wt)


def _first_index(hit, iota, axis, size):
    return jnp.min(jnp.where(hit, iota, size), axis=axis, keepdims=True)


def _route_kernel(lt_ref, bias_ref, idx_ref, wt_ref, rank_ref, sizes_ref, cnt_scr, *, n_exp, tn):
    i = pl.program_id(0)
    gsz = n_exp // N_GROUPS

    @pl.when(i == 0)
    def _():
        cnt_scr[...] = jnp.zeros(cnt_scr.shape, F32)

    scores = jax.nn.sigmoid(lt_ref[...])
    sel = scores + bias_ref[...]
    sel3 = sel.reshape(N_GROUPS, gsz, tn)
    j3 = lax.broadcasted_iota(I32, sel3.shape, 1)
    top1 = jnp.max(sel3, axis=1, keepdims=True)
    first = _first_index(sel3 == top1, j3, 1, gsz)
    top2 = jnp.max(jnp.where(j3 == first, -jnp.inf, sel3), axis=1, keepdims=True)
    gscore = (top1 + top2).reshape(N_GROUPS, tn)

    giota = lax.broadcasted_iota(I32, gscore.shape, 0)
    gmask = jnp.zeros(gscore.shape, jnp.bool_)
    for _ in range(TOPK_GROUPS):
        best = jnp.max(gscore, axis=0, keepdims=True)
        gi = _first_index(gscore == best, giota, 0, N_GROUPS)
        hit = giota == gi
        gmask = gmask | hit
        gscore = jnp.where(hit, -jnp.inf, gscore)

    emask = jnp.broadcast_to(gmask.reshape(N_GROUPS, 1, tn), sel3.shape)
    cand = jnp.where(emask, sel3, -jnp.inf).reshape(n_exp, tn)
    eiota = lax.broadcasted_iota(I32, cand.shape, 0)
    hits, idxs, vals = [], [], []
    for _ in range(TOP_K):
        best = jnp.max(cand, axis=0, keepdims=True)
        ei = _first_index(cand == best, eiota, 0, n_exp)
        hit = eiota == ei
        hits.append(hit)
        idxs.append(ei)
        vals.append(jnp.sum(jnp.where(hit, scores, 0.0), axis=0, keepdims=True))
        cand = jnp.where(hit, -jnp.inf, cand)
    total = functools.reduce(lambda a, b: a + b, vals)

    chosen = functools.reduce(lambda a, b: a | b, hits)
    onehot = jnp.where(chosen, 1.0, 0.0)
    r = lax.broadcasted_iota(I32, (tn, tn), 0)
    c = lax.broadcasted_iota(I32, (tn, tn), 1)
    upper = jnp.where(r < c, 1.0, 0.0).astype(BF16)
    before = cnt_scr[...] + jnp.dot(onehot.astype(BF16), upper, preferred_element_type=F32)
    cnt_new = cnt_scr[...] + jnp.sum(onehot, axis=1, keepdims=True)
    cnt_scr[...] = cnt_new

    pad = SUBLANES - TOP_K
    ranks = [jnp.sum(jnp.where(h, before, 0.0), axis=0, keepdims=True).astype(I32) for h in hits]
    zi = [jnp.zeros((pad, tn), I32)]
    idx_ref[...] = jnp.concatenate(idxs + zi, axis=0)
    rank_ref[...] = jnp.concatenate(ranks + zi, axis=0)
    wt_ref[...] = jnp.concatenate([v / total * ROUTED_SCALE for v in vals] + [jnp.zeros((pad, tn), F32)], axis=0)
    sizes_ref[...] = jnp.broadcast_to(cnt_new, sizes_ref.shape).astype(I32)


def _route(logits_t, bias):
    n_exp, n = logits_t.shape
    tn = _tile(n, 512)
    row_spec = pl.BlockSpec((SUBLANES, tn), lambda i: (0, i))
    return pl.pallas_call(
        functools.partial(_route_kernel, n_exp=n_exp, tn=tn),
        grid=(n // tn,),
        in_specs=[pl.BlockSpec((n_exp, tn), lambda i: (0, i)),
                  pl.BlockSpec((n_exp, 1), lambda i: (0, 0))],
        out_specs=[row_spec, row_spec, row_spec, pl.BlockSpec((n_exp, LANES), lambda i: (0, 0))],
        out_shape=[jax.ShapeDtypeStruct((SUBLANES, n), I32),
                   jax.ShapeDtypeStruct((SUBLANES, n), F32),
                   jax.ShapeDtypeStruct((SUBLANES, n), I32),
                   jax.ShapeDtypeStruct((n_exp, LANES), I32)],
        scratch_shapes=[pltpu.VMEM((n_exp, 1), F32)],
        compiler_params=_cparams(("arbitrary",)),
        name="route_topk",
    )(logits_t, bias.reshape(n_exp, 1))


def _dispatch_kernel(dest_ref, seg_ref, hp_ref, xs_ref, zero_scr, sem, zsem, *, n_tok, tc, nch, n_exp, blk, nb):
    i = pl.program_id(0)

    def row_copy(j, slot):
        return pltpu.make_async_copy(hp_ref.at[pl.ds(pl.multiple_of(j * nch, nch), nch)],
                                     xs_ref.at[pl.ds(pl.multiple_of(slot * nch, nch), nch)], sem)

    def zero_row(slot):
        return pltpu.make_async_copy(zero_scr.at[pl.ds(0, nch)],
                                     xs_ref.at[pl.ds(pl.multiple_of(slot * nch, nch), nch)], zsem)

    def zero_block(b):
        return pltpu.make_async_copy(zero_scr,
                                     xs_ref.at[pl.ds(pl.multiple_of(b * (blk * nch), blk * nch), blk * nch)], zsem)

    def start_then_wait(lo, hi, copy):
        def start(v, c):
            copy(v).start()
            return c
        lax.fori_loop(lo, hi, start, 0)

        def wait(v, c):
            copy(v).wait()
            return c
        lax.fori_loop(lo, hi, wait, 0)

    @pl.when(i == 0)
    def _():
        zero_scr[...] = jnp.zeros(zero_scr.shape, U32)

        def per_expert(e, carry):
            start_then_wait(seg_ref[e], seg_ref[n_exp + e], zero_row)
            return carry
        lax.fori_loop(0, n_exp, per_expert, 0)
        start_then_wait(seg_ref[2 * n_exp], nb, zero_block)

    base = i * tc

    def start(j, c):
        for k in range(TOP_K):
            row_copy(j, dest_ref[k * n_tok + base + j]).start()
        return c
    lax.fori_loop(0, tc, start, 0)

    def wait(j, c):
        for k in range(TOP_K):
            row_copy(0, 0).wait()
        return c
    lax.fori_loop(0, tc, wait, 0)


def _dispatch(dest, seg, hp, nb, blk, nch, n_exp):
    n_tok = dest.shape[0] // TOP_K
    tc = _tile(n_tok, 256)
    return pl.pallas_call(
        functools.partial(_dispatch_kernel, n_tok=n_tok, tc=tc, nch=nch, n_exp=n_exp, blk=blk, nb=nb),
        grid_spec=pltpu.PrefetchScalarGridSpec(
            num_scalar_prefetch=2,
            grid=(n_tok // tc,),
            in_specs=[pl.BlockSpec((tc * nch, LANES), lambda i, dst, sg: (i, 0))],
            out_specs=pl.BlockSpec(memory_space=pl.ANY),
            scratch_shapes=[pltpu.VMEM((blk * nch, LANES), U32), pltpu.SemaphoreType.DMA,
                            pltpu.SemaphoreType.DMA]),
        out_shape=jax.ShapeDtypeStruct((nb * blk * nch, LANES), U32),
        compiler_params=_cparams(("arbitrary",)),
        name="moe_dispatch",
    )(dest, seg, hp)


def _load_rows(ref, tm, nch):
    los, his = [], []
    for c in range(nch):
        lo, hi = _unpack_rows(ref[pl.ds(c, tm, stride=nch), :])
        los.append(lo.astype(BF16))
        his.append(hi.astype(BF16))
    return jnp.concatenate(los + his, axis=1)


def _expert_kernel(be_ref, nact_ref, xs_ref, w1_ref, w3_ref, w2_ref, ys_ref, *, tm, nch):
    active = pl.program_id(0) < nact_ref[0]

    @pl.when(jnp.logical_not(active))
    def _():
        ys_ref[...] = jnp.zeros(ys_ref.shape, U32)

    @pl.when(active)
    def _():
        x = _load_rows(xs_ref, tm, nch)
        a = jnp.dot(x, w1_ref[0], preferred_element_type=F32)
        b = jnp.dot(x, w3_ref[0], preferred_element_type=F32)
        y = jnp.dot((_silu(a) * b).astype(BF16), w2_ref[0], preferred_element_type=F32)
        packed = _pack_rows(y)
        for c in range(nch):
            ys_ref[pl.ds(c, tm, stride=nch), :] = packed[:, c * LANES:(c + 1) * LANES]


def _experts(block_expert, nact, xs, w1, w3, w2, nch):
    n_exp, d, f = w1.shape
    tm = EXPERT_ROWS
    nb = xs.shape[0] // (tm * nch)
    blk = lambda i, be, na: (jnp.minimum(i, na[0] - 1), 0)
    wsel = lambda i, be, na: (be[jnp.minimum(i, na[0] - 1)], 0, 0)
    return pl.pallas_call(
        functools.partial(_expert_kernel, tm=tm, nch=nch),
        grid_spec=pltpu.PrefetchScalarGridSpec(
            num_scalar_prefetch=2,
            grid=(nb,),
            in_specs=[pl.BlockSpec((tm * nch, LANES), blk),
                      pl.BlockSpec((1, d, f), wsel),
                      pl.BlockSpec((1, d, f), wsel),
                      pl.BlockSpec((1, f, d), wsel)],
            out_specs=pl.BlockSpec((tm * nch, LANES), lambda i, be, na: (i, 0))),
        out_shape=jax.ShapeDtypeStruct(xs.shape, U32),
        compiler_params=_cparams(("arbitrary",)),
        name="moe_experts",
    )(block_expert, nact, xs, w1, w3, w2)


def _combine_kernel(dest_ref, ys_ref, wt_ref, h_ref, w1_ref, w3_ref, w2_ref, x_ref, gt_ref, fn_ref, o_ref,
                    ybuf, sem, *, n_tok, tm, nch, ns, final):
    b, i = pl.program_id(0), pl.program_id(1)
    base = (b * ns + i) * tm

    def row_copy(slot, k, j):
        return pltpu.make_async_copy(ys_ref.at[pl.ds(pl.multiple_of(slot * nch, nch), nch)],
                                     ybuf.at[k, pl.ds(pl.multiple_of(j * nch, nch), nch)], sem)

    def start(j, c):
        for k in range(TOP_K):
            row_copy(dest_ref[k * n_tok + base + j], k, j).start()
        return c
    lax.fori_loop(0, tm, start, 0)

    h = h_ref[0]
    a = jnp.dot(h, w1_ref[...], preferred_element_type=F32)
    g = jnp.dot(h, w3_ref[...], preferred_element_type=F32)
    ffn = jnp.dot((_silu(a) * g).astype(BF16), w2_ref[...], preferred_element_type=F32)

    def wait(j, c):
        for k in range(TOP_K):
            row_copy(0, k, 0).wait()
        return c
    lax.fori_loop(0, tm, wait, 0)

    wts = wt_ref[...]
    los = [None] * nch
    his = [None] * nch
    for k in range(TOP_K):
        wk = wts[:, k:k + 1]
        for c in range(nch):
            lo, hi = _unpack_rows(ybuf[k, pl.ds(c, tm, stride=nch), :])
            los[c] = wk * lo if k == 0 else los[c] + wk * lo
            his[c] = wk * hi if k == 0 else his[c] + wk * hi
    routed = jnp.concatenate(los + his, axis=1)
    y = x_ref[0] + gt_ref[0, 0] * (routed + ffn)
    o_ref[0] = _rms(y, fn_ref[...]) if final else y


def _combine(dest, ys, wts, h, w1, w3, w2, x, mod, gt_idx, final_norm, nch, final):
    bsz, s, d = x.shape
    f = w1.shape[1]
    n_tok = bsz * s
    tm = _tile(s, 128)
    ns = s // tm
    return pl.pallas_call(
        functools.partial(_combine_kernel, n_tok=n_tok, tm=tm, nch=nch, ns=ns, final=final),
        grid_spec=pltpu.PrefetchScalarGridSpec(
            num_scalar_prefetch=1,
            grid=(bsz, ns),
            in_specs=[pl.BlockSpec(memory_space=pl.ANY),
                      pl.BlockSpec((tm, SUBLANES), lambda b, i, dst: (b * ns + i, 0)),
                      pl.BlockSpec((1, tm, d), lambda b, i, dst: (b, i, 0)),
                      pl.BlockSpec((d, f), lambda b, i, dst: (0, 0)),
                      pl.BlockSpec((d, f), lambda b, i, dst: (0, 0)),
                      pl.BlockSpec((f, d), lambda b, i, dst: (0, 0)),
                      pl.BlockSpec((1, tm, d), lambda b, i, dst: (b, i, 0)),
                      pl.BlockSpec((1, 1, 1, d), lambda b, i, dst: (gt_idx, b, 0, 0)),
                      pl.BlockSpec((1, d), lambda b, i, dst: (0, 0))],
            out_specs=pl.BlockSpec((1, tm, d), lambda b, i, dst: (b, i, 0)),
            scratch_shapes=[pltpu.VMEM((TOP_K, tm * nch, LANES), U32), pltpu.SemaphoreType.DMA]),
        out_shape=jax.ShapeDtypeStruct((bsz, s, d), F32),
        compiler_params=_cparams(("arbitrary", "arbitrary")),
        name="moe_combine",
    )(dest, ys, wts, h, w1, w3, w2, x, mod, final_norm.reshape(1, d))


def _mla_weights(w_uq, w_ukv):
    heads = MLA_HEADS
    qr, kvr = w_uq.shape[0], w_ukv.shape[0]
    wq = w_uq.reshape(qr, heads, MLA_NOPE_DIM + MLA_ROPE_DIM)
    wq = jnp.pad(wq, ((0, 0), (0, 0), (0, MLA_QK_PAD - MLA_NOPE_DIM - MLA_ROPE_DIM)))
    wkv = w_ukv.reshape(kvr, heads, MLA_NOPE_DIM + MLA_V_DIM)
    wk = wkv[:, :, :MLA_NOPE_DIM].reshape(kvr, heads * MLA_NOPE_DIM)
    wv = wkv[:, :, MLA_NOPE_DIM:].reshape(kvr, heads * MLA_V_DIM)
    return (wq.reshape(qr, heads * MLA_QK_PAD).astype(BF16), wk.astype(BF16), wv.astype(BF16))


def _layer(x, mod, positions, tabs, l, norm_attn, w_in, diff_lambda, diff_subln, mla_q_norm, mla_w_uq,
           mla_kv_norm, mla_w_ukv, w_out, norm_ffn, router_w, router_bias, exp_w1, exp_w3, exp_w2,
           shared_w1, shared_w3, shared_w2, final_norm, final):
    bsz, s, d = x.shape
    n_tok = bsz * s
    q_rank, kv_rank = mla_w_uq.shape[0], mla_w_ukv.shape[0]
    qk_cols = 2 * DIFF_HEADS * DIFF_HEAD_DIM
    v_cols = DIFF_HEADS * 2 * DIFF_HEAD_DIM
    qkv_cols = 2 * qk_cols + v_cols
    lat_cols = q_rank + kv_rank + MLA_ROPE_DIM
    lam_init = 0.8 - 0.6 * math.exp(-0.3 * l)

    h = _norm_mod(x, norm_attn, mod, 1, 0).reshape(n_tok, d)
    w_qkv = w_in[:, :qkv_cols].astype(BF16)
    w_lat = jnp.pad(w_in[:, qkv_cols:qkv_cols + lat_cols], ((0, 0), (0, LANES - MLA_ROPE_DIM))).astype(BF16)
    w_gate = w_in[:, qkv_cols + lat_cols:].astype(BF16)
    qkv = _matmul(h, w_qkv, BF16, name="qkv_proj").reshape(bsz, s, qkv_cols)
    cq, ckv, kpe = _latent_proj(h, w_lat, q_rank, kv_rank)
    gates = _matmul(h, w_gate, BF16, name="gate_proj", sigmoid=True).reshape(bsz, s, 2 * d)

    o_d = _diff_attention(qkv, positions, diff_lambda, diff_subln, lam_init)

    wq, wk, wv = _mla_weights(mla_w_uq, mla_w_ukv)
    scale = (MLA_NOPE_DIM + MLA_ROPE_DIM) ** -0.5
    q_m = _mla_q(cq.reshape(bsz, s, q_rank), mla_q_norm, wq, tabs, scale)
    k_m, v_m = _mla_kv(ckv.reshape(bsz, s, kv_rank), mla_kv_norm, wk, wv, kpe.reshape(bsz, s, LANES), tabs)
    o_m = _mla_attention(q_m, k_m, v_m)

    w_o = w_out.astype(BF16)
    x = _merge(o_d, o_m, w_o[:v_cols], w_o[v_cols:], gates, x, mod, 2)

    n_exp = router_w.shape[1]
    nch = d // (2 * LANES)
    h2, hp, logits_t = _ffn_norm(x, norm_ffn, mod, 4, 3, router_w.T.astype(BF16))
    idx_t, wts_t, rank_t, sizes = _route(logits_t, router_bias)

    blk = EXPERT_ROWS
    sizes = sizes[:, 0]
    padded = (sizes + blk - 1) // blk * blk
    pad_end = jnp.cumsum(padded)
    pad_start = pad_end - padded
    onehot = idx_t[:TOP_K, :, None] == jnp.arange(n_exp, dtype=I32)
    dest = (jnp.sum(jnp.where(onehot, pad_start, 0), axis=-1) + rank_t[:TOP_K]).astype(I32).reshape(-1)
    n_blocks = -(-n_tok * TOP_K // blk) + n_exp
    block_start = jnp.arange(n_blocks, dtype=I32) * blk
    block_expert = jnp.minimum(jnp.sum(pad_end[None, :] <= block_start[:, None], axis=1), n_exp - 1).astype(I32)
    nact = (pad_end[-1:] // blk).astype(I32)
    seg = jnp.concatenate([pad_start + sizes, pad_end, nact]).astype(I32)

    xs = _dispatch(dest, seg, hp, n_blocks, blk, nch, n_exp)
    ys = _experts(block_expert, nact, xs, exp_w1.astype(BF16), exp_w3.astype(BF16), exp_w2.astype(BF16), nch)
    return _combine(dest, ys, wts_t.T, h2, shared_w1.astype(BF16), shared_w3.astype(BF16),
                    shared_w2.astype(BF16), x, mod, 5, final_norm, nch, final)


def kernel(x, c, positions, w_ada, b_ada, norm_attn, w_in, diff_lambda, diff_subln, mla_q_norm, mla_w_uq,
           mla_kv_norm, mla_w_ukv, w_out, norm_ffn, router_w, router_bias, exp_w1, exp_w3, exp_w2,
           shared_w1, shared_w3, shared_w2, final_norm):
    bsz, s, d = x.shape
    depth = w_ada.shape[0]
    tabs = _rope_tables(positions)
    for l in range(depth):
        mod = _ada(c, w_ada[l], b_ada[l])
        mod = mod.reshape(bsz, N_MOD, 1, d).transpose(1, 0, 2, 3)
        x = _layer(x, mod, positions, tabs, l, norm_attn[l], w_in[l], diff_lambda[l], diff_subln[l],
                   mla_q_norm[l], mla_w_uq[l], mla_kv_norm[l], mla_w_ukv[l], w_out[l], norm_ffn[l],
                   router_w[l], router_bias[l], exp_w1[l], exp_w3[l], exp_w2[l],
                   shared_w1[l], shared_w3[l], shared_w2[l], final_norm, l == depth - 1)
    return x
```

```python
import functools
import math

import jax
import jax.numpy as jnp
from jax import lax
from jax.experimental import pallas as pl
from jax.experimental.pallas import tpu as pltpu

F32 = jnp.float32
BF16 = jnp.bfloat16
U32 = jnp.uint32
I32 = jnp.int32

DIFF_HEADS = 8
DIFF_HEAD_DIM = 128
MLA_HEADS = 16
MLA_NOPE_DIM = 128
MLA_ROPE_DIM = 64
MLA_V_DIM = 128
ROPE_THETA = 10000.0
TOP_K = 6
N_GROUPS = 8
TOPK_GROUPS = 4
ROUTED_SCALE = 2.5
NORM_EPS = 1e-6
N_MOD = 6
LOG2_E = math.log2(math.e)

LANES = 128
SUBLANES = 8
MLA_QK_PAD = 256
EXPERT_ROWS = 256
ATTN_LOOKAHEAD = 2
VMEM_LIMIT = 56 * 1024 * 1024


def _cparams(sem):
    return pltpu.CompilerParams(dimension_semantics=sem, vmem_limit_bytes=VMEM_LIMIT)


def _tile(n, pref):
    t = min(n, pref)
    assert n % t == 0, (n, pref)
    return t


def _silu(a):
    return a * jax.nn.sigmoid(a)


def _ada_kernel(c_ref, w_ref, b_ref, o_ref):
    c = c_ref[...]
    a = _silu(c).astype(BF16)
    o_ref[...] = jnp.dot(a, w_ref[...].astype(BF16), preferred_element_type=F32) + b_ref[...]


def _ada(c, w, b):
    bsz, d = c.shape
    n = w.shape[1]
    tn = _tile(n, 512)
    return pl.pallas_call(
        _ada_kernel,
        grid=(n // tn,),
        in_specs=[pl.BlockSpec((bsz, d), lambda j: (0, 0)),
                  pl.BlockSpec((d, tn), lambda j: (0, j)),
                  pl.BlockSpec((1, tn), lambda j: (0, j))],
        out_specs=pl.BlockSpec((bsz, tn), lambda j: (0, j)),
        out_shape=jax.ShapeDtypeStruct((bsz, n), F32),
        compiler_params=_cparams(("arbitrary",)),
        name="ada_mod",
    )(c, w, b.reshape(1, n))


def _norm_mod_kernel(x_ref, g_ref, sc_ref, sh_ref, o_ref):
    x = x_ref[0]
    ms = jnp.mean(x * x, axis=-1, keepdims=True)
    y = x * lax.rsqrt(ms + NORM_EPS) * g_ref[...]
    o_ref[0] = (y * (1.0 + sc_ref[0, 0]) + sh_ref[0, 0]).astype(o_ref.dtype)


def _norm_mod(x, gain, mod, sc_idx, sh_idx):
    bsz, s, d = x.shape
    tm = _tile(s, 512)
    return pl.pallas_call(
        _norm_mod_kernel,
        grid=(bsz, s // tm),
        in_specs=[pl.BlockSpec((1, tm, d), lambda b, i: (b, i, 0)),
                  pl.BlockSpec((1, d), lambda b, i: (0, 0)),
                  pl.BlockSpec((1, 1, 1, d), lambda b, i: (sc_idx, b, 0, 0)),
                  pl.BlockSpec((1, 1, 1, d), lambda b, i: (sh_idx, b, 0, 0))],
        out_specs=pl.BlockSpec((1, tm, d), lambda b, i: (b, i, 0)),
        out_shape=jax.ShapeDtypeStruct((bsz, s, d), BF16),
        compiler_params=_cparams(("arbitrary", "arbitrary")),
        name="norm_mod",
    )(x, gain.reshape(1, d), mod, mod)


def _mm_kernel(a_ref, b_ref, o_ref, *, sigmoid, scaled_tiles, col_scale):
    acc = jnp.dot(a_ref[...], b_ref[...], preferred_element_type=F32)
    if sigmoid:
        acc = jax.nn.sigmoid(acc)
    if scaled_tiles:
        acc = acc * jnp.where(pl.program_id(1) < scaled_tiles, col_scale, 1.0)
    o_ref[...] = acc.astype(o_ref.dtype)


def _matmul(a, b, out_dtype, *, name, sigmoid=False, scaled_cols=0, col_scale=1.0, tm_pref=1024, tn_pref=512):
    m, k = a.shape
    n = b.shape[1]
    tm, tn = _tile(m, tm_pref), _tile(n, tn_pref)
    assert scaled_cols % tn == 0
    return pl.pallas_call(
        functools.partial(_mm_kernel, sigmoid=sigmoid, scaled_tiles=scaled_cols // tn, col_scale=col_scale),
        grid=(m // tm, n // tn),
        in_specs=[pl.BlockSpec((tm, k), lambda i, j: (i, 0)),
                  pl.BlockSpec((k, tn), lambda i, j: (0, j))],
        out_specs=pl.BlockSpec((tm, tn), lambda i, j: (i, j)),
        out_shape=jax.ShapeDtypeStruct((m, n), out_dtype),
        compiler_params=_cparams(("arbitrary", "arbitrary")),
        name=name,
    )(a, b)


def _latent_kernel(a_ref, b_ref, cq_ref, ckv_ref, kpe_ref, *, q_rank, kv_rank):
    acc = jnp.dot(a_ref[...], b_ref[...], preferred_element_type=F32)
    cq_ref[...] = acc[:, :q_rank].astype(cq_ref.dtype)
    ckv_ref[...] = acc[:, q_rank:q_rank + kv_rank].astype(ckv_ref.dtype)
    kpe_ref[...] = acc[:, q_rank + kv_rank:]


def _latent_proj(h, w_lat, q_rank, kv_rank):
    m, k = h.shape
    n = w_lat.shape[1]
    tm = _tile(m, 512)
    return pl.pallas_call(
        functools.partial(_latent_kernel, q_rank=q_rank, kv_rank=kv_rank),
        grid=(m // tm,),
        in_specs=[pl.BlockSpec((tm, k), lambda i: (i, 0)),
                  pl.BlockSpec((k, n), lambda i: (0, 0))],
        out_specs=[pl.BlockSpec((tm, q_rank), lambda i: (i, 0)),
                   pl.BlockSpec((tm, kv_rank), lambda i: (i, 0)),
                   pl.BlockSpec((tm, LANES), lambda i: (i, 0))],
        out_shape=[jax.ShapeDtypeStruct((m, q_rank), BF16),
                   jax.ShapeDtypeStruct((m, kv_rank), BF16),
                   jax.ShapeDtypeStruct((m, LANES), F32)],
        compiler_params=_cparams(("arbitrary",)),
        name="latent_proj",
    )(h, w_lat)


def _rope_table_kernel(pos_ref, inv_ref, c_ref, s1_ref, s2_ref):
    half = MLA_ROPE_DIM // 2
    ang = pos_ref[0].astype(F32) * inv_ref[...]
    cos, sin = jnp.cos(ang), jnp.sin(ang)
    lane = lax.broadcasted_iota(I32, ang.shape, 1)
    c_ref[0] = jnp.where(lane < 2 * half, cos, 0.0)
    s1_ref[0] = jnp.where(lane < half, -sin, 0.0)
    s2_ref[0] = jnp.where((lane >= half) & (lane < 2 * half), sin, 0.0)


def _rope_tables(positions):
    bsz, s = positions.shape
    half = MLA_ROPE_DIM // 2
    inv = ROPE_THETA ** (-(jnp.arange(LANES, dtype=F32) % half) / half)
    ts = _tile(s, 512)
    spec = pl.BlockSpec((1, ts, LANES), lambda b, i: (b, i, 0))
    shp = jax.ShapeDtypeStruct((bsz, s, LANES), F32)
    return pl.pallas_call(
        _rope_table_kernel,
        grid=(bsz, s // ts),
        in_specs=[pl.BlockSpec((1, ts, 1), lambda b, i: (b, i, 0)),
                  pl.BlockSpec((1, LANES), lambda b, i: (0, 0))],
        out_specs=[spec, spec, spec],
        out_shape=[shp, shp, shp],
        compiler_params=_cparams(("arbitrary", "arbitrary")),
        name="rope_tables",
    )(positions.reshape(bsz, s, 1), inv.reshape(1, LANES))


def _rotate(r, c, s1, s2):
    half = MLA_ROPE_DIM // 2
    return r * c + pltpu.roll(r, LANES - half, 1) * s1 + pltpu.roll(r, half, 1) * s2


def _rms(x, gain):
    ms = jnp.mean(x * x, axis=-1, keepdims=True)
    return x * lax.rsqrt(ms + NORM_EPS) * gain


def _mla_q_kernel(cq_ref, g_ref, w_ref, c_ref, s1_ref, s2_ref, o_ref, *, heads, scale):
    y = _rms(cq_ref[0].astype(F32), g_ref[...]).astype(BF16)
    q = jnp.dot(y, w_ref[...], preferred_element_type=F32)
    c, s1, s2 = c_ref[0], s1_ref[0], s2_ref[0]
    for h in range(heads):
        base = h * MLA_QK_PAD
        o_ref[0, :, base:base + LANES] = (q[:, base:base + LANES] * scale).astype(BF16)
        r = q[:, base + LANES:base + 2 * LANES]
        o_ref[0, :, base + LANES:base + 2 * LANES] = (_rotate(r, c, s1, s2) * scale).astype(BF16)


def _mla_q(cq, gain, w_q, tabs, scale):
    bsz, s, qr = cq.shape
    heads = MLA_HEADS
    n = heads * MLA_QK_PAD
    tm = _tile(s, 512)
    tab_spec = pl.BlockSpec((1, tm, LANES), lambda b, i: (b, i, 0))
    return pl.pallas_call(
        functools.partial(_mla_q_kernel, heads=heads, scale=scale),
        grid=(bsz, s // tm),
        in_specs=[pl.BlockSpec((1, tm, qr), lambda b, i: (b, i, 0)),
                  pl.BlockSpec((1, qr), lambda b, i: (0, 0)),
                  pl.BlockSpec((qr, n), lambda b, i: (0, 0)),
                  tab_spec, tab_spec, tab_spec],
        out_specs=pl.BlockSpec((1, tm, n), lambda b, i: (b, i, 0)),
        out_shape=jax.ShapeDtypeStruct((bsz, s, n), BF16),
        compiler_params=_cparams(("arbitrary", "arbitrary")),
        name="mla_q_prep",
    )(cq, gain.reshape(1, qr), w_q, *tabs)


def _mla_kv_kernel(ckv_ref, g_ref, wk_ref, wv_ref, kpe_ref, c_ref, s1_ref, s2_ref, k_ref, v_ref, *, heads):
    y = _rms(ckv_ref[0].astype(F32), g_ref[...]).astype(BF16)
    kn = jnp.dot(y, wk_ref[...], preferred_element_type=F32)
    v_ref[0] = jnp.dot(y, wv_ref[...], preferred_element_type=F32).astype(BF16)
    kr = _rotate(kpe_ref[0], c_ref[0], s1_ref[0], s2_ref[0]).astype(BF16)
    for h in range(heads):
        base = h * MLA_QK_PAD
        k_ref[0, :, base:base + LANES] = kn[:, h * LANES:(h + 1) * LANES].astype(BF16)
        k_ref[0, :, base + LANES:base + 2 * LANES] = kr


def _mla_kv(ckv, gain, w_k, w_v, kpe, tabs):
    bsz, s, kvr = ckv.shape
    heads = MLA_HEADS
    tm = _tile(s, 512)
    tab_spec = pl.BlockSpec((1, tm, LANES), lambda b, i: (b, i, 0))
    return pl.pallas_call(
        functools.partial(_mla_kv_kernel, heads=heads),
        grid=(bsz, s // tm),
        in_specs=[pl.BlockSpec((1, tm, kvr), lambda b, i: (b, i, 0)),
                  pl.BlockSpec((1, kvr), lambda b, i: (0, 0)),
                  pl.BlockSpec((kvr, heads * MLA_NOPE_DIM), lambda b, i: (0, 0)),
                  pl.BlockSpec((kvr, heads * MLA_V_DIM), lambda b, i: (0, 0)),
                  tab_spec, tab_spec, tab_spec, tab_spec],
        out_specs=[pl.BlockSpec((1, tm, heads * MLA_QK_PAD), lambda b, i: (b, i, 0)),
                   pl.BlockSpec((1, tm, heads * MLA_V_DIM), lambda b, i: (b, i, 0))],
        out_shape=[jax.ShapeDtypeStruct((bsz, s, heads * MLA_QK_PAD), BF16),
                   jax.ShapeDtypeStruct((bsz, s, heads * MLA_V_DIM), BF16)],
        compiler_params=_cparams(("arbitrary", "arbitrary")),
        name="mla_kv_prep",
    )(ckv, gain.reshape(1, kvr), w_k, w_v, kpe, *tabs)


def _qk(q, k):
    return lax.dot_general(q, k, (((1,), (1,)), ((), ())), preferred_element_type=F32)


def _causal_mask(s, qi, ki, tq, tk):
    row = qi * tq + lax.broadcasted_iota(I32, s.shape, 0)
    col = ki * tk + lax.broadcasted_iota(I32, s.shape, 1)
    return jnp.where(row >= col, s, -jnp.inf)


def _lane_chunks(x):
    return [x[:, c * LANES:(c + 1) * LANES] for c in range(x.shape[1] // LANES)]


def _softmax_update(s, v, m_prev, l_prev, acc_prev):
    chunks = _lane_chunks(s)
    cmax = functools.reduce(jnp.maximum, chunks)
    m_new = jnp.maximum(m_prev, jnp.max(cmax, axis=-1, keepdims=True))
    alpha = jnp.exp2(m_prev - m_new)
    ps = [jnp.exp2(c - m_new) for c in chunks]
    psum = functools.reduce(lambda a, b: a + b, ps)
    l_new = alpha * l_prev + jnp.sum(psum, axis=-1, keepdims=True)
    p = jnp.concatenate([c.astype(BF16) for c in ps], axis=1)
    pv = jnp.dot(p, v, preferred_element_type=F32)
    acc_new = jnp.concatenate([alpha * a for a in _lane_chunks(acc_prev)], axis=1) + pv
    return m_new, l_new, acc_new


def _emit_pipelined(score_fns, update_fns):
    n = len(score_fns)
    pending = [score_fns[i]() for i in range(min(ATTN_LOOKAHEAD, n))]
    for i in range(n):
        if i + ATTN_LOOKAHEAD < n:
            pending.append(score_fns[i + ATTN_LOOKAHEAD]())
        update_fns[i](pending[i])
        pending[i] = None


def _diag_mask(s, r):
    rows = s.shape[0]
    off = s.shape[1] - rows
    row = lax.broadcasted_iota(I32, s.shape, 0)
    col = lax.broadcasted_iota(I32, s.shape, 1)
    return jnp.where(col - off <= row, s, -jnp.inf)


def _mla_attn_kernel(q_ref, k_ref, v_ref, o_ref, *scr, tq, tk, rb):
    qi = pl.program_id(2)
    nr = tq // rb
    state = [scr[3 * r:3 * r + 3] for r in range(nr)]
    for m_scr, l_scr, acc_scr in state:
        m_scr[...] = jnp.full(m_scr.shape, -jnp.inf, F32)
        l_scr[...] = jnp.zeros(l_scr.shape, F32)
        acc_scr[...] = jnp.zeros(acc_scr.shape, F32)

    def rows(r):
        return pl.ds(r * rb, rb)

    def update(r, s, v):
        m_scr, l_scr, acc_scr = state[r]
        m, l, acc = _softmax_update(s, v, m_scr[...], l_scr[...], acc_scr[...])
        m_scr[...] = m
        l_scr[...] = l
        acc_scr[...] = acc

    def full_tile(ki, carry):
        start = pl.multiple_of(ki * tk, tk)
        k = k_ref[0, pl.ds(start, tk), :]
        v = v_ref[0, pl.ds(start, tk), :]
        _emit_pipelined([functools.partial(_qk, q_ref[0, rows(r), :], k) for r in range(nr)],
                        [functools.partial(update, r, v=v) for r in range(nr)])
        return carry
    lax.fori_loop(0, qi * (tq // tk), full_tile, 0)

    dstart = pl.multiple_of(qi * tq, tq)
    cols = [pl.ds(dstart, (r + 1) * rb) for r in range(nr)]
    _emit_pipelined(
        [lambda r=r: _diag_mask(_qk(q_ref[0, rows(r), :], k_ref[0, cols[r], :]), r) for r in range(nr)],
        [lambda s, r=r: update(r, s, v_ref[0, cols[r], :]) for r in range(nr)])

    for r, (_, l_scr, acc_scr) in enumerate(state):
        o_ref[0, rows(r), :] = (acc_scr[...] / l_scr[...]).astype(o_ref.dtype)


def _attn_state_scratch(nr, rb, dv):
    return [pltpu.VMEM((rb, w), F32) for _ in range(nr) for w in (LANES, LANES, dv)]


def _mla_attention(q, k, v):
    bsz, s, _ = q.shape
    heads = MLA_HEADS
    t = _tile(s, 1024)
    rb = _tile(t, 256)
    return pl.pallas_call(
        functools.partial(_mla_attn_kernel, tq=t, tk=t, rb=rb),
        grid=(bsz, heads, s // t),
        in_specs=[pl.BlockSpec((1, t, MLA_QK_PAD), lambda b, h, i: (b, i, h)),
                  pl.BlockSpec((1, s, MLA_QK_PAD), lambda b, h, i: (b, 0, h)),
                  pl.BlockSpec((1, s, MLA_V_DIM), lambda b, h, i: (b, 0, h))],
        out_specs=pl.BlockSpec((1, t, MLA_V_DIM), lambda b, h, i: (b, i, h)),
        out_shape=jax.ShapeDtypeStruct((bsz, s, heads * MLA_V_DIM), BF16),
        scratch_shapes=_attn_state_scratch(t // rb, rb, MLA_V_DIM),
        compiler_params=_cparams(("arbitrary", "arbitrary", "arbitrary")),
        name="mla_attn",
    )(q, k, v)


def _diff_attn_kernel(q_ref, k_ref, v_ref, qpos_ref, kpos_ref, slope_ref, lam_ref, subln_ref, o_ref,
                      *scr, tq, tk, rb, lam_init):
    qi = pl.program_id(2)
    d = DIFF_HEAD_DIM
    nr = tq // rb
    state = [[scr[6 * r + 3 * g:6 * r + 3 * g + 3] for g in range(2)] for r in range(nr)]
    for r in range(nr):
        for m_scr, l_scr, a_scr in state[r]:
            m_scr[...] = jnp.full(m_scr.shape, -jnp.inf, F32)
            l_scr[...] = jnp.zeros(l_scr.shape, F32)
            a_scr[...] = jnp.zeros(a_scr.shape, F32)
    slope = slope_ref[0, :, 0:1]

    def rows(r):
        return pl.ds(r * rb, rb)

    def kpos(first, count):
        return jnp.concatenate([kpos_ref[0, first + j] for j in range(count)], axis=1)

    chains = [(r, g) for r in range(nr) for g in range(2)]

    def emit(k_of, v_of, kp_of, masked):
        bias = {}

        def score(r, g):
            if g == 0:
                bias[r] = slope * jnp.abs(qpos_ref[0, rows(r), :] - kp_of(r))
            s = _qk(q_ref[0, rows(r), g * d:(g + 1) * d], k_of(r)[:, g * d:(g + 1) * d]) - bias[r]
            return _diag_mask(s, r) if masked else s

        def update(r, g, s):
            m_scr, l_scr, a_scr = state[r][g]
            m, l, acc = _softmax_update(s, v_of(r), m_scr[...], l_scr[...], a_scr[...])
            m_scr[...] = m
            l_scr[...] = l
            a_scr[...] = acc

        _emit_pipelined([functools.partial(score, r, g) for r, g in chains],
                        [functools.partial(update, r, g) for r, g in chains])

    def full_tile(ki, carry):
        start = pl.multiple_of(ki * tk, tk)
        k = k_ref[0, pl.ds(start, tk), :]
        v = v_ref[0, pl.ds(start, tk), :]
        kp = kpos(ki * (tk // rb), tk // rb)
        emit(lambda r: k, lambda r: v, lambda r: kp, False)
        return carry
    lax.fori_loop(0, qi * (tq // tk), full_tile, 0)

    dstart = pl.multiple_of(qi * tq, tq)
    cols = [pl.ds(dstart, (r + 1) * rb) for r in range(nr)]
    emit(lambda r: k_ref[0, cols[r], :], lambda r: v_ref[0, cols[r], :], lambda r: kpos(qi * nr, r + 1), True)

    lp = lam_ref[...]
    e1 = jnp.exp(jnp.sum(lp[0:1] * lp[1:2], axis=-1, keepdims=True))
    e2 = jnp.exp(jnp.sum(lp[2:3] * lp[3:4], axis=-1, keepdims=True))
    lam = e1 - e2 + lam_init
    for r in range(nr):
        (_, l1, a1), (_, l2, a2) = state[r]
        o1 = [a / l1[...] for a in _lane_chunks(a1[...])]
        o2 = [a / l2[...] for a in _lane_chunks(a2[...])]
        o = jnp.concatenate([x - lam * y for x, y in zip(o1, o2)], axis=1)
        o_ref[0, rows(r), :] = (_rms(o, subln_ref[...]) * (1.0 - lam_init)).astype(o_ref.dtype)


def _diff_attention(qkv, positions, diff_lambda, subln, lam_init):
    bsz, s, _ = qkv.shape
    heads = DIFF_HEADS
    dv = 2 * DIFF_HEAD_DIM
    t = _tile(s, 1024)
    tk = _tile(t, 512)
    rb = _tile(tk, 256)
    nq = s // t
    slopes = 2.0 ** (-8.0 * jnp.arange(1, heads + 1, dtype=F32) / heads) * LOG2_E
    slopes = jnp.broadcast_to(slopes[:, None, None], (heads, 1, LANES))
    posf = positions.astype(F32)
    return pl.pallas_call(
        functools.partial(_diff_attn_kernel, tq=t, tk=tk, rb=rb, lam_init=lam_init),
        grid=(bsz, heads, nq),
        in_specs=[pl.BlockSpec((1, t, dv), lambda b, h, i: (b, i, h)),
                  pl.BlockSpec((1, s, dv), lambda b, h, i: (b, 0, heads + h)),
                  pl.BlockSpec((1, s, dv), lambda b, h, i: (b, 0, 2 * heads + h)),
                  pl.BlockSpec((1, t, 1), lambda b, h, i: (b, i, 0)),
                  pl.BlockSpec((1, s // rb, 1, rb), lambda b, h, i: (b, 0, 0, 0)),
                  pl.BlockSpec((1, 1, LANES), lambda b, h, i: (h, 0, 0)),
                  pl.BlockSpec((4, DIFF_HEAD_DIM), lambda b, h, i: (0, 0)),
                  pl.BlockSpec((1, dv), lambda b, h, i: (0, 0))],
        out_specs=pl.BlockSpec((1, t, dv), lambda b, h, i: (b, i, h)),
        out_shape=jax.ShapeDtypeStruct((bsz, s, heads * dv), BF16),
        scratch_shapes=_attn_state_scratch(2 * (t // rb), rb, dv),
        compiler_params=_cparams(("arbitrary", "arbitrary", "arbitrary")),
        name="diff_attn",
    )(qkv, qkv, qkv, posf.reshape(bsz, s, 1), posf.reshape(bsz, s // rb, 1, rb), slopes, diff_lambda,
      subln.reshape(1, dv))


def _merge_kernel(od_ref, om_ref, wd_ref, wm_ref, g0_ref, g1_ref, x_ref, gt_ref, o_ref):
    yd = jnp.dot(od_ref[0], wd_ref[...], preferred_element_type=F32)
    ym = jnp.dot(om_ref[0], wm_ref[...], preferred_element_type=F32)
    y = g0_ref[0].astype(F32) * yd + g1_ref[0].astype(F32) * ym
    o_ref[0] = x_ref[0] + gt_ref[0, 0] * y


def _merge(o_d, o_m, w_d, w_m, gates, x, mod, gt_idx):
    bsz, s, d = x.shape
    kd, km = o_d.shape[-1], o_m.shape[-1]
    tm, tn = _tile(s, 512), _tile(d, 512)
    nj = d // tn
    return pl.pallas_call(
        _merge_kernel,
        grid=(bsz, s // tm, nj),
        in_specs=[pl.BlockSpec((1, tm, kd), lambda b, i, j: (b, i, 0)),
                  pl.BlockSpec((1, tm, km), lambda b, i, j: (b, i, 0)),
                  pl.BlockSpec((kd, tn), lambda b, i, j: (0, j)),
                  pl.BlockSpec((km, tn), lambda b, i, j: (0, j)),
                  pl.BlockSpec((1, tm, tn), lambda b, i, j: (b, i, j)),
                  pl.BlockSpec((1, tm, tn), lambda b, i, j: (b, i, j + nj)),
                  pl.BlockSpec((1, tm, tn), lambda b, i, j: (b, i, j)),
                  pl.BlockSpec((1, 1, 1, tn), lambda b, i, j: (gt_idx, b, 0, j))],
        out_specs=pl.BlockSpec((1, tm, tn), lambda b, i, j: (b, i, j)),
        out_shape=jax.ShapeDtypeStruct((bsz, s, d), F32),
        compiler_params=_cparams(("arbitrary", "arbitrary", "arbitrary")),
        name="out_merge",
    )(o_d, o_m, w_d, w_m, gates, gates, x, mod)


def _pack_rows(y):
    half = y.shape[1] // 2
    bits = lax.bitcast_convert_type(y.astype(BF16).astype(F32), U32)
    return (bits[:, half:] & jnp.uint32(0xFFFF0000)) | (bits[:, :half] >> 16)


def _unpack_rows(w):
    lo = lax.bitcast_convert_type(w << 16, F32)
    hi = lax.bitcast_convert_type(w & jnp.uint32(0xFFFF0000), F32)
    return lo, hi


def _ffn_norm_kernel(x_ref, g_ref, sc_ref, sh_ref, rw_ref, h_ref, hp_ref, lt_ref, *, nch):
    x = x_ref[0]
    y = _rms(x, g_ref[...]) * (1.0 + sc_ref[0, 0]) + sh_ref[0, 0]
    hb = y.astype(BF16)
    h_ref[0] = hb
    lt_ref[...] = lax.dot_general(rw_ref[...], hb, (((1,), (1,)), ((), ())), preferred_element_type=F32)
    packed = _pack_rows(y)
    for c in range(nch):
        hp_ref[pl.ds(c, x.shape[0], stride=nch), :] = packed[:, c * LANES:(c + 1) * LANES]


def _ffn_norm(x, gain, mod, sc_idx, sh_idx, router_wt):
    bsz, s, d = x.shape
    e = router_wt.shape[0]
    nch = d // (2 * LANES)
    tm = _tile(s, 256)
    ns = s // tm
    return pl.pallas_call(
        functools.partial(_ffn_norm_kernel, nch=nch),
        grid=(bsz, ns),
        in_specs=[pl.BlockSpec((1, tm, d), lambda b, i: (b, i, 0)),
                  pl.BlockSpec((1, d), lambda b, i: (0, 0)),
                  pl.BlockSpec((1, 1, 1, d), lambda b, i: (sc_idx, b, 0, 0)),
                  pl.BlockSpec((1, 1, 1, d), lambda b, i: (sh_idx, b, 0, 0)),
                  pl.BlockSpec((e, d), lambda b, i: (0, 0))],
        out_specs=[pl.BlockSpec((1, tm, d), lambda b, i: (b, i, 0)),
                   pl.BlockSpec((tm * nch, LANES), lambda b, i: (b * ns + i, 0)),
                   pl.BlockSpec((e, tm), lambda b, i: (0, b * ns + i))],
        out_shape=[jax.ShapeDtypeStruct((bsz, s, d), BF16),
                   jax.ShapeDtypeStruct((bsz * s * nch, LANES), U32),
                   jax.ShapeDtypeStruct((e, bsz * s), F32)],
        compiler_params=_cparams(("arbitrary", "arbitrary")),
        name="ffn_norm_router",
    )(x, gain.reshape(1, d), mod, mod, router_wt)


def _first_index(hit, iota, axis, size):
    return jnp.min(jnp.where(hit, iota, size), axis=axis, keepdims=True)


def _route_kernel(lt_ref, bias_ref, idx_ref, wt_ref, rank_ref, sizes_ref, cnt_scr, *, n_exp, tn):
    i = pl.program_id(0)
    gsz = n_exp // N_GROUPS

    @pl.when(i == 0)
    def _():
        cnt_scr[...] = jnp.zeros(cnt_scr.shape, F32)

    scores = jax.nn.sigmoid(lt_ref[...])
    sel = scores + bias_ref[...]
    sel3 = sel.reshape(N_GROUPS, gsz, tn)
    j3 = lax.broadcasted_iota(I32, sel3.shape, 1)
    top1 = jnp.max(sel3, axis=1, keepdims=True)
    first = _first_index(sel3 == top1, j3, 1, gsz)
    top2 = jnp.max(jnp.where(j3 == first, -jnp.inf, sel3), axis=1, keepdims=True)
    gscore = (top1 + top2).reshape(N_GROUPS, tn)

    giota = lax.broadcasted_iota(I32, gscore.shape, 0)
    gmask = jnp.zeros(gscore.shape, jnp.bool_)
    for _ in range(TOPK_GROUPS):
        best = jnp.max(gscore, axis=0, keepdims=True)
        gi = _first_index(gscore == best, giota, 0, N_GROUPS)
        hit = giota == gi
        gmask = gmask | hit
        gscore = jnp.where(hit, -jnp.inf, gscore)

    emask = jnp.broadcast_to(gmask.reshape(N_GROUPS, 1, tn), sel3.shape)
    cand = jnp.where(emask, sel3, -jnp.inf).reshape(n_exp, tn)
    eiota = lax.broadcasted_iota(I32, cand.shape, 0)
    hits, idxs, vals = [], [], []
    for _ in range(TOP_K):
        best = jnp.max(cand, axis=0, keepdims=True)
        ei = _first_index(cand == best, eiota, 0, n_exp)
        hit = eiota == ei
        hits.append(hit)
        idxs.append(ei)
        vals.append(jnp.sum(jnp.where(hit, scores, 0.0), axis=0, keepdims=True))
        cand = jnp.where(hit, -jnp.inf, cand)
    total = functools.reduce(lambda a, b: a + b, vals)

    chosen = functools.reduce(lambda a, b: a | b, hits)
    onehot = jnp.where(chosen, 1.0, 0.0)
    r = lax.broadcasted_iota(I32, (tn, tn), 0)
    c = lax.broadcasted_iota(I32, (tn, tn), 1)
    upper = jnp.where(r < c, 1.0, 0.0).astype(BF16)
    before = cnt_scr[...] + jnp.dot(onehot.astype(BF16), upper, preferred_element_type=F32)
    cnt_new = cnt_scr[...] + jnp.sum(onehot, axis=1, keepdims=True)
    cnt_scr[...] = cnt_new

    pad = SUBLANES - TOP_K
    ranks = [jnp.sum(jnp.where(h, before, 0.0), axis=0, keepdims=True).astype(I32) for h in hits]
    zi = [jnp.zeros((pad, tn), I32)]
    idx_ref[...] = jnp.concatenate(idxs + zi, axis=0)
    rank_ref[...] = jnp.concatenate(ranks + zi, axis=0)
    wt_ref[...] = jnp.concatenate([v / total * ROUTED_SCALE for v in vals] + [jnp.zeros((pad, tn), F32)], axis=0)
    sizes_ref[...] = jnp.broadcast_to(cnt_new, sizes_ref.shape).astype(I32)


def _route(logits_t, bias):
    n_exp, n = logits_t.shape
    tn = _tile(n, 512)
    row_spec = pl.BlockSpec((SUBLANES, tn), lambda i: (0, i))
    return pl.pallas_call(
        functools.partial(_route_kernel, n_exp=n_exp, tn=tn),
        grid=(n // tn,),
        in_specs=[pl.BlockSpec((n_exp, tn), lambda i: (0, i)),
                  pl.BlockSpec((n_exp, 1), lambda i: (0, 0))],
        out_specs=[row_spec, row_spec, row_spec, pl.BlockSpec((n_exp, LANES), lambda i: (0, 0))],
        out_shape=[jax.ShapeDtypeStruct((SUBLANES, n), I32),
                   jax.ShapeDtypeStruct((SUBLANES, n), F32),
                   jax.ShapeDtypeStruct((SUBLANES, n), I32),
                   jax.ShapeDtypeStruct((n_exp, LANES), I32)],
        scratch_shapes=[pltpu.VMEM((n_exp, 1), F32)],
        compiler_params=_cparams(("arbitrary",)),
        name="route_topk",
    )(logits_t, bias.reshape(n_exp, 1))


def _dispatch_kernel(dest_ref, seg_ref, hp_ref, xs_ref, zero_scr, sem, zsem, *, n_tok, tc, nch, n_exp, blk, nb):
    i = pl.program_id(0)

    def row_copy(j, slot):
        return pltpu.make_async_copy(hp_ref.at[pl.ds(pl.multiple_of(j * nch, nch), nch)],
                                     xs_ref.at[pl.ds(pl.multiple_of(slot * nch, nch), nch)], sem)

    def zero_row(slot):
        return pltpu.make_async_copy(zero_scr.at[pl.ds(0, nch)],
                                     xs_ref.at[pl.ds(pl.multiple_of(slot * nch, nch), nch)], zsem)

    def zero_block(b):
        return pltpu.make_async_copy(zero_scr,
                                     xs_ref.at[pl.ds(pl.multiple_of(b * (blk * nch), blk * nch), blk * nch)], zsem)

    def start_then_wait(lo, hi, copy):
        def start(v, c):
            copy(v).start()
            return c
        lax.fori_loop(lo, hi, start, 0)

        def wait(v, c):
            copy(v).wait()
            return c
        lax.fori_loop(lo, hi, wait, 0)

    @pl.when(i == 0)
    def _():
        zero_scr[...] = jnp.zeros(zero_scr.shape, U32)

        def per_expert(e, carry):
            start_then_wait(seg_ref[e], seg_ref[n_exp + e], zero_row)
            return carry
        lax.fori_loop(0, n_exp, per_expert, 0)
        start_then_wait(seg_ref[2 * n_exp], nb, zero_block)

    base = i * tc

    def start(j, c):
        for k in range(TOP_K):
            row_copy(j, dest_ref[k * n_tok + base + j]).start()
        return c
    lax.fori_loop(0, tc, start, 0)

    def wait(j, c):
        for k in range(TOP_K):
            row_copy(0, 0).wait()
        return c
    lax.fori_loop(0, tc, wait, 0)


def _dispatch(dest, seg, hp, nb, blk, nch, n_exp):
    n_tok = dest.shape[0] // TOP_K
    tc = _tile(n_tok, 256)
    return pl.pallas_call(
        functools.partial(_dispatch_kernel, n_tok=n_tok, tc=tc, nch=nch, n_exp=n_exp, blk=blk, nb=nb),
        grid_spec=pltpu.PrefetchScalarGridSpec(
            num_scalar_prefetch=2,
            grid=(n_tok // tc,),
            in_specs=[pl.BlockSpec((tc * nch, LANES), lambda i, dst, sg: (i, 0))],
            out_specs=pl.BlockSpec(memory_space=pl.ANY),
            scratch_shapes=[pltpu.VMEM((blk * nch, LANES), U32), pltpu.SemaphoreType.DMA,
                            pltpu.SemaphoreType.DMA]),
        out_shape=jax.ShapeDtypeStruct((nb * blk * nch, LANES), U32),
        compiler_params=_cparams(("arbitrary",)),
        name="moe_dispatch",
    )(dest, seg, hp)


def _load_rows(ref, tm, nch):
    los, his = [], []
    for c in range(nch):
        lo, hi = _unpack_rows(ref[pl.ds(c, tm, stride=nch), :])
        los.append(lo.astype(BF16))
        his.append(hi.astype(BF16))
    return jnp.concatenate(los + his, axis=1)


def _expert_kernel(be_ref, nact_ref, xs_ref, w1_ref, w3_ref, w2_ref, ys_ref, *, tm, nch):
    active = pl.program_id(0) < nact_ref[0]

    @pl.when(jnp.logical_not(active))
    def _():
        ys_ref[...] = jnp.zeros(ys_ref.shape, U32)

    @pl.when(active)
    def _():
        x = _load_rows(xs_ref, tm, nch)
        a = jnp.dot(x, w1_ref[0], preferred_element_type=F32)
        b = jnp.dot(x, w3_ref[0], preferred_element_type=F32)
        y = jnp.dot((_silu(a) * b).astype(BF16), w2_ref[0], preferred_element_type=F32)
        packed = _pack_rows(y)
        for c in range(nch):
            ys_ref[pl.ds(c, tm, stride=nch), :] = packed[:, c * LANES:(c + 1) * LANES]


def _experts(block_expert, nact, xs, w1, w3, w2, nch):
    n_exp, d, f = w1.shape
    tm = EXPERT_ROWS
    nb = xs.shape[0] // (tm * nch)
    blk = lambda i, be, na: (jnp.minimum(i, na[0] - 1), 0)
    wsel = lambda i, be, na: (be[jnp.minimum(i, na[0] - 1)], 0, 0)
    return pl.pallas_call(
        functools.partial(_expert_kernel, tm=tm, nch=nch),
        grid_spec=pltpu.PrefetchScalarGridSpec(
            num_scalar_prefetch=2,
            grid=(nb,),
            in_specs=[pl.BlockSpec((tm * nch, LANES), blk),
                      pl.BlockSpec((1, d, f), wsel),
                      pl.BlockSpec((1, d, f), wsel),
                      pl.BlockSpec((1, f, d), wsel)],
            out_specs=pl.BlockSpec((tm * nch, LANES), lambda i, be, na: (i, 0))),
        out_shape=jax.ShapeDtypeStruct(xs.shape, U32),
        compiler_params=_cparams(("arbitrary",)),
        name="moe_experts",
    )(block_expert, nact, xs, w1, w3, w2)


def _combine_kernel(dest_ref, ys_ref, wt_ref, h_ref, w1_ref, w3_ref, w2_ref, x_ref, gt_ref, fn_ref, o_ref,
                    ybuf, sem, *, n_tok, tm, nch, ns, final):
    b, i = pl.program_id(0), pl.program_id(1)
    base = (b * ns + i) * tm

    def row_copy(slot, k, j):
        return pltpu.make_async_copy(ys_ref.at[pl.ds(pl.multiple_of(slot * nch, nch), nch)],
                                     ybuf.at[k, pl.ds(pl.multiple_of(j * nch, nch), nch)], sem)

    def start(j, c):
        for k in range(TOP_K):
            row_copy(dest_ref[k * n_tok + base + j], k, j).start()
        return c
    lax.fori_loop(0, tm, start, 0)

    h = h_ref[0]
    a = jnp.dot(h, w1_ref[...], preferred_element_type=F32)
    g = jnp.dot(h, w3_ref[...], preferred_element_type=F32)
    ffn = jnp.dot((_silu(a) * g).astype(BF16), w2_ref[...], preferred_element_type=F32)

    def wait(j, c):
        for k in range(TOP_K):
            row_copy(0, k, 0).wait()
        return c
    lax.fori_loop(0, tm, wait, 0)

    wts = wt_ref[...]
    los = [None] * nch
    his = [None] * nch
    for k in range(TOP_K):
        wk = wts[:, k:k + 1]
        for c in range(nch):
            lo, hi = _unpack_rows(ybuf[k, pl.ds(c, tm, stride=nch), :])
            los[c] = wk * lo if k == 0 else los[c] + wk * lo
            his[c] = wk * hi if k == 0 else his[c] + wk * hi
    routed = jnp.concatenate(los + his, axis=1)
    y = x_ref[0] + gt_ref[0, 0] * (routed + ffn)
    o_ref[0] = _rms(y, fn_ref[...]) if final else y


def _combine(dest, ys, wts, h, w1, w3, w2, x, mod, gt_idx, final_norm, nch, final):
    bsz, s, d = x.shape
    f = w1.shape[1]
    n_tok = bsz * s
    tm = _tile(s, 128)
    ns = s // tm
    return pl.pallas_call(
        functools.partial(_combine_kernel, n_tok=n_tok, tm=tm, nch=nch, ns=ns, final=final),
        grid_spec=pltpu.PrefetchScalarGridSpec(
            num_scalar_prefetch=1,
            grid=(bsz, ns),
            in_specs=[pl.BlockSpec(memory_space=pl.ANY),
                      pl.BlockSpec((tm, SUBLANES), lambda b, i, dst: (b * ns + i, 0)),
                      pl.BlockSpec((1, tm, d), lambda b, i, dst: (b, i, 0)),
                      pl.BlockSpec((d, f), lambda b, i, dst: (0, 0)),
                      pl.BlockSpec((d, f), lambda b, i, dst: (0, 0)),
                      pl.BlockSpec((f, d), lambda b, i, dst: (0, 0)),
                      pl.BlockSpec((1, tm, d), lambda b, i, dst: (b, i, 0)),
                      pl.BlockSpec((1, 1, 1, d), lambda b, i, dst: (gt_idx, b, 0, 0)),
                      pl.BlockSpec((1, d), lambda b, i, dst: (0, 0))],
            out_specs=pl.BlockSpec((1, tm, d), lambda b, i, dst: (b, i, 0)),
            scratch_shapes=[pltpu.VMEM((TOP_K, tm * nch, LANES), U32), pltpu.SemaphoreType.DMA]),
        out_shape=jax.ShapeDtypeStruct((bsz, s, d), F32),
        compiler_params=_cparams(("arbitrary", "arbitrary")),
        name="moe_combine",
    )(dest, ys, wts, h, w1, w3, w2, x, mod, final_norm.reshape(1, d))


def _mla_weights(w_uq, w_ukv):
    heads = MLA_HEADS
    qr, kvr = w_uq.shape[0], w_ukv.shape[0]
    wq = w_uq.reshape(qr, heads, MLA_NOPE_DIM + MLA_ROPE_DIM)
    wq = jnp.pad(wq, ((0, 0), (0, 0), (0, MLA_QK_PAD - MLA_NOPE_DIM - MLA_ROPE_DIM)))
    wkv = w_ukv.reshape(kvr, heads, MLA_NOPE_DIM + MLA_V_DIM)
    wk = wkv[:, :, :MLA_NOPE_DIM].reshape(kvr, heads * MLA_NOPE_DIM)
    wv = wkv[:, :, MLA_NOPE_DIM:].reshape(kvr, heads * MLA_V_DIM)
    return (wq.reshape(qr, heads * MLA_QK_PAD).astype(BF16), wk.astype(BF16), wv.astype(BF16))


def _layer(x, mod, positions, tabs, l, norm_attn, w_in, diff_lambda, diff_subln, mla_q_norm, mla_w_uq,
           mla_kv_norm, mla_w_ukv, w_out, norm_ffn, router_w, router_bias, exp_w1, exp_w3, exp_w2,
           shared_w1, shared_w3, shared_w2, final_norm, final):
    bsz, s, d = x.shape
    n_tok = bsz * s
    q_rank, kv_rank = mla_w_uq.shape[0], mla_w_ukv.shape[0]
    qk_cols = 2 * DIFF_HEADS * DIFF_HEAD_DIM
    v_cols = DIFF_HEADS * 2 * DIFF_HEAD_DIM
    qkv_cols = 2 * qk_cols + v_cols
    lat_cols = q_rank + kv_rank + MLA_ROPE_DIM
    lam_init = 0.8 - 0.6 * math.exp(-0.3 * l)

    h = _norm_mod(x, norm_attn, mod, 1, 0).reshape(n_tok, d)
    w_qkv = w_in[:, :qkv_cols].astype(BF16)
    w_lat = jnp.pad(w_in[:, qkv_cols:qkv_cols + lat_cols], ((0, 0), (0, LANES - MLA_ROPE_DIM))).astype(BF16)
    w_gate = w_in[:, qkv_cols + lat_cols:].astype(BF16)
    qkv = _matmul(h, w_qkv, BF16, name="qkv_proj", scaled_cols=qk_cols,
                  col_scale=DIFF_HEAD_DIM ** -0.5 * LOG2_E).reshape(bsz, s, qkv_cols)
    cq, ckv, kpe = _latent_proj(h, w_lat, q_rank, kv_rank)
    gates = _matmul(h, w_gate, BF16, name="gate_proj", sigmoid=True).reshape(bsz, s, 2 * d)

    o_d = _diff_attention(qkv, positions, diff_lambda, diff_subln, lam_init)

    wq, wk, wv = _mla_weights(mla_w_uq, mla_w_ukv)
    scale = (MLA_NOPE_DIM + MLA_ROPE_DIM) ** -0.5 * LOG2_E
    q_m = _mla_q(cq.reshape(bsz, s, q_rank), mla_q_norm, wq, tabs, scale)
    k_m, v_m = _mla_kv(ckv.reshape(bsz, s, kv_rank), mla_kv_norm, wk, wv, kpe.reshape(bsz, s, LANES), tabs)
    o_m = _mla_attention(q_m, k_m, v_m)

    w_o = w_out.astype(BF16)
    x = _merge(o_d, o_m, w_o[:v_cols], w_o[v_cols:], gates, x, mod, 2)

    n_exp = router_w.shape[1]
    nch = d // (2 * LANES)
    h2, hp, logits_t = _ffn_norm(x, norm_ffn, mod, 4, 3, router_w.T.astype(BF16))
    idx_t, wts_t, rank_t, sizes = _route(logits_t, router_bias)

    blk = EXPERT_ROWS
    sizes = sizes[:, 0]
    padded = (sizes + blk - 1) // blk * blk
    pad_end = jnp.cumsum(padded)
    pad_start = pad_end - padded
    onehot = idx_t[:TOP_K, :, None] == jnp.arange(n_exp, dtype=I32)
    dest = (jnp.sum(jnp.where(onehot, pad_start, 0), axis=-1) + rank_t[:TOP_K]).astype(I32).reshape(-1)
    n_blocks = -(-n_tok * TOP_K // blk) + n_exp
    block_start = jnp.arange(n_blocks, dtype=I32) * blk
    block_expert = jnp.minimum(jnp.sum(pad_end[None, :] <= block_start[:, None], axis=1), n_exp - 1).astype(I32)
    nact = (pad_end[-1:] // blk).astype(I32)
    seg = jnp.concatenate([pad_start + sizes, pad_end, nact]).astype(I32)

    xs = _dispatch(dest, seg, hp, n_blocks, blk, nch, n_exp)
    ys = _experts(block_expert, nact, xs, exp_w1.astype(BF16), exp_w3.astype(BF16), exp_w2.astype(BF16), nch)
    return _combine(dest, ys, wts_t.T, h2, shared_w1.astype(BF16), shared_w3.astype(BF16),
                    shared_w2.astype(BF16), x, mod, 5, final_norm, nch, final)


def kernel(x, c, positions, w_ada, b_ada, norm_attn, w_in, diff_lambda, diff_subln, mla_q_norm, mla_w_uq,
           mla_kv_norm, mla_w_ukv, w_out, norm_ffn, router_w, router_bias, exp_w1, exp_w3, exp_w2,
           shared_w1, shared_w3, shared_w2, final_norm):
    bsz, s, d = x.shape
    depth = w_ada.shape[0]
    tabs = _rope_tables(positions)
    for l in range(depth):
        mod = _ada(c, w_ada[l], b_ada[l])
        mod = mod.reshape(bsz, N_MOD, 1, d).transpose(1, 0, 2, 3)
        x = _layer(x, mod, positions, tabs, l, norm_attn[l], w_in[l], diff_lambda[l], diff_subln[l],
                   mla_q_norm[l], mla_w_uq[l], mla_kv_norm[l], mla_w_ukv[l], w_out[l], norm_ffn[l],
                   router_w[l], router_bias[l], exp_w1[l], exp_w3[l], exp_w2[l],
                   shared_w1[l], shared_w3[l], shared_w2[l], final_norm, l == depth - 1)
    return x
```

```python
import functools
import math

import jax
import jax.numpy as jnp
from jax import lax
from jax.experimental import pallas as pl
from jax.experimental.pallas import tpu as pltpu

F32 = jnp.float32
BF16 = jnp.bfloat16
U32 = jnp.uint32
I32 = jnp.int32

DIFF_HEADS = 8
DIFF_HEAD_DIM = 128
MLA_HEADS = 16
MLA_NOPE_DIM = 128
MLA_ROPE_DIM = 64
MLA_V_DIM = 128
ROPE_THETA = 10000.0
TOP_K = 6
N_GROUPS = 8
TOPK_GROUPS = 4
ROUTED_SCALE = 2.5
NORM_EPS = 1e-6
N_MOD = 6
LOG2_E = math.log2(math.e)

LANES = 128
SUBLANES = 8
MLA_QK_PAD = 256
EXPERT_ROWS = 256
DMA_ISSUE_UNROLL = 4
ATTN_LOOKAHEAD = 2
VMEM_LIMIT = 56 * 1024 * 1024


def _cparams(sem):
    return pltpu.CompilerParams(dimension_semantics=sem, vmem_limit_bytes=VMEM_LIMIT)


def _tile(n, pref):
    t = min(n, pref)
    assert n % t == 0, (n, pref)
    return t


def _silu(a):
    return a * jax.nn.sigmoid(a)


def _ada_kernel(c_ref, w_ref, b_ref, o_ref):
    c = c_ref[...]
    a = _silu(c).astype(BF16)
    o_ref[...] = jnp.dot(a, w_ref[...].astype(BF16), preferred_element_type=F32) + b_ref[...]


def _ada(c, w, b):
    bsz, d = c.shape
    n = w.shape[1]
    tn = _tile(n, 512)
    return pl.pallas_call(
        _ada_kernel,
        grid=(n // tn,),
        in_specs=[pl.BlockSpec((bsz, d), lambda j: (0, 0)),
                  pl.BlockSpec((d, tn), lambda j: (0, j)),
                  pl.BlockSpec((1, tn), lambda j: (0, j))],
        out_specs=pl.BlockSpec((bsz, tn), lambda j: (0, j)),
        out_shape=jax.ShapeDtypeStruct((bsz, n), F32),
        compiler_params=_cparams(("arbitrary",)),
        name="ada_mod",
    )(c, w, b.reshape(1, n))


def _norm_mod_kernel(x_ref, g_ref, sc_ref, sh_ref, o_ref):
    x = x_ref[0]
    ms = jnp.mean(x * x, axis=-1, keepdims=True)
    y = x * lax.rsqrt(ms + NORM_EPS) * g_ref[...]
    o_ref[0] = (y * (1.0 + sc_ref[0, 0]) + sh_ref[0, 0]).astype(o_ref.dtype)


def _norm_mod(x, gain, mod, sc_idx, sh_idx):
    bsz, s, d = x.shape
    tm = _tile(s, 512)
    return pl.pallas_call(
        _norm_mod_kernel,
        grid=(bsz, s // tm),
        in_specs=[pl.BlockSpec((1, tm, d), lambda b, i: (b, i, 0)),
                  pl.BlockSpec((1, d), lambda b, i: (0, 0)),
                  pl.BlockSpec((1, 1, 1, d), lambda b, i: (sc_idx, b, 0, 0)),
                  pl.BlockSpec((1, 1, 1, d), lambda b, i: (sh_idx, b, 0, 0))],
        out_specs=pl.BlockSpec((1, tm, d), lambda b, i: (b, i, 0)),
        out_shape=jax.ShapeDtypeStruct((bsz, s, d), BF16),
        compiler_params=_cparams(("arbitrary", "arbitrary")),
        name="norm_mod",
    )(x, gain.reshape(1, d), mod, mod)


def _mm_kernel(a_ref, b_ref, o_ref, *, sigmoid, scaled_tiles, col_scale):
    acc = jnp.dot(a_ref[...], b_ref[...], preferred_element_type=F32)
    if sigmoid:
        acc = jax.nn.sigmoid(acc)
    if scaled_tiles:
        acc = acc * jnp.where(pl.program_id(1) < scaled_tiles, col_scale, 1.0)
    o_ref[...] = acc.astype(o_ref.dtype)


def _matmul(a, b, out_dtype, *, name, sigmoid=False, scaled_cols=0, col_scale=1.0, tm_pref=1024, tn_pref=512):
    m, k = a.shape
    n = b.shape[1]
    tm, tn = _tile(m, tm_pref), _tile(n, tn_pref)
    assert scaled_cols % tn == 0
    return pl.pallas_call(
        functools.partial(_mm_kernel, sigmoid=sigmoid, scaled_tiles=scaled_cols // tn, col_scale=col_scale),
        grid=(m // tm, n // tn),
        in_specs=[pl.BlockSpec((tm, k), lambda i, j: (i, 0)),
                  pl.BlockSpec((k, tn), lambda i, j: (0, j))],
        out_specs=pl.BlockSpec((tm, tn), lambda i, j: (i, j)),
        out_shape=jax.ShapeDtypeStruct((m, n), out_dtype),
        compiler_params=_cparams(("arbitrary", "arbitrary")),
        name=name,
    )(a, b)


def _latent_kernel(a_ref, b_ref, cq_ref, ckv_ref, kpe_ref, *, q_rank, kv_rank):
    acc = jnp.dot(a_ref[...], b_ref[...], preferred_element_type=F32)
    cq_ref[...] = acc[:, :q_rank].astype(cq_ref.dtype)
    ckv_ref[...] = acc[:, q_rank:q_rank + kv_rank].astype(ckv_ref.dtype)
    kpe_ref[...] = acc[:, q_rank + kv_rank:]


def _latent_proj(h, w_lat, q_rank, kv_rank):
    m, k = h.shape
    n = w_lat.shape[1]
    tm = _tile(m, 512)
    return pl.pallas_call(
        functools.partial(_latent_kernel, q_rank=q_rank, kv_rank=kv_rank),
        grid=(m // tm,),
        in_specs=[pl.BlockSpec((tm, k), lambda i: (i, 0)),
                  pl.BlockSpec((k, n), lambda i: (0, 0))],
        out_specs=[pl.BlockSpec((tm, q_rank), lambda i: (i, 0)),
                   pl.BlockSpec((tm, kv_rank), lambda i: (i, 0)),
                   pl.BlockSpec((tm, LANES), lambda i: (i, 0))],
        out_shape=[jax.ShapeDtypeStruct((m, q_rank), BF16),
                   jax.ShapeDtypeStruct((m, kv_rank), BF16),
                   jax.ShapeDtypeStruct((m, LANES), F32)],
        compiler_params=_cparams(("arbitrary",)),
        name="latent_proj",
    )(h, w_lat)


def _rope_table_kernel(pos_ref, inv_ref, c_ref, s1_ref, s2_ref):
    half = MLA_ROPE_DIM // 2
    ang = pos_ref[0].astype(F32) * inv_ref[...]
    cos, sin = jnp.cos(ang), jnp.sin(ang)
    lane = lax.broadcasted_iota(I32, ang.shape, 1)
    c_ref[0] = jnp.where(lane < 2 * half, cos, 0.0)
    s1_ref[0] = jnp.where(lane < half, -sin, 0.0)
    s2_ref[0] = jnp.where((lane >= half) & (lane < 2 * half), sin, 0.0)


def _rope_tables(positions):
    bsz, s = positions.shape
    half = MLA_ROPE_DIM // 2
    inv = ROPE_THETA ** (-(jnp.arange(LANES, dtype=F32) % half) / half)
    ts = _tile(s, 512)
    spec = pl.BlockSpec((1, ts, LANES), lambda b, i: (b, i, 0))
    shp = jax.ShapeDtypeStruct((bsz, s, LANES), F32)
    return pl.pallas_call(
        _rope_table_kernel,
        grid=(bsz, s // ts),
        in_specs=[pl.BlockSpec((1, ts, 1), lambda b, i: (b, i, 0)),
                  pl.BlockSpec((1, LANES), lambda b, i: (0, 0))],
        out_specs=[spec, spec, spec],
        out_shape=[shp, shp, shp],
        compiler_params=_cparams(("arbitrary", "arbitrary")),
        name="rope_tables",
    )(positions.reshape(bsz, s, 1), inv.reshape(1, LANES))


def _rotate(r, c, s1, s2):
    half = MLA_ROPE_DIM // 2
    return r * c + pltpu.roll(r, LANES - half, 1) * s1 + pltpu.roll(r, half, 1) * s2


def _rms(x, gain):
    ms = jnp.mean(x * x, axis=-1, keepdims=True)
    return x * lax.rsqrt(ms + NORM_EPS) * gain


def _mla_q_kernel(cq_ref, g_ref, w_ref, c_ref, s1_ref, s2_ref, o_ref, *, heads, scale):
    y = _rms(cq_ref[0].astype(F32), g_ref[...]).astype(BF16)
    q = jnp.dot(y, w_ref[...], preferred_element_type=F32)
    c, s1, s2 = c_ref[0], s1_ref[0], s2_ref[0]
    for h in range(heads):
        base = h * MLA_QK_PAD
        o_ref[0, :, base:base + LANES] = (q[:, base:base + LANES] * scale).astype(BF16)
        r = q[:, base + LANES:base + 2 * LANES]
        o_ref[0, :, base + LANES:base + 2 * LANES] = (_rotate(r, c, s1, s2) * scale).astype(BF16)


def _mla_q(cq, gain, w_q, tabs, scale):
    bsz, s, qr = cq.shape
    heads = MLA_HEADS
    n = heads * MLA_QK_PAD
    tm = _tile(s, 512)
    tab_spec = pl.BlockSpec((1, tm, LANES), lambda b, i: (b, i, 0))
    return pl.pallas_call(
        functools.partial(_mla_q_kernel, heads=heads, scale=scale),
        grid=(bsz, s // tm),
        in_specs=[pl.BlockSpec((1, tm, qr), lambda b, i: (b, i, 0)),
                  pl.BlockSpec((1, qr), lambda b, i: (0, 0)),
                  pl.BlockSpec((qr, n), lambda b, i: (0, 0)),
                  tab_spec, tab_spec, tab_spec],
        out_specs=pl.BlockSpec((1, tm, n), lambda b, i: (b, i, 0)),
        out_shape=jax.ShapeDtypeStruct((bsz, s, n), BF16),
        compiler_params=_cparams(("arbitrary", "arbitrary")),
        name="mla_q_prep",
    )(cq, gain.reshape(1, qr), w_q, *tabs)


def _mla_kv_kernel(ckv_ref, g_ref, wk_ref, wv_ref, kpe_ref, c_ref, s1_ref, s2_ref, k_ref, v_ref, *, heads):
    y = _rms(ckv_ref[0].astype(F32), g_ref[...]).astype(BF16)
    kn = jnp.dot(y, wk_ref[...], preferred_element_type=F32)
    v_ref[0] = jnp.dot(y, wv_ref[...], preferred_element_type=F32).astype(BF16)
    kr = _rotate(kpe_ref[0], c_ref[0], s1_ref[0], s2_ref[0]).astype(BF16)
    for h in range(heads):
        base = h * MLA_QK_PAD
        k_ref[0, :, base:base + LANES] = kn[:, h * LANES:(h + 1) * LANES].astype(BF16)
        k_ref[0, :, base + LANES:base + 2 * LANES] = kr


def _mla_kv(ckv, gain, w_k, w_v, kpe, tabs):
    bsz, s, kvr = ckv.shape
    heads = MLA_HEADS
    tm = _tile(s, 512)
    tab_spec = pl.BlockSpec((1, tm, LANES), lambda b, i: (b, i, 0))
    return pl.pallas_call(
        functools.partial(_mla_kv_kernel, heads=heads),
        grid=(bsz, s // tm),
        in_specs=[pl.BlockSpec((1, tm, kvr), lambda b, i: (b, i, 0)),
                  pl.BlockSpec((1, kvr), lambda b, i: (0, 0)),
                  pl.BlockSpec((kvr, heads * MLA_NOPE_DIM), lambda b, i: (0, 0)),
                  pl.BlockSpec((kvr, heads * MLA_V_DIM), lambda b, i: (0, 0)),
                  tab_spec, tab_spec, tab_spec, tab_spec],
        out_specs=[pl.BlockSpec((1, tm, heads * MLA_QK_PAD), lambda b, i: (b, i, 0)),
                   pl.BlockSpec((1, tm, heads * MLA_V_DIM), lambda b, i: (b, i, 0))],
        out_shape=[jax.ShapeDtypeStruct((bsz, s, heads * MLA_QK_PAD), BF16),
                   jax.ShapeDtypeStruct((bsz, s, heads * MLA_V_DIM), BF16)],
        compiler_params=_cparams(("arbitrary", "arbitrary")),
        name="mla_kv_prep",
    )(ckv, gain.reshape(1, kvr), w_k, w_v, kpe, *tabs)


def _qk(q, k):
    return lax.dot_general(q, k, (((1,), (1,)), ((), ())), preferred_element_type=F32)


def _causal_mask(s, qi, ki, tq, tk):
    row = qi * tq + lax.broadcasted_iota(I32, s.shape, 0)
    col = ki * tk + lax.broadcasted_iota(I32, s.shape, 1)
    return jnp.where(row >= col, s, -jnp.inf)


def _lane_chunks(x):
    return [x[:, c * LANES:(c + 1) * LANES] for c in range(x.shape[1] // LANES)]


def _softmax_update(s, v, m_prev, l_prev, acc_prev):
    chunks = _lane_chunks(s)
    cmax = functools.reduce(jnp.maximum, chunks)
    m_new = jnp.maximum(m_prev, jnp.max(cmax, axis=-1, keepdims=True))
    alpha = jnp.exp2(m_prev - m_new)
    ps = [jnp.exp2(c - m_new) for c in chunks]
    psum = functools.reduce(lambda a, b: a + b, ps)
    l_new = alpha * l_prev + jnp.sum(psum, axis=-1, keepdims=True)
    p = jnp.concatenate([c.astype(BF16) for c in ps], axis=1)
    pv = jnp.dot(p, v, preferred_element_type=F32)
    acc_new = jnp.concatenate([alpha * a for a in _lane_chunks(acc_prev)], axis=1) + pv
    return m_new, l_new, acc_new


def _emit_pipelined(score_fns, update_fns):
    n = len(score_fns)
    pending = [score_fns[i]() for i in range(min(ATTN_LOOKAHEAD, n))]
    for i in range(n):
        if i + ATTN_LOOKAHEAD < n:
            pending.append(score_fns[i + ATTN_LOOKAHEAD]())
        update_fns[i](pending[i])
        pending[i] = None


def _diag_mask(s, r):
    rows = s.shape[0]
    off = s.shape[1] - rows
    row = lax.broadcasted_iota(I32, s.shape, 0)
    col = lax.broadcasted_iota(I32, s.shape, 1)
    return jnp.where(col - off <= row, s, -jnp.inf)


def _slabs(w, steps):
    cols = w.shape[-1]
    rows = w.size // cols
    assert rows % (steps * 2 * SUBLANES) == 0, (w.shape, steps)
    return w.reshape(steps, rows // steps, cols)


def _slab_specs(slabs, step_of):
    return [pl.BlockSpec((1,) + w.shape[1:], lambda *g: (step_of(*g), 0, 0)) for w in slabs]


def _cast_slabs(src_refs, dst_refs):
    for src, dst in zip(src_refs, dst_refs):
        dst[...] = src[...].astype(BF16)


def _mla_attn_kernel(q_ref, k_ref, v_ref, *rest, tq, tk, rb, n_cast):
    cast_in, o_ref, cast_out, scr = (rest[:n_cast], rest[n_cast], rest[n_cast + 1:2 * n_cast + 1],
                                     rest[2 * n_cast + 1:])
    qi = pl.program_id(2)
    nr = tq // rb
    state = [scr[3 * r:3 * r + 3] for r in range(nr)]
    for m_scr, l_scr, acc_scr in state:
        m_scr[...] = jnp.full(m_scr.shape, -jnp.inf, F32)
        l_scr[...] = jnp.zeros(l_scr.shape, F32)
        acc_scr[...] = jnp.zeros(acc_scr.shape, F32)

    def rows(r):
        return pl.ds(r * rb, rb)

    def update(r, s, v):
        m_scr, l_scr, acc_scr = state[r]
        m, l, acc = _softmax_update(s, v, m_scr[...], l_scr[...], acc_scr[...])
        m_scr[...] = m
        l_scr[...] = l
        acc_scr[...] = acc

    def full_tile(ki, carry):
        start = pl.multiple_of(ki * tk, tk)
        k = k_ref[0, pl.ds(start, tk), :]
        v = v_ref[0, pl.ds(start, tk), :]
        _emit_pipelined([functools.partial(_qk, q_ref[0, rows(r), :], k) for r in range(nr)],
                        [functools.partial(update, r, v=v) for r in range(nr)])
        return carry
    lax.fori_loop(0, qi * (tq // tk), full_tile, 0)

    _cast_slabs(cast_in, cast_out)
    dstart = pl.multiple_of(qi * tq, tq)
    cols = [pl.ds(dstart, (r + 1) * rb) for r in range(nr)]
    _emit_pipelined(
        [lambda r=r: _diag_mask(_qk(q_ref[0, rows(r), :], k_ref[0, cols[r], :]), r) for r in range(nr)],
        [lambda s, r=r: update(r, s, v_ref[0, cols[r], :]) for r in range(nr)])

    for r, (_, l_scr, acc_scr) in enumerate(state):
        o_ref[0, rows(r), :] = (acc_scr[...] / l_scr[...]).astype(o_ref.dtype)


def _attn_state_scratch(nr, rb, dv):
    return [pltpu.VMEM((rb, w), F32) for _ in range(nr) for w in (LANES, LANES, dv)]


def _mla_attention(q, k, v, cast=()):
    bsz, s, _ = q.shape
    heads = MLA_HEADS
    t = _tile(s, 1024)
    rb = _tile(t, 256)
    nq = s // t
    slabs = [_slabs(w, bsz * heads * nq) for w in cast]
    slab_specs = _slab_specs(slabs, lambda b, h, i: (b * heads + h) * nq + i)
    out = pl.pallas_call(
        functools.partial(_mla_attn_kernel, tq=t, tk=t, rb=rb, n_cast=len(cast)),
        grid=(bsz, heads, nq),
        in_specs=[pl.BlockSpec((1, t, MLA_QK_PAD), lambda b, h, i: (b, i, h)),
                  pl.BlockSpec((1, s, MLA_QK_PAD), lambda b, h, i: (b, 0, h)),
                  pl.BlockSpec((1, s, MLA_V_DIM), lambda b, h, i: (b, 0, h))] + slab_specs,
        out_specs=[pl.BlockSpec((1, t, MLA_V_DIM), lambda b, h, i: (b, i, h))] + slab_specs,
        out_shape=[jax.ShapeDtypeStruct((bsz, s, heads * MLA_V_DIM), BF16)]
                  + [jax.ShapeDtypeStruct(w.shape, BF16) for w in slabs],
        scratch_shapes=_attn_state_scratch(t // rb, rb, MLA_V_DIM),
        compiler_params=_cparams(("arbitrary", "arbitrary", "arbitrary")),
        name="mla_attn",
    )(q, k, v, *slabs)
    return [out[0]] + [o.reshape(w.shape) for o, w in zip(out[1:], cast)]


def _diff_attn_kernel(q_ref, k_ref, v_ref, qpos_ref, kpos_ref, slope_ref, lam_ref, subln_ref, *rest,
                      tq, tk, rb, lam_init, n_cast):
    cast_in, o_ref, cast_out, scr = (rest[:n_cast], rest[n_cast], rest[n_cast + 1:2 * n_cast + 1],
                                     rest[2 * n_cast + 1:])
    qi = pl.program_id(2)
    d = DIFF_HEAD_DIM
    nr = tq // rb
    state = [[scr[6 * r + 3 * g:6 * r + 3 * g + 3] for g in range(2)] for r in range(nr)]
    for r in range(nr):
        for m_scr, l_scr, a_scr in state[r]:
            m_scr[...] = jnp.full(m_scr.shape, -jnp.inf, F32)
            l_scr[...] = jnp.zeros(l_scr.shape, F32)
            a_scr[...] = jnp.zeros(a_scr.shape, F32)
    slope = slope_ref[0, :, 0:1]

    def rows(r):
        return pl.ds(r * rb, rb)

    def kpos(first, count):
        return jnp.concatenate([kpos_ref[0, first + j] for j in range(count)], axis=1)

    chains = [(r, g) for r in range(nr) for g in range(2)]

    def emit(k_of, v_of, kp_of, masked):
        bias = {}

        def score(r, g):
            if g == 0:
                bias[r] = slope * jnp.abs(qpos_ref[0, rows(r), :] - kp_of(r))
            s = _qk(q_ref[0, rows(r), g * d:(g + 1) * d], k_of(r)[:, g * d:(g + 1) * d]) - bias[r]
            return _diag_mask(s, r) if masked else s

        def update(r, g, s):
            m_scr, l_scr, a_scr = state[r][g]
            m, l, acc = _softmax_update(s, v_of(r), m_scr[...], l_scr[...], a_scr[...])
            m_scr[...] = m
            l_scr[...] = l
            a_scr[...] = acc

        _emit_pipelined([functools.partial(score, r, g) for r, g in chains],
                        [functools.partial(update, r, g) for r, g in chains])

    def full_tile(ki, carry):
        start = pl.multiple_of(ki * tk, tk)
        k = k_ref[0, pl.ds(start, tk), :]
        v = v_ref[0, pl.ds(start, tk), :]
        kp = kpos(ki * (tk // rb), tk // rb)
        emit(lambda r: k, lambda r: v, lambda r: kp, False)
        return carry
    lax.fori_loop(0, qi * (tq // tk), full_tile, 0)

    _cast_slabs(cast_in, cast_out)
    dstart = pl.multiple_of(qi * tq, tq)
    cols = [pl.ds(dstart, (r + 1) * rb) for r in range(nr)]
    emit(lambda r: k_ref[0, cols[r], :], lambda r: v_ref[0, cols[r], :], lambda r: kpos(qi * nr, r + 1), True)

    lp = lam_ref[...]
    e1 = jnp.exp(jnp.sum(lp[0:1] * lp[1:2], axis=-1, keepdims=True))
    e2 = jnp.exp(jnp.sum(lp[2:3] * lp[3:4], axis=-1, keepdims=True))
    lam = e1 - e2 + lam_init
    for r in range(nr):
        (_, l1, a1), (_, l2, a2) = state[r]
        o1 = [a / l1[...] for a in _lane_chunks(a1[...])]
        o2 = [a / l2[...] for a in _lane_chunks(a2[...])]
        o = jnp.concatenate([x - lam * y for x, y in zip(o1, o2)], axis=1)
        o_ref[0, rows(r), :] = (_rms(o, subln_ref[...]) * (1.0 - lam_init)).astype(o_ref.dtype)


def _diff_attention(qkv, positions, diff_lambda, subln, lam_init, cast=()):
    bsz, s, _ = qkv.shape
    heads = DIFF_HEADS
    dv = 2 * DIFF_HEAD_DIM
    t = _tile(s, 1024)
    tk = _tile(t, 512)
    rb = _tile(tk, 256)
    nq = s // t
    slopes = 2.0 ** (-8.0 * jnp.arange(1, heads + 1, dtype=F32) / heads) * LOG2_E
    slopes = jnp.broadcast_to(slopes[:, None, None], (heads, 1, LANES))
    posf = positions.astype(F32)
    slabs = [_slabs(w, bsz * heads * nq) for w in cast]
    slab_specs = _slab_specs(slabs, lambda b, h, i: (b * heads + h) * nq + i)
    out = pl.pallas_call(
        functools.partial(_diff_attn_kernel, tq=t, tk=tk, rb=rb, lam_init=lam_init, n_cast=len(cast)),
        grid=(bsz, heads, nq),
        in_specs=[pl.BlockSpec((1, t, dv), lambda b, h, i: (b, i, h)),
                  pl.BlockSpec((1, s, dv), lambda b, h, i: (b, 0, heads + h)),
                  pl.BlockSpec((1, s, dv), lambda b, h, i: (b, 0, 2 * heads + h)),
                  pl.BlockSpec((1, t, 1), lambda b, h, i: (b, i, 0)),
                  pl.BlockSpec((1, s // rb, 1, rb), lambda b, h, i: (b, 0, 0, 0)),
                  pl.BlockSpec((1, 1, LANES), lambda b, h, i: (h, 0, 0)),
                  pl.BlockSpec((4, DIFF_HEAD_DIM), lambda b, h, i: (0, 0)),
                  pl.BlockSpec((1, dv), lambda b, h, i: (0, 0))] + slab_specs,
        out_specs=[pl.BlockSpec((1, t, dv), lambda b, h, i: (b, i, h))] + slab_specs,
        out_shape=[jax.ShapeDtypeStruct((bsz, s, heads * dv), BF16)]
                  + [jax.ShapeDtypeStruct(w.shape, BF16) for w in slabs],
        scratch_shapes=_attn_state_scratch(2 * (t // rb), rb, dv),
        compiler_params=_cparams(("arbitrary", "arbitrary", "arbitrary")),
        name="diff_attn",
    )(qkv, qkv, qkv, posf.reshape(bsz, s, 1), posf.reshape(bsz, s // rb, 1, rb), slopes, diff_lambda,
      subln.reshape(1, dv), *slabs)
    return [out[0]] + [o.reshape(w.shape) for o, w in zip(out[1:], cast)]


def _merge_kernel(od_ref, om_ref, wd_ref, wm_ref, g0_ref, g1_ref, x_ref, gt_ref, o_ref):
    yd = jnp.dot(od_ref[0], wd_ref[...], preferred_element_type=F32)
    ym = jnp.dot(om_ref[0], wm_ref[...], preferred_element_type=F32)
    y = g0_ref[0].astype(F32) * yd + g1_ref[0].astype(F32) * ym
    o_ref[0] = x_ref[0] + gt_ref[0, 0] * y


def _merge(o_d, o_m, w_d, w_m, gates, x, mod, gt_idx):
    bsz, s, d = x.shape
    kd, km = o_d.shape[-1], o_m.shape[-1]
    tm, tn = _tile(s, 512), _tile(d, 512)
    nj = d // tn
    return pl.pallas_call(
        _merge_kernel,
        grid=(bsz, s // tm, nj),
        in_specs=[pl.BlockSpec((1, tm, kd), lambda b, i, j: (b, i, 0)),
                  pl.BlockSpec((1, tm, km), lambda b, i, j: (b, i, 0)),
                  pl.BlockSpec((kd, tn), lambda b, i, j: (0, j)),
                  pl.BlockSpec((km, tn), lambda b, i, j: (0, j)),
                  pl.BlockSpec((1, tm, tn), lambda b, i, j: (b, i, j)),
                  pl.BlockSpec((1, tm, tn), lambda b, i, j: (b, i, j + nj)),
                  pl.BlockSpec((1, tm, tn), lambda b, i, j: (b, i, j)),
                  pl.BlockSpec((1, 1, 1, tn), lambda b, i, j: (gt_idx, b, 0, j))],
        out_specs=pl.BlockSpec((1, tm, tn), lambda b, i, j: (b, i, j)),
        out_shape=jax.ShapeDtypeStruct((bsz, s, d), F32),
        compiler_params=_cparams(("arbitrary", "arbitrary", "arbitrary")),
        name="out_merge",
    )(o_d, o_m, w_d, w_m, gates, gates, x, mod)


def _pack_rows(y):
    half = y.shape[1] // 2
    bits = lax.bitcast_convert_type(y.astype(BF16).astype(F32), U32)
    return (bits[:, half:] & jnp.uint32(0xFFFF0000)) | (bits[:, :half] >> 16)


def _unpack_rows(w):
    lo = lax.bitcast_convert_type(w << 16, F32)
    hi = lax.bitcast_convert_type(w & jnp.uint32(0xFFFF0000), F32)
    return lo, hi


def _ffn_norm_kernel(x_ref, g_ref, sc_ref, sh_ref, rw_ref, h_ref, hp_ref, lt_ref, *, nch):
    x = x_ref[0]
    y = _rms(x, g_ref[...]) * (1.0 + sc_ref[0, 0]) + sh_ref[0, 0]
    hb = y.astype(BF16)
    h_ref[0] = hb
    lt_ref[...] = lax.dot_general(rw_ref[...], hb, (((1,), (1,)), ((), ())), preferred_element_type=F32)
    packed = _pack_rows(y)
    for c in range(nch):
        hp_ref[pl.ds(c, x.shape[0], stride=nch), :] = packed[:, c * LANES:(c + 1) * LANES]


def _ffn_norm(x, gain, mod, sc_idx, sh_idx, router_wt):
    bsz, s, d = x.shape
    e = router_wt.shape[0]
    nch = d // (2 * LANES)
    tm = _tile(s, 256)
    ns = s // tm
    return pl.pallas_call(
        functools.partial(_ffn_norm_kernel, nch=nch),
        grid=(bsz, ns),
        in_specs=[pl.BlockSpec((1, tm, d), lambda b, i: (b, i, 0)),
                  pl.BlockSpec((1, d), lambda b, i: (0, 0)),
                  pl.BlockSpec((1, 1, 1, d), lambda b, i: (sc_idx, b, 0, 0)),
                  pl.BlockSpec((1, 1, 1, d), lambda b, i: (sh_idx, b, 0, 0)),
                  pl.BlockSpec((e, d), lambda b, i: (0, 0))],
        out_specs=[pl.BlockSpec((1, tm, d), lambda b, i: (b, i, 0)),
                   pl.BlockSpec((tm * nch, LANES), lambda b, i: (b * ns + i, 0)),
                   pl.BlockSpec((e, tm), lambda b, i: (0, b * ns + i))],
        out_shape=[jax.ShapeDtypeStruct((bsz, s, d), BF16),
                   jax.ShapeDtypeStruct((bsz * s * nch, LANES), U32),
                   jax.ShapeDtypeStruct((e, bsz * s), F32)],
        compiler_params=_cparams(("arbitrary", "arbitrary")),
        name="ffn_norm_router",
    )(x, gain.reshape(1, d), mod, mod, router_wt)


def _first_index(hit, iota, axis, size):
    return jnp.min(jnp.where(hit, iota, size), axis=axis, keepdims=True)


def _route_kernel(lt_ref, bias_ref, idx_ref, wt_ref, rank_ref, sizes_ref, cnt_scr, *, n_exp, tn):
    i = pl.program_id(0)
    gsz = n_exp // N_GROUPS

    @pl.when(i == 0)
    def _():
        cnt_scr[...] = jnp.zeros(cnt_scr.shape, F32)

    scores = jax.nn.sigmoid(lt_ref[...])
    sel = scores + bias_ref[...]
    sel3 = sel.reshape(N_GROUPS, gsz, tn)
    j3 = lax.broadcasted_iota(I32, sel3.shape, 1)
    top1 = jnp.max(sel3, axis=1, keepdims=True)
    first = _first_index(sel3 == top1, j3, 1, gsz)
    top2 = jnp.max(jnp.where(j3 == first, -jnp.inf, sel3), axis=1, keepdims=True)
    gscore = (top1 + top2).reshape(N_GROUPS, tn)

    giota = lax.broadcasted_iota(I32, gscore.shape, 0)
    gmask = jnp.zeros(gscore.shape, jnp.bool_)
    for _ in range(TOPK_GROUPS):
        best = jnp.max(gscore, axis=0, keepdims=True)
        gi = _first_index(gscore == best, giota, 0, N_GROUPS)
        hit = giota == gi
        gmask = gmask | hit
        gscore = jnp.where(hit, -jnp.inf, gscore)

    emask = jnp.broadcast_to(gmask.reshape(N_GROUPS, 1, tn), sel3.shape)
    cand = jnp.where(emask, sel3, -jnp.inf).reshape(n_exp, tn)
    eiota = lax.broadcasted_iota(I32, cand.shape, 0)
    hits, idxs, vals = [], [], []
    for _ in range(TOP_K):
        best = jnp.max(cand, axis=0, keepdims=True)
        ei = _first_index(cand == best, eiota, 0, n_exp)
        hit = eiota == ei
        hits.append(hit)
        idxs.append(ei)
        vals.append(jnp.sum(jnp.where(hit, scores, 0.0), axis=0, keepdims=True))
        cand = jnp.where(hit, -jnp.inf, cand)
    total = functools.reduce(lambda a, b: a + b, vals)

    chosen = functools.reduce(lambda a, b: a | b, hits)
    onehot = jnp.where(chosen, 1.0, 0.0)
    r = lax.broadcasted_iota(I32, (tn, tn), 0)
    c = lax.broadcasted_iota(I32, (tn, tn), 1)
    upper = jnp.where(r < c, 1.0, 0.0).astype(BF16)
    before = cnt_scr[...] + jnp.dot(onehot.astype(BF16), upper, preferred_element_type=F32)
    cnt_new = cnt_scr[...] + jnp.sum(onehot, axis=1, keepdims=True)
    cnt_scr[...] = cnt_new

    pad = SUBLANES - TOP_K
    ranks = [jnp.sum(jnp.where(h, before, 0.0), axis=0, keepdims=True).astype(I32) for h in hits]
    zi = [jnp.zeros((pad, tn), I32)]
    idx_ref[...] = jnp.concatenate(idxs + zi, axis=0)
    rank_ref[...] = jnp.concatenate(ranks + zi, axis=0)
    wt_ref[...] = jnp.concatenate([v / total * ROUTED_SCALE for v in vals] + [jnp.zeros((pad, tn), F32)], axis=0)
    sizes_ref[...] = jnp.broadcast_to(cnt_new, sizes_ref.shape).astype(I32)


def _route(logits_t, bias):
    n_exp, n = logits_t.shape
    tn = _tile(n, 512)
    row_spec = pl.BlockSpec((SUBLANES, tn), lambda i: (0, i))
    return pl.pallas_call(
        functools.partial(_route_kernel, n_exp=n_exp, tn=tn),
        grid=(n // tn,),
        in_specs=[pl.BlockSpec((n_exp, tn), lambda i: (0, i)),
                  pl.BlockSpec((n_exp, 1), lambda i: (0, 0))],
        out_specs=[row_spec, row_spec, row_spec, pl.BlockSpec((n_exp, LANES), lambda i: (0, 0))],
        out_shape=[jax.ShapeDtypeStruct((SUBLANES, n), I32),
                   jax.ShapeDtypeStruct((SUBLANES, n), F32),
                   jax.ShapeDtypeStruct((SUBLANES, n), I32),
                   jax.ShapeDtypeStruct((n_exp, LANES), I32)],
        scratch_shapes=[pltpu.VMEM((n_exp, 1), F32)],
        compiler_params=_cparams(("arbitrary",)),
        name="route_topk",
    )(logits_t, bias.reshape(n_exp, 1))


def _dispatch_kernel(dest_ref, seg_ref, hp_ref, xs_ref, zero_scr, sem, zsem, *, n_tok, tc, nch, n_exp, blk, nb):
    i = pl.program_id(0)

    def row_copy(j, slot):
        return pltpu.make_async_copy(hp_ref.at[pl.ds(pl.multiple_of(j * nch, nch), nch)],
                                     xs_ref.at[pl.ds(pl.multiple_of(slot * nch, nch), nch)], sem)

    def zero_row(slot):
        return pltpu.make_async_copy(zero_scr.at[pl.ds(0, nch)],
                                     xs_ref.at[pl.ds(pl.multiple_of(slot * nch, nch), nch)], zsem)

    def zero_block(b):
        return pltpu.make_async_copy(zero_scr,
                                     xs_ref.at[pl.ds(pl.multiple_of(b * (blk * nch), blk * nch), blk * nch)], zsem)

    def start_then_wait(lo, hi, copy):
        def start(v, c):
            copy(v).start()
            return c
        lax.fori_loop(lo, hi, start, 0)

        def wait(v, c):
            copy(v).wait()
            return c
        lax.fori_loop(lo, hi, wait, 0)

    @pl.when(i == 0)
    def _():
        zero_scr[...] = jnp.zeros(zero_scr.shape, U32)

        def per_expert(e, carry):
            start_then_wait(seg_ref[e], seg_ref[n_exp + e], zero_row)
            return carry
        lax.fori_loop(0, n_exp, per_expert, 0)
        start_then_wait(seg_ref[2 * n_exp], nb, zero_block)

    base = i * tc

    def start(j, c):
        for k in range(TOP_K):
            row_copy(j, dest_ref[k * n_tok + base + j]).start()
        return c
    lax.fori_loop(0, tc, start, 0, unroll=DMA_ISSUE_UNROLL)

    def wait(j, c):
        for k in range(TOP_K):
            row_copy(0, 0).wait()
        return c
    lax.fori_loop(0, tc, wait, 0)


def _dispatch(dest, seg, hp, nb, blk, nch, n_exp):
    n_tok = dest.shape[0] // TOP_K
    tc = _tile(n_tok, 256)
    return pl.pallas_call(
        functools.partial(_dispatch_kernel, n_tok=n_tok, tc=tc, nch=nch, n_exp=n_exp, blk=blk, nb=nb),
        grid_spec=pltpu.PrefetchScalarGridSpec(
            num_scalar_prefetch=2,
            grid=(n_tok // tc,),
            in_specs=[pl.BlockSpec((tc * nch, LANES), lambda i, dst, sg: (i, 0))],
            out_specs=pl.BlockSpec(memory_space=pl.ANY),
            scratch_shapes=[pltpu.VMEM((blk * nch, LANES), U32), pltpu.SemaphoreType.DMA,
                            pltpu.SemaphoreType.DMA]),
        out_shape=jax.ShapeDtypeStruct((nb * blk * nch, LANES), U32),
        compiler_params=_cparams(("arbitrary",)),
        name="moe_dispatch",
    )(dest, seg, hp)


def _load_rows(ref, tm, nch):
    los, his = [], []
    for c in range(nch):
        lo, hi = _unpack_rows(ref[pl.ds(c, tm, stride=nch), :])
        los.append(lo.astype(BF16))
        his.append(hi.astype(BF16))
    return jnp.concatenate(los + his, axis=1)


def _expert_kernel(be_ref, nact_ref, xs_ref, w1_ref, w3_ref, w2_ref, ys_ref, *, tm, nch):
    active = pl.program_id(0) < nact_ref[0]

    @pl.when(jnp.logical_not(active))
    def _():
        ys_ref[...] = jnp.zeros(ys_ref.shape, U32)

    @pl.when(active)
    def _():
        x = _load_rows(xs_ref, tm, nch)
        a = jnp.dot(x, w1_ref[0], preferred_element_type=F32)
        b = jnp.dot(x, w3_ref[0], preferred_element_type=F32)
        y = jnp.dot((_silu(a) * b).astype(BF16), w2_ref[0], preferred_element_type=F32)
        packed = _pack_rows(y)
        for c in range(nch):
            ys_ref[pl.ds(c, tm, stride=nch), :] = packed[:, c * LANES:(c + 1) * LANES]


def _experts(block_expert, nact, xs, w1, w3, w2, nch):
    n_exp, d, f = w1.shape
    tm = EXPERT_ROWS
    nb = xs.shape[0] // (tm * nch)
    blk = lambda i, be, na: (jnp.minimum(i, na[0] - 1), 0)
    wsel = lambda i, be, na: (be[jnp.minimum(i, na[0] - 1)], 0, 0)
    return pl.pallas_call(
        functools.partial(_expert_kernel, tm=tm, nch=nch),
        grid_spec=pltpu.PrefetchScalarGridSpec(
            num_scalar_prefetch=2,
            grid=(nb,),
            in_specs=[pl.BlockSpec((tm * nch, LANES), blk),
                      pl.BlockSpec((1, d, f), wsel),
                      pl.BlockSpec((1, d, f), wsel),
                      pl.BlockSpec((1, f, d), wsel)],
            out_specs=pl.BlockSpec((tm * nch, LANES), lambda i, be, na: (i, 0))),
        out_shape=jax.ShapeDtypeStruct(xs.shape, U32),
        compiler_params=_cparams(("arbitrary",)),
        name="moe_experts",
    )(block_expert, nact, xs, w1, w3, w2)


def _combine_kernel(dest_ref, ys_ref, wt_ref, h_ref, w1_ref, w3_ref, w2_ref, x_ref, gt_ref, fn_ref, o_ref,
                    ybuf0, ybuf1, sem0, sem1, *, n_tok, tm, nch, nsteps, final):
    step = pl.program_id(0) * pl.num_programs(1) + pl.program_id(1)
    bufs = ((ybuf0, sem0), (ybuf1, sem1))

    def row_copy(buf, sem, slot, k, j):
        return pltpu.make_async_copy(ys_ref.at[pl.ds(pl.multiple_of(slot * nch, nch), nch)],
                                     buf.at[k, pl.ds(pl.multiple_of(j * nch, nch), nch)], sem)

    def start_gathers(tile, buf, sem):
        base = tile * tm

        def start(j, c):
            for k in range(TOP_K):
                row_copy(buf, sem, dest_ref[k * n_tok + base + j], k, j).start()
            return c
        lax.fori_loop(0, tm, start, 0, unroll=DMA_ISSUE_UNROLL)

    def wait_gathers(buf, sem):
        def wait(j, c):
            for k in range(TOP_K):
                row_copy(buf, sem, 0, k, 0).wait()
            return c
        lax.fori_loop(0, tm, wait, 0)

    @pl.when(step == 0)
    def _():
        start_gathers(0, *bufs[0])

    def consume(cur, nxt):
        start_gathers(jnp.minimum(step + 1, nsteps - 1), *nxt)
        h = h_ref[0]
        a = jnp.dot(h, w1_ref[...], preferred_element_type=F32)
        g = jnp.dot(h, w3_ref[...], preferred_element_type=F32)
        ffn = jnp.dot((_silu(a) * g).astype(BF16), w2_ref[...], preferred_element_type=F32)
        wait_gathers(*cur)
        ybuf = cur[0]
        wts = wt_ref[...]
        los = [None] * nch
        his = [None] * nch
        for k in range(TOP_K):
            wk = wts[:, k:k + 1]
            for c in range(nch):
                lo, hi = _unpack_rows(ybuf[k, pl.ds(c, tm, stride=nch), :])
                los[c] = wk * lo if k == 0 else los[c] + wk * lo
                his[c] = wk * hi if k == 0 else his[c] + wk * hi
        routed = jnp.concatenate(los + his, axis=1)
        y = x_ref[0] + gt_ref[0, 0] * (routed + ffn)
        o_ref[0] = _rms(y, fn_ref[...]) if final else y

    pl.when(step % 2 == 0)(lambda: consume(bufs[0], bufs[1]))
    pl.when(step % 2 == 1)(lambda: consume(bufs[1], bufs[0]))

    @pl.when(step == nsteps - 1)
    def _():
        wait_gathers(*bufs[nsteps % 2])


def _combine(dest, ys, wts, h, w1, w3, w2, x, mod, gt_idx, final_norm, nch, final):
    bsz, s, d = x.shape
    f = w1.shape[1]
    n_tok = bsz * s
    tm = _tile(s, 128)
    ns = s // tm
    return pl.pallas_call(
        functools.partial(_combine_kernel, n_tok=n_tok, tm=tm, nch=nch, nsteps=bsz * ns, final=final),
        grid_spec=pltpu.PrefetchScalarGridSpec(
            num_scalar_prefetch=1,
            grid=(bsz, ns),
            in_specs=[pl.BlockSpec(memory_space=pl.ANY),
                      pl.BlockSpec((tm, SUBLANES), lambda b, i, dst: (b * ns + i, 0)),
                      pl.BlockSpec((1, tm, d), lambda b, i, dst: (b, i, 0)),
                      pl.BlockSpec((d, f), lambda b, i, dst: (0, 0)),
                      pl.BlockSpec((d, f), lambda b, i, dst: (0, 0)),
                      pl.BlockSpec((f, d), lambda b, i, dst: (0, 0)),
                      pl.BlockSpec((1, tm, d), lambda b, i, dst: (b, i, 0)),
                      pl.BlockSpec((1, 1, 1, d), lambda b, i, dst: (gt_idx, b, 0, 0)),
                      pl.BlockSpec((1, d), lambda b, i, dst: (0, 0))],
            out_specs=pl.BlockSpec((1, tm, d), lambda b, i, dst: (b, i, 0)),
            scratch_shapes=[pltpu.VMEM((TOP_K, tm * nch, LANES), U32), pltpu.VMEM((TOP_K, tm * nch, LANES), U32),
                            pltpu.SemaphoreType.DMA, pltpu.SemaphoreType.DMA]),
        out_shape=jax.ShapeDtypeStruct((bsz, s, d), F32),
        compiler_params=_cparams(("arbitrary", "arbitrary")),
        name="moe_combine",
    )(dest, ys, wts, h, w1, w3, w2, x, mod, final_norm.reshape(1, d))


def _mla_weights(w_uq, w_ukv):
    heads = MLA_HEADS
    qr, kvr = w_uq.shape[0], w_ukv.shape[0]
    wq = w_uq.reshape(qr, heads, MLA_NOPE_DIM + MLA_ROPE_DIM)
    wq = jnp.pad(wq, ((0, 0), (0, 0), (0, MLA_QK_PAD - MLA_NOPE_DIM - MLA_ROPE_DIM)))
    wkv = w_ukv.reshape(kvr, heads, MLA_NOPE_DIM + MLA_V_DIM)
    wk = wkv[:, :, :MLA_NOPE_DIM].reshape(kvr, heads * MLA_NOPE_DIM)
    wv = wkv[:, :, MLA_NOPE_DIM:].reshape(kvr, heads * MLA_V_DIM)
    return (wq.reshape(qr, heads * MLA_QK_PAD).astype(BF16), wk.astype(BF16), wv.astype(BF16))


def _layer(x, mod, positions, tabs, l, norm_attn, w_in, diff_lambda, diff_subln, mla_q_norm, mla_w_uq,
           mla_kv_norm, mla_w_ukv, w_out, norm_ffn, router_w, router_bias, exp_w1, exp_w3, exp_w2,
           shared_w1, shared_w3, shared_w2, final_norm, final):
    bsz, s, d = x.shape
    n_tok = bsz * s
    q_rank, kv_rank = mla_w_uq.shape[0], mla_w_ukv.shape[0]
    qk_cols = 2 * DIFF_HEADS * DIFF_HEAD_DIM
    v_cols = DIFF_HEADS * 2 * DIFF_HEAD_DIM
    qkv_cols = 2 * qk_cols + v_cols
    lat_cols = q_rank + kv_rank + MLA_ROPE_DIM
    lam_init = 0.8 - 0.6 * math.exp(-0.3 * l)

    h = _norm_mod(x, norm_attn, mod, 1, 0).reshape(n_tok, d)
    w_qkv = w_in[:, :qkv_cols].astype(BF16)
    w_lat = jnp.pad(w_in[:, qkv_cols:qkv_cols + lat_cols], ((0, 0), (0, LANES - MLA_ROPE_DIM))).astype(BF16)
    w_gate = w_in[:, qkv_cols + lat_cols:].astype(BF16)
    qkv = _matmul(h, w_qkv, BF16, name="qkv_proj", scaled_cols=qk_cols,
                  col_scale=DIFF_HEAD_DIM ** -0.5 * LOG2_E).reshape(bsz, s, qkv_cols)
    cq, ckv, kpe = _latent_proj(h, w_lat, q_rank, kv_rank)
    gates = _matmul(h, w_gate, BF16, name="gate_proj", sigmoid=True).reshape(bsz, s, 2 * d)

    o_d, exp_w2b = _diff_attention(qkv, positions, diff_lambda, diff_subln, lam_init, cast=(exp_w2,))

    wq, wk, wv = _mla_weights(mla_w_uq, mla_w_ukv)
    scale = (MLA_NOPE_DIM + MLA_ROPE_DIM) ** -0.5 * LOG2_E
    q_m = _mla_q(cq.reshape(bsz, s, q_rank), mla_q_norm, wq, tabs, scale)
    k_m, v_m = _mla_kv(ckv.reshape(bsz, s, kv_rank), mla_kv_norm, wk, wv, kpe.reshape(bsz, s, LANES), tabs)
    o_m, exp_w1b, exp_w3b = _mla_attention(q_m, k_m, v_m, cast=(exp_w1, exp_w3))

    w_o = w_out.astype(BF16)
    x = _merge(o_d, o_m, w_o[:v_cols], w_o[v_cols:], gates, x, mod, 2)

    n_exp = router_w.shape[1]
    nch = d // (2 * LANES)
    h2, hp, logits_t = _ffn_norm(x, norm_ffn, mod, 4, 3, router_w.T.astype(BF16))
    idx_t, wts_t, rank_t, sizes = _route(logits_t, router_bias)

    blk = EXPERT_ROWS
    sizes = sizes[:, 0]
    padded = (sizes + blk - 1) // blk * blk
    pad_end = jnp.cumsum(padded)
    pad_start = pad_end - padded
    onehot = idx_t[:TOP_K, :, None] == jnp.arange(n_exp, dtype=I32)
    dest = (jnp.sum(jnp.where(onehot, pad_start, 0), axis=-1) + rank_t[:TOP_K]).astype(I32).reshape(-1)
    n_blocks = -(-n_tok * TOP_K // blk) + n_exp
    block_start = jnp.arange(n_blocks, dtype=I32) * blk
    block_expert = jnp.minimum(jnp.sum(pad_end[None, :] <= block_start[:, None], axis=1), n_exp - 1).astype(I32)
    nact = (pad_end[-1:] // blk).astype(I32)
    seg = jnp.concatenate([pad_start + sizes, pad_end, nact]).astype(I32)

    xs = _dispatch(dest, seg, hp, n_blocks, blk, nch, n_exp)
    ys = _experts(block_expert, nact, xs, exp_w1b, exp_w3b, exp_w2b, nch)
    return _combine(dest, ys, wts_t.T, h2, shared_w1.astype(BF16), shared_w3.astype(BF16),
                    shared_w2.astype(BF16), x, mod, 5, final_norm, nch, final)


def kernel(x, c, positions, w_ada, b_ada, norm_attn, w_in, diff_lambda, diff_subln, mla_q_norm, mla_w_uq,
           mla_kv_norm, mla_w_ukv, w_out, norm_ffn, router_w, router_bias, exp_w1, exp_w3, exp_w2,
           shared_w1, shared_w3, shared_w2, final_norm):
    bsz, s, d = x.shape
    depth = w_ada.shape[0]
    tabs = _rope_tables(positions)
    for l in range(depth):
        mod = _ada(c, w_ada[l], b_ada[l])
        mod = mod.reshape(bsz, N_MOD, 1, d).transpose(1, 0, 2, 3)
        x = _layer(x, mod, positions, tabs, l, norm_attn[l], w_in[l], diff_lambda[l], diff_subln[l],
                   mla_q_norm[l], mla_w_uq[l], mla_kv_norm[l], mla_w_ukv[l], w_out[l], norm_ffn[l],
                   router_w[l], router_bias[l], exp_w1[l], exp_w3[l], exp_w2[l],
                   shared_w1[l], shared_w3[l], shared_w2[l], final_norm, l == depth - 1)
    return x
```

```python
import functools
import math

import jax
import jax.numpy as jnp
from jax import lax
from jax.experimental import pallas as pl
from jax.experimental.pallas import tpu as pltpu

F32 = jnp.float32
BF16 = jnp.bfloat16
U32 = jnp.uint32
I32 = jnp.int32

DIFF_HEADS = 8
DIFF_HEAD_DIM = 128
MLA_HEADS = 16
MLA_NOPE_DIM = 128
MLA_ROPE_DIM = 64
MLA_V_DIM = 128
ROPE_THETA = 10000.0
TOP_K = 6
N_GROUPS = 8
TOPK_GROUPS = 4
ROUTED_SCALE = 2.5
NORM_EPS = 1e-6
N_MOD = 6
LOG2_E = math.log2(math.e)

LANES = 128
SUBLANES = 8
MLA_QK_PAD = 256
EXPERT_ROWS = 256
DMA_ISSUE_UNROLL = 4
DMA_QUEUES = 2
ATTN_LOOKAHEAD = 2
VMEM_LIMIT = 56 * 1024 * 1024


def _cparams(sem):
    return pltpu.CompilerParams(dimension_semantics=sem, vmem_limit_bytes=VMEM_LIMIT)


def _tile(n, pref):
    t = min(n, pref)
    assert n % t == 0, (n, pref)
    return t


def _silu(a):
    return a * jax.nn.sigmoid(a)


def _ada_kernel(c_ref, w_ref, b_ref, o_ref):
    c = c_ref[...]
    a = _silu(c).astype(BF16)
    o_ref[...] = jnp.dot(a, w_ref[...].astype(BF16), preferred_element_type=F32) + b_ref[...]


def _ada(c, w, b):
    bsz, d = c.shape
    n = w.shape[1]
    tn = _tile(n, 512)
    return pl.pallas_call(
        _ada_kernel,
        grid=(n // tn,),
        in_specs=[pl.BlockSpec((bsz, d), lambda j: (0, 0)),
                  pl.BlockSpec((d, tn), lambda j: (0, j)),
                  pl.BlockSpec((1, tn), lambda j: (0, j))],
        out_specs=pl.BlockSpec((bsz, tn), lambda j: (0, j)),
        out_shape=jax.ShapeDtypeStruct((bsz, n), F32),
        compiler_params=_cparams(("arbitrary",)),
        name="ada_mod",
    )(c, w, b.reshape(1, n))


def _norm_mod_kernel(x_ref, g_ref, sc_ref, sh_ref, o_ref):
    x = x_ref[0]
    ms = jnp.mean(x * x, axis=-1, keepdims=True)
    y = x * lax.rsqrt(ms + NORM_EPS) * g_ref[...]
    o_ref[0] = (y * (1.0 + sc_ref[0, 0]) + sh_ref[0, 0]).astype(o_ref.dtype)


def _norm_mod(x, gain, mod, sc_idx, sh_idx):
    bsz, s, d = x.shape
    tm = _tile(s, 512)
    return pl.pallas_call(
        _norm_mod_kernel,
        grid=(bsz, s // tm),
        in_specs=[pl.BlockSpec((1, tm, d), lambda b, i: (b, i, 0)),
                  pl.BlockSpec((1, d), lambda b, i: (0, 0)),
                  pl.BlockSpec((1, 1, 1, d), lambda b, i: (sc_idx, b, 0, 0)),
                  pl.BlockSpec((1, 1, 1, d), lambda b, i: (sh_idx, b, 0, 0))],
        out_specs=pl.BlockSpec((1, tm, d), lambda b, i: (b, i, 0)),
        out_shape=jax.ShapeDtypeStruct((bsz, s, d), BF16),
        compiler_params=_cparams(("arbitrary", "arbitrary")),
        name="norm_mod",
    )(x, gain.reshape(1, d), mod, mod)


def _mm_kernel(a_ref, b_ref, *rest, sigmoid, scaled_tiles, col_scale, n_cast):
    cast_in, o_ref, cast_out = rest[:n_cast], rest[n_cast], rest[n_cast + 1:]
    acc = jnp.dot(a_ref[...], b_ref[...], preferred_element_type=F32)
    if sigmoid:
        acc = jax.nn.sigmoid(acc)
    if scaled_tiles:
        acc = acc * jnp.where(pl.program_id(1) < scaled_tiles, col_scale, 1.0)
    o_ref[...] = acc.astype(o_ref.dtype)
    _cast_slabs(cast_in, cast_out)


def _matmul(a, b, out_dtype, *, name, sigmoid=False, scaled_cols=0, col_scale=1.0, cast=(),
            tm_pref=1024, tn_pref=512):
    m, k = a.shape
    n = b.shape[1]
    tm, tn = _tile(m, tm_pref), _tile(n, tn_pref)
    assert scaled_cols % tn == 0
    nj = n // tn
    slabs = [_slabs(w, (m // tm) * nj) for w in cast]
    slab_specs = _slab_specs(slabs, lambda i, j: i * nj + j)
    out = pl.pallas_call(
        functools.partial(_mm_kernel, sigmoid=sigmoid, scaled_tiles=scaled_cols // tn, col_scale=col_scale,
                          n_cast=len(cast)),
        grid=(m // tm, nj),
        in_specs=[pl.BlockSpec((tm, k), lambda i, j: (i, 0)),
                  pl.BlockSpec((k, tn), lambda i, j: (0, j))] + slab_specs,
        out_specs=[pl.BlockSpec((tm, tn), lambda i, j: (i, j))] + slab_specs,
        out_shape=[jax.ShapeDtypeStruct((m, n), out_dtype)] + [jax.ShapeDtypeStruct(w.shape, BF16) for w in slabs],
        compiler_params=_cparams(("arbitrary", "arbitrary")),
        name=name,
    )(a, b, *slabs)
    return [out[0]] + [o.reshape(w.shape) for o, w in zip(out[1:], cast)]


def _latent_kernel(a_ref, b_ref, cq_ref, ckv_ref, kpe_ref, *, q_rank, kv_rank):
    acc = jnp.dot(a_ref[...], b_ref[...], preferred_element_type=F32)
    cq_ref[...] = acc[:, :q_rank].astype(cq_ref.dtype)
    ckv_ref[...] = acc[:, q_rank:q_rank + kv_rank].astype(ckv_ref.dtype)
    kpe_ref[...] = acc[:, q_rank + kv_rank:]


def _latent_proj(h, w_lat, q_rank, kv_rank):
    m, k = h.shape
    n = w_lat.shape[1]
    tm = _tile(m, 512)
    return pl.pallas_call(
        functools.partial(_latent_kernel, q_rank=q_rank, kv_rank=kv_rank),
        grid=(m // tm,),
        in_specs=[pl.BlockSpec((tm, k), lambda i: (i, 0)),
                  pl.BlockSpec((k, n), lambda i: (0, 0))],
        out_specs=[pl.BlockSpec((tm, q_rank), lambda i: (i, 0)),
                   pl.BlockSpec((tm, kv_rank), lambda i: (i, 0)),
                   pl.BlockSpec((tm, LANES), lambda i: (i, 0))],
        out_shape=[jax.ShapeDtypeStruct((m, q_rank), BF16),
                   jax.ShapeDtypeStruct((m, kv_rank), BF16),
                   jax.ShapeDtypeStruct((m, LANES), F32)],
        compiler_params=_cparams(("arbitrary",)),
        name="latent_proj",
    )(h, w_lat)


def _rope_table_kernel(pos_ref, inv_ref, c_ref, s1_ref, s2_ref):
    half = MLA_ROPE_DIM // 2
    ang = pos_ref[0].astype(F32) * inv_ref[...]
    cos, sin = jnp.cos(ang), jnp.sin(ang)
    lane = lax.broadcasted_iota(I32, ang.shape, 1)
    c_ref[0] = jnp.where(lane < 2 * half, cos, 0.0)
    s1_ref[0] = jnp.where(lane < half, -sin, 0.0)
    s2_ref[0] = jnp.where((lane >= half) & (lane < 2 * half), sin, 0.0)


def _rope_tables(positions):
    bsz, s = positions.shape
    half = MLA_ROPE_DIM // 2
    inv = ROPE_THETA ** (-(jnp.arange(LANES, dtype=F32) % half) / half)
    ts = _tile(s, 512)
    spec = pl.BlockSpec((1, ts, LANES), lambda b, i: (b, i, 0))
    shp = jax.ShapeDtypeStruct((bsz, s, LANES), F32)
    return pl.pallas_call(
        _rope_table_kernel,
        grid=(bsz, s // ts),
        in_specs=[pl.BlockSpec((1, ts, 1), lambda b, i: (b, i, 0)),
                  pl.BlockSpec((1, LANES), lambda b, i: (0, 0))],
        out_specs=[spec, spec, spec],
        out_shape=[shp, shp, shp],
        compiler_params=_cparams(("arbitrary", "arbitrary")),
        name="rope_tables",
    )(positions.reshape(bsz, s, 1), inv.reshape(1, LANES))


def _rotate(r, c, s1, s2):
    half = MLA_ROPE_DIM // 2
    return r * c + pltpu.roll(r, LANES - half, 1) * s1 + pltpu.roll(r, half, 1) * s2


def _rms(x, gain):
    ms = jnp.mean(x * x, axis=-1, keepdims=True)
    return x * lax.rsqrt(ms + NORM_EPS) * gain


def _mla_q_kernel(cq_ref, g_ref, w_ref, c_ref, s1_ref, s2_ref, o_ref, *, heads, scale):
    y = _rms(cq_ref[0].astype(F32), g_ref[...]).astype(BF16)
    q = jnp.dot(y, w_ref[...], preferred_element_type=F32)
    c, s1, s2 = c_ref[0], s1_ref[0], s2_ref[0]
    for h in range(heads):
        base = h * MLA_QK_PAD
        o_ref[0, :, base:base + LANES] = (q[:, base:base + LANES] * scale).astype(BF16)
        r = q[:, base + LANES:base + 2 * LANES]
        o_ref[0, :, base + LANES:base + 2 * LANES] = (_rotate(r, c, s1, s2) * scale).astype(BF16)


def _mla_q(cq, gain, w_q, tabs, scale):
    bsz, s, qr = cq.shape
    heads = MLA_HEADS
    n = heads * MLA_QK_PAD
    tm = _tile(s, 512)
    tab_spec = pl.BlockSpec((1, tm, LANES), lambda b, i: (b, i, 0))
    return pl.pallas_call(
        functools.partial(_mla_q_kernel, heads=heads, scale=scale),
        grid=(bsz, s // tm),
        in_specs=[pl.BlockSpec((1, tm, qr), lambda b, i: (b, i, 0)),
                  pl.BlockSpec((1, qr), lambda b, i: (0, 0)),
                  pl.BlockSpec((qr, n), lambda b, i: (0, 0)),
                  tab_spec, tab_spec, tab_spec],
        out_specs=pl.BlockSpec((1, tm, n), lambda b, i: (b, i, 0)),
        out_shape=jax.ShapeDtypeStruct((bsz, s, n), BF16),
        compiler_params=_cparams(("arbitrary", "arbitrary")),
        name="mla_q_prep",
    )(cq, gain.reshape(1, qr), w_q, *tabs)


def _mla_kv_kernel(ckv_ref, g_ref, wk_ref, wv_ref, kpe_ref, c_ref, s1_ref, s2_ref, k_ref, v_ref, *, heads):
    y = _rms(ckv_ref[0].astype(F32), g_ref[...]).astype(BF16)
    kn = jnp.dot(y, wk_ref[...], preferred_element_type=F32)
    v_ref[0] = jnp.dot(y, wv_ref[...], preferred_element_type=F32).astype(BF16)
    kr = _rotate(kpe_ref[0], c_ref[0], s1_ref[0], s2_ref[0]).astype(BF16)
    for h in range(heads):
        base = h * MLA_QK_PAD
        k_ref[0, :, base:base + LANES] = kn[:, h * LANES:(h + 1) * LANES].astype(BF16)
        k_ref[0, :, base + LANES:base + 2 * LANES] = kr


def _mla_kv(ckv, gain, w_k, w_v, kpe, tabs):
    bsz, s, kvr = ckv.shape
    heads = MLA_HEADS
    tm = _tile(s, 512)
    tab_spec = pl.BlockSpec((1, tm, LANES), lambda b, i: (b, i, 0))
    return pl.pallas_call(
        functools.partial(_mla_kv_kernel, heads=heads),
        grid=(bsz, s // tm),
        in_specs=[pl.BlockSpec((1, tm, kvr), lambda b, i: (b, i, 0)),
                  pl.BlockSpec((1, kvr), lambda b, i: (0, 0)),
                  pl.BlockSpec((kvr, heads * MLA_NOPE_DIM), lambda b, i: (0, 0)),
                  pl.BlockSpec((kvr, heads * MLA_V_DIM), lambda b, i: (0, 0)),
                  tab_spec, tab_spec, tab_spec, tab_spec],
        out_specs=[pl.BlockSpec((1, tm, heads * MLA_QK_PAD), lambda b, i: (b, i, 0)),
                   pl.BlockSpec((1, tm, heads * MLA_V_DIM), lambda b, i: (b, i, 0))],
        out_shape=[jax.ShapeDtypeStruct((bsz, s, heads * MLA_QK_PAD), BF16),
                   jax.ShapeDtypeStruct((bsz, s, heads * MLA_V_DIM), BF16)],
        compiler_params=_cparams(("arbitrary", "arbitrary")),
        name="mla_kv_prep",
    )(ckv, gain.reshape(1, kvr), w_k, w_v, kpe, *tabs)


def _qk(q, k):
    return lax.dot_general(q, k, (((1,), (1,)), ((), ())), preferred_element_type=F32)


def _causal_mask(s, qi, ki, tq, tk):
    row = qi * tq + lax.broadcasted_iota(I32, s.shape, 0)
    col = ki * tk + lax.broadcasted_iota(I32, s.shape, 1)
    return jnp.where(row >= col, s, -jnp.inf)


def _lane_chunks(x):
    return [x[:, c * LANES:(c + 1) * LANES] for c in range(x.shape[1] // LANES)]


def _softmax_update(s, v, m_prev, l_prev, acc_prev):
    chunks = _lane_chunks(s)
    cmax = functools.reduce(jnp.maximum, chunks)
    m_new = jnp.maximum(m_prev, jnp.max(cmax, axis=-1, keepdims=True))
    alpha = jnp.exp2(m_prev - m_new)
    ps = [jnp.exp2(c - m_new) for c in chunks]
    psum = functools.reduce(lambda a, b: a + b, ps)
    l_new = alpha * l_prev + jnp.sum(psum, axis=-1, keepdims=True)
    p = jnp.concatenate([c.astype(BF16) for c in ps], axis=1)
    pv = jnp.dot(p, v, preferred_element_type=F32)
    acc_new = jnp.concatenate([alpha * a for a in _lane_chunks(acc_prev)], axis=1) + pv
    return m_new, l_new, acc_new


def _emit_pipelined(score_fns, update_fns):
    n = len(score_fns)
    pending = [score_fns[i]() for i in range(min(ATTN_LOOKAHEAD, n))]
    for i in range(n):
        if i + ATTN_LOOKAHEAD < n:
            pending.append(score_fns[i + ATTN_LOOKAHEAD]())
        update_fns[i](pending[i])
        pending[i] = None


def _diag_mask(s, r):
    rows = s.shape[0]
    off = s.shape[1] - rows
    row = lax.broadcasted_iota(I32, s.shape, 0)
    col = lax.broadcasted_iota(I32, s.shape, 1)
    return jnp.where(col - off <= row, s, -jnp.inf)


def _slabs(w, steps):
    cols = w.shape[-1]
    rows = w.size // cols
    assert rows % (steps * 2 * SUBLANES) == 0, (w.shape, steps)
    return w.reshape(steps, rows // steps, cols)


def _slab_specs(slabs, step_of):
    return [pl.BlockSpec((1,) + w.shape[1:], lambda *g: (step_of(*g), 0, 0)) for w in slabs]


def _cast_slabs(src_refs, dst_refs):
    for src, dst in zip(src_refs, dst_refs):
        dst[...] = src[...].astype(BF16)


def _mla_attn_kernel(q_ref, k_ref, v_ref, *rest, tq, tk, rb, n_cast):
    cast_in, o_ref, cast_out, scr = (rest[:n_cast], rest[n_cast], rest[n_cast + 1:2 * n_cast + 1],
                                     rest[2 * n_cast + 1:])
    qi = pl.program_id(2)
    nr = tq // rb
    state = [scr[3 * r:3 * r + 3] for r in range(nr)]
    for m_scr, l_scr, acc_scr in state:
        m_scr[...] = jnp.full(m_scr.shape, -jnp.inf, F32)
        l_scr[...] = jnp.zeros(l_scr.shape, F32)
        acc_scr[...] = jnp.zeros(acc_scr.shape, F32)

    def rows(r):
        return pl.ds(r * rb, rb)

    def update(r, s, v):
        m_scr, l_scr, acc_scr = state[r]
        m, l, acc = _softmax_update(s, v, m_scr[...], l_scr[...], acc_scr[...])
        m_scr[...] = m
        l_scr[...] = l
        acc_scr[...] = acc

    def full_tile(ki, carry):
        start = pl.multiple_of(ki * tk, tk)
        k = k_ref[0, pl.ds(start, tk), :]
        v = v_ref[0, pl.ds(start, tk), :]
        _emit_pipelined([functools.partial(_qk, q_ref[0, rows(r), :], k) for r in range(nr)],
                        [functools.partial(update, r, v=v) for r in range(nr)])
        return carry
    lax.fori_loop(0, qi * (tq // tk), full_tile, 0)

    _cast_slabs(cast_in, cast_out)
    dstart = pl.multiple_of(qi * tq, tq)
    cols = [pl.ds(dstart, (r + 1) * rb) for r in range(nr)]
    _emit_pipelined(
        [lambda r=r: _diag_mask(_qk(q_ref[0, rows(r), :], k_ref[0, cols[r], :]), r) for r in range(nr)],
        [lambda s, r=r: update(r, s, v_ref[0, cols[r], :]) for r in range(nr)])

    for r, (_, l_scr, acc_scr) in enumerate(state):
        o_ref[0, rows(r), :] = (acc_scr[...] / l_scr[...]).astype(o_ref.dtype)


def _attn_state_scratch(nr, rb, dv):
    return [pltpu.VMEM((rb, w), F32) for _ in range(nr) for w in (LANES, LANES, dv)]


def _mla_attention(q, k, v, cast=()):
    bsz, s, _ = q.shape
    heads = MLA_HEADS
    t = _tile(s, 1024)
    rb = _tile(t, 256)
    nq = s // t
    slabs = [_slabs(w, bsz * heads * nq) for w in cast]
    slab_specs = _slab_specs(slabs, lambda b, h, i: (b * heads + h) * nq + i)
    out = pl.pallas_call(
        functools.partial(_mla_attn_kernel, tq=t, tk=t, rb=rb, n_cast=len(cast)),
        grid=(bsz, heads, nq),
        in_specs=[pl.BlockSpec((1, t, MLA_QK_PAD), lambda b, h, i: (b, i, h)),
                  pl.BlockSpec((1, s, MLA_QK_PAD), lambda b, h, i: (b, 0, h)),
                  pl.BlockSpec((1, s, MLA_V_DIM), lambda b, h, i: (b, 0, h))] + slab_specs,
        out_specs=[pl.BlockSpec((1, t, MLA_V_DIM), lambda b, h, i: (b, i, h))] + slab_specs,
        out_shape=[jax.ShapeDtypeStruct((bsz, s, heads * MLA_V_DIM), BF16)]
                  + [jax.ShapeDtypeStruct(w.shape, BF16) for w in slabs],
        scratch_shapes=_attn_state_scratch(t // rb, rb, MLA_V_DIM),
        compiler_params=_cparams(("arbitrary", "arbitrary", "arbitrary")),
        name="mla_attn",
    )(q, k, v, *slabs)
    return [out[0]] + [o.reshape(w.shape) for o, w in zip(out[1:], cast)]


def _diff_attn_kernel(q_ref, k_ref, v_ref, qpos_ref, kpos_ref, slope_ref, lam_ref, subln_ref, *rest,
                      tq, tk, rb, lam_init, n_cast):
    cast_in, o_ref, cast_out, scr = (rest[:n_cast], rest[n_cast], rest[n_cast + 1:2 * n_cast + 1],
                                     rest[2 * n_cast + 1:])
    qi = pl.program_id(2)
    d = DIFF_HEAD_DIM
    nr = tq // rb
    state = [[scr[6 * r + 3 * g:6 * r + 3 * g + 3] for g in range(2)] for r in range(nr)]
    for r in range(nr):
        for m_scr, l_scr, a_scr in state[r]:
            m_scr[...] = jnp.full(m_scr.shape, -jnp.inf, F32)
            l_scr[...] = jnp.zeros(l_scr.shape, F32)
            a_scr[...] = jnp.zeros(a_scr.shape, F32)
    slope = slope_ref[0, :, 0:1]

    def rows(r):
        return pl.ds(r * rb, rb)

    def kpos(first, count):
        return jnp.concatenate([kpos_ref[0, first + j] for j in range(count)], axis=1)

    chains = [(r, g) for r in range(nr) for g in range(2)]

    def emit(k_of, v_of, kp_of, masked):
        bias = {}

        def score(r, g):
            if g == 0:
                bias[r] = slope * jnp.abs(qpos_ref[0, rows(r), :] - kp_of(r))
            s = _qk(q_ref[0, rows(r), g * d:(g + 1) * d], k_of(r)[:, g * d:(g + 1) * d]) - bias[r]
            return _diag_mask(s, r) if masked else s

        def update(r, g, s):
            m_scr, l_scr, a_scr = state[r][g]
            m, l, acc = _softmax_update(s, v_of(r), m_scr[...], l_scr[...], a_scr[...])
            m_scr[...] = m
            l_scr[...] = l
            a_scr[...] = acc

        _emit_pipelined([functools.partial(score, r, g) for r, g in chains],
                        [functools.partial(update, r, g) for r, g in chains])

    def full_tile(ki, carry):
        start = pl.multiple_of(ki * tk, tk)
        k = k_ref[0, pl.ds(start, tk), :]
        v = v_ref[0, pl.ds(start, tk), :]
        kp = kpos(ki * (tk // rb), tk // rb)
        emit(lambda r: k, lambda r: v, lambda r: kp, False)
        return carry
    lax.fori_loop(0, qi * (tq // tk), full_tile, 0)

    _cast_slabs(cast_in, cast_out)
    dstart = pl.multiple_of(qi * tq, tq)
    cols = [pl.ds(dstart, (r + 1) * rb) for r in range(nr)]
    emit(lambda r: k_ref[0, cols[r], :], lambda r: v_ref[0, cols[r], :], lambda r: kpos(qi * nr, r + 1), True)

    lp = lam_ref[...]
    e1 = jnp.exp(jnp.sum(lp[0:1] * lp[1:2], axis=-1, keepdims=True))
    e2 = jnp.exp(jnp.sum(lp[2:3] * lp[3:4], axis=-1, keepdims=True))
    lam = e1 - e2 + lam_init
    for r in range(nr):
        (_, l1, a1), (_, l2, a2) = state[r]
        o1 = [a / l1[...] for a in _lane_chunks(a1[...])]
        o2 = [a / l2[...] for a in _lane_chunks(a2[...])]
        o = jnp.concatenate([x - lam * y for x, y in zip(o1, o2)], axis=1)
        o_ref[0, rows(r), :] = (_rms(o, subln_ref[...]) * (1.0 - lam_init)).astype(o_ref.dtype)


def _diff_attention(qkv, positions, diff_lambda, subln, lam_init, cast=()):
    bsz, s, _ = qkv.shape
    heads = DIFF_HEADS
    dv = 2 * DIFF_HEAD_DIM
    t = _tile(s, 1024)
    tk = _tile(t, 512)
    rb = _tile(tk, 256)
    nq = s // t
    slopes = 2.0 ** (-8.0 * jnp.arange(1, heads + 1, dtype=F32) / heads) * LOG2_E
    slopes = jnp.broadcast_to(slopes[:, None, None], (heads, 1, LANES))
    posf = positions.astype(F32)
    slabs = [_slabs(w, bsz * heads * nq) for w in cast]
    slab_specs = _slab_specs(slabs, lambda b, h, i: (b * heads + h) * nq + i)
    out = pl.pallas_call(
        functools.partial(_diff_attn_kernel, tq=t, tk=tk, rb=rb, lam_init=lam_init, n_cast=len(cast)),
        grid=(bsz, heads, nq),
        in_specs=[pl.BlockSpec((1, t, dv), lambda b, h, i: (b, i, h)),
                  pl.BlockSpec((1, s, dv), lambda b, h, i: (b, 0, heads + h)),
                  pl.BlockSpec((1, s, dv), lambda b, h, i: (b, 0, 2 * heads + h)),
                  pl.BlockSpec((1, t, 1), lambda b, h, i: (b, i, 0)),
                  pl.BlockSpec((1, s // rb, 1, rb), lambda b, h, i: (b, 0, 0, 0)),
                  pl.BlockSpec((1, 1, LANES), lambda b, h, i: (h, 0, 0)),
                  pl.BlockSpec((4, DIFF_HEAD_DIM), lambda b, h, i: (0, 0)),
                  pl.BlockSpec((1, dv), lambda b, h, i: (0, 0))] + slab_specs,
        out_specs=[pl.BlockSpec((1, t, dv), lambda b, h, i: (b, i, h))] + slab_specs,
        out_shape=[jax.ShapeDtypeStruct((bsz, s, heads * dv), BF16)]
                  + [jax.ShapeDtypeStruct(w.shape, BF16) for w in slabs],
        scratch_shapes=_attn_state_scratch(2 * (t // rb), rb, dv),
        compiler_params=_cparams(("arbitrary", "arbitrary", "arbitrary")),
        name="diff_attn",
    )(qkv, qkv, qkv, posf.reshape(bsz, s, 1), posf.reshape(bsz, s // rb, 1, rb), slopes, diff_lambda,
      subln.reshape(1, dv), *slabs)
    return [out[0]] + [o.reshape(w.shape) for o, w in zip(out[1:], cast)]


def _merge_kernel(od_ref, om_ref, wd_ref, wm_ref, g0_ref, g1_ref, x_ref, gt_ref, o_ref):
    yd = jnp.dot(od_ref[0], wd_ref[...], preferred_element_type=F32)
    ym = jnp.dot(om_ref[0], wm_ref[...], preferred_element_type=F32)
    y = g0_ref[0].astype(F32) * yd + g1_ref[0].astype(F32) * ym
    o_ref[0] = x_ref[0] + gt_ref[0, 0] * y


def _merge(o_d, o_m, w_d, w_m, gates, x, mod, gt_idx):
    bsz, s, d = x.shape
    kd, km = o_d.shape[-1], o_m.shape[-1]
    tm, tn = _tile(s, 512), _tile(d, 512)
    nj = d // tn
    return pl.pallas_call(
        _merge_kernel,
        grid=(bsz, s // tm, nj),
        in_specs=[pl.BlockSpec((1, tm, kd), lambda b, i, j: (b, i, 0)),
                  pl.BlockSpec((1, tm, km), lambda b, i, j: (b, i, 0)),
                  pl.BlockSpec((kd, tn), lambda b, i, j: (0, j)),
                  pl.BlockSpec((km, tn), lambda b, i, j: (0, j)),
                  pl.BlockSpec((1, tm, tn), lambda b, i, j: (b, i, j)),
                  pl.BlockSpec((1, tm, tn), lambda b, i, j: (b, i, j + nj)),
                  pl.BlockSpec((1, tm, tn), lambda b, i, j: (b, i, j)),
                  pl.BlockSpec((1, 1, 1, tn), lambda b, i, j: (gt_idx, b, 0, j))],
        out_specs=pl.BlockSpec((1, tm, tn), lambda b, i, j: (b, i, j)),
        out_shape=jax.ShapeDtypeStruct((bsz, s, d), F32),
        compiler_params=_cparams(("arbitrary", "arbitrary", "arbitrary")),
        name="out_merge",
    )(o_d, o_m, w_d, w_m, gates, gates, x, mod)


def _pack_rows(y):
    half = y.shape[1] // 2
    bits = lax.bitcast_convert_type(y.astype(BF16).astype(F32), U32)
    return (bits[:, half:] & jnp.uint32(0xFFFF0000)) | (bits[:, :half] >> 16)


def _unpack_rows(w):
    lo = lax.bitcast_convert_type(w << 16, F32)
    hi = lax.bitcast_convert_type(w & jnp.uint32(0xFFFF0000), F32)
    return lo, hi


def _ffn_norm_kernel(x_ref, g_ref, sc_ref, sh_ref, rw_ref, h_ref, hp_ref, lt_ref, *, nch):
    x = x_ref[0]
    y = _rms(x, g_ref[...]) * (1.0 + sc_ref[0, 0]) + sh_ref[0, 0]
    hb = y.astype(BF16)
    h_ref[0] = hb
    lt_ref[...] = lax.dot_general(rw_ref[...], hb, (((1,), (1,)), ((), ())), preferred_element_type=F32)
    packed = _pack_rows(y)
    for c in range(nch):
        hp_ref[pl.ds(c, x.shape[0], stride=nch), :] = packed[:, c * LANES:(c + 1) * LANES]


def _ffn_norm(x, gain, mod, sc_idx, sh_idx, router_wt):
    bsz, s, d = x.shape
    e = router_wt.shape[0]
    nch = d // (2 * LANES)
    tm = _tile(s, 256)
    ns = s // tm
    return pl.pallas_call(
        functools.partial(_ffn_norm_kernel, nch=nch),
        grid=(bsz, ns),
        in_specs=[pl.BlockSpec((1, tm, d), lambda b, i: (b, i, 0)),
                  pl.BlockSpec((1, d), lambda b, i: (0, 0)),
                  pl.BlockSpec((1, 1, 1, d), lambda b, i: (sc_idx, b, 0, 0)),
                  pl.BlockSpec((1, 1, 1, d), lambda b, i: (sh_idx, b, 0, 0)),
                  pl.BlockSpec((e, d), lambda b, i: (0, 0))],
        out_specs=[pl.BlockSpec((1, tm, d), lambda b, i: (b, i, 0)),
                   pl.BlockSpec((tm * nch, LANES), lambda b, i: (b * ns + i, 0)),
                   pl.BlockSpec((e, tm), lambda b, i: (0, b * ns + i))],
        out_shape=[jax.ShapeDtypeStruct((bsz, s, d), BF16),
                   jax.ShapeDtypeStruct((bsz * s * nch, LANES), U32),
                   jax.ShapeDtypeStruct((e, bsz * s), F32)],
        compiler_params=_cparams(("arbitrary", "arbitrary")),
        name="ffn_norm_router",
    )(x, gain.reshape(1, d), mod, mod, router_wt)


def _first_index(hit, iota, axis, size):
    return jnp.min(jnp.where(hit, iota, size), axis=axis, keepdims=True)


def _route_kernel(lt_ref, bias_ref, idx_ref, wt_ref, rank_ref, sizes_ref, cnt_scr, *, n_exp, tn):
    i = pl.program_id(0)
    gsz = n_exp // N_GROUPS

    @pl.when(i == 0)
    def _():
        cnt_scr[...] = jnp.zeros(cnt_scr.shape, F32)

    scores = jax.nn.sigmoid(lt_ref[...])
    sel = scores + bias_ref[...]
    sel3 = sel.reshape(N_GROUPS, gsz, tn)
    j3 = lax.broadcasted_iota(I32, sel3.shape, 1)
    top1 = jnp.max(sel3, axis=1, keepdims=True)
    first = _first_index(sel3 == top1, j3, 1, gsz)
    top2 = jnp.max(jnp.where(j3 == first, -jnp.inf, sel3), axis=1, keepdims=True)
    gscore = (top1 + top2).reshape(N_GROUPS, tn)

    giota = lax.broadcasted_iota(I32, gscore.shape, 0)
    gmask = jnp.zeros(gscore.shape, jnp.bool_)
    for _ in range(TOPK_GROUPS):
        best = jnp.max(gscore, axis=0, keepdims=True)
        gi = _first_index(gscore == best, giota, 0, N_GROUPS)
        hit = giota == gi
        gmask = gmask | hit
        gscore = jnp.where(hit, -jnp.inf, gscore)

    emask = jnp.broadcast_to(gmask.reshape(N_GROUPS, 1, tn), sel3.shape)
    cand = jnp.where(emask, sel3, -jnp.inf).reshape(n_exp, tn)
    eiota = lax.broadcasted_iota(I32, cand.shape, 0)
    hits, idxs, vals = [], [], []
    for _ in range(TOP_K):
        best = jnp.max(cand, axis=0, keepdims=True)
        ei = _first_index(cand == best, eiota, 0, n_exp)
        hit = eiota == ei
        hits.append(hit)
        idxs.append(ei)
        vals.append(jnp.sum(jnp.where(hit, scores, 0.0), axis=0, keepdims=True))
        cand = jnp.where(hit, -jnp.inf, cand)
    total = functools.reduce(lambda a, b: a + b, vals)

    chosen = functools.reduce(lambda a, b: a | b, hits)
    onehot = jnp.where(chosen, 1.0, 0.0)
    r = lax.broadcasted_iota(I32, (tn, tn), 0)
    c = lax.broadcasted_iota(I32, (tn, tn), 1)
    upper = jnp.where(r < c, 1.0, 0.0).astype(BF16)
    before = cnt_scr[...] + jnp.dot(onehot.astype(BF16), upper, preferred_element_type=F32)
    cnt_new = cnt_scr[...] + jnp.sum(onehot, axis=1, keepdims=True)
    cnt_scr[...] = cnt_new

    pad = SUBLANES - TOP_K
    ranks = [jnp.sum(jnp.where(h, before, 0.0), axis=0, keepdims=True).astype(I32) for h in hits]
    zi = [jnp.zeros((pad, tn), I32)]
    idx_ref[...] = jnp.concatenate(idxs + zi, axis=0)
    rank_ref[...] = jnp.concatenate(ranks + zi, axis=0)
    wt_ref[...] = jnp.concatenate([v / total * ROUTED_SCALE for v in vals] + [jnp.zeros((pad, tn), F32)], axis=0)
    sizes_ref[...] = jnp.broadcast_to(cnt_new, sizes_ref.shape).astype(I32)


def _route(logits_t, bias):
    n_exp, n = logits_t.shape
    tn = _tile(n, 512)
    row_spec = pl.BlockSpec((SUBLANES, tn), lambda i: (0, i))
    return pl.pallas_call(
        functools.partial(_route_kernel, n_exp=n_exp, tn=tn),
        grid=(n // tn,),
        in_specs=[pl.BlockSpec((n_exp, tn), lambda i: (0, i)),
                  pl.BlockSpec((n_exp, 1), lambda i: (0, 0))],
        out_specs=[row_spec, row_spec, row_spec, pl.BlockSpec((n_exp, LANES), lambda i: (0, 0))],
        out_shape=[jax.ShapeDtypeStruct((SUBLANES, n), I32),
                   jax.ShapeDtypeStruct((SUBLANES, n), F32),
                   jax.ShapeDtypeStruct((SUBLANES, n), I32),
                   jax.ShapeDtypeStruct((n_exp, LANES), I32)],
        scratch_shapes=[pltpu.VMEM((n_exp, 1), F32)],
        compiler_params=_cparams(("arbitrary",)),
        name="route_topk",
    )(logits_t, bias.reshape(n_exp, 1))


def _dispatch_kernel(dest_ref, seg_ref, hp_ref, xs_ref, zero_scr, sem, zsem, *, n_tok, tc, nch, n_exp, blk, nb):
    i = pl.program_id(0)

    def row_copy(j, slot):
        return pltpu.make_async_copy(hp_ref.at[pl.ds(pl.multiple_of(j * nch, nch), nch)],
                                     xs_ref.at[pl.ds(pl.multiple_of(slot * nch, nch), nch)], sem)

    def zero_row(slot):
        return pltpu.make_async_copy(zero_scr.at[pl.ds(0, nch)],
                                     xs_ref.at[pl.ds(pl.multiple_of(slot * nch, nch), nch)], zsem)

    def zero_block(b):
        return pltpu.make_async_copy(zero_scr,
                                     xs_ref.at[pl.ds(pl.multiple_of(b * (blk * nch), blk * nch), blk * nch)], zsem)

    def start_then_wait(lo, hi, copy):
        def start(v, c):
            copy(v).start()
            return c
        lax.fori_loop(lo, hi, start, 0)

        def wait(v, c):
            copy(v).wait()
            return c
        lax.fori_loop(lo, hi, wait, 0)

    @pl.when(i == 0)
    def _():
        zero_scr[...] = jnp.zeros(zero_scr.shape, U32)

        def per_expert(e, carry):
            start_then_wait(seg_ref[e], seg_ref[n_exp + e], zero_row)
            return carry
        lax.fori_loop(0, n_exp, per_expert, 0)
        start_then_wait(seg_ref[2 * n_exp], nb, zero_block)

    base = i * tc

    def start(j, c):
        for k in range(TOP_K):
            row_copy(j, dest_ref[k * n_tok + base + j]).start(priority=k % DMA_QUEUES)
        return c
    lax.fori_loop(0, tc, start, 0, unroll=DMA_ISSUE_UNROLL)

    def wait(j, c):
        for k in range(TOP_K):
            row_copy(0, 0).wait()
        return c
    lax.fori_loop(0, tc, wait, 0)


def _dispatch(dest, seg, hp, nb, blk, nch, n_exp):
    n_tok = dest.shape[0] // TOP_K
    tc = _tile(n_tok, 256)
    return pl.pallas_call(
        functools.partial(_dispatch_kernel, n_tok=n_tok, tc=tc, nch=nch, n_exp=n_exp, blk=blk, nb=nb),
        grid_spec=pltpu.PrefetchScalarGridSpec(
            num_scalar_prefetch=2,
            grid=(n_tok // tc,),
            in_specs=[pl.BlockSpec((tc * nch, LANES), lambda i, dst, sg: (i, 0))],
            out_specs=pl.BlockSpec(memory_space=pl.ANY),
            scratch_shapes=[pltpu.VMEM((blk * nch, LANES), U32), pltpu.SemaphoreType.DMA,
                            pltpu.SemaphoreType.DMA]),
        out_shape=jax.ShapeDtypeStruct((nb * blk * nch, LANES), U32),
        compiler_params=_cparams(("arbitrary",)),
        name="moe_dispatch",
    )(dest, seg, hp)


def _load_rows(ref, tm, nch):
    los, his = [], []
    for c in range(nch):
        lo, hi = _unpack_rows(ref[pl.ds(c, tm, stride=nch), :])
        los.append(lo.astype(BF16))
        his.append(hi.astype(BF16))
    return jnp.concatenate(los + his, axis=1)


def _expert_kernel(be_ref, nact_ref, xs_ref, w1_ref, w3_ref, w2_ref, ys_ref, *, tm, nch):
    active = pl.program_id(0) < nact_ref[0]

    @pl.when(jnp.logical_not(active))
    def _():
        ys_ref[...] = jnp.zeros(ys_ref.shape, U32)

    @pl.when(active)
    def _():
        x = _load_rows(xs_ref, tm, nch)
        a = jnp.dot(x, w1_ref[0], preferred_element_type=F32)
        b = jnp.dot(x, w3_ref[0], preferred_element_type=F32)
        y = jnp.dot((_silu(a) * b).astype(BF16), w2_ref[0], preferred_element_type=F32)
        packed = _pack_rows(y)
        for c in range(nch):
            ys_ref[pl.ds(c, tm, stride=nch), :] = packed[:, c * LANES:(c + 1) * LANES]


def _experts(block_expert, nact, xs, w1, w3, w2, nch):
    n_exp, d, f = w1.shape
    tm = EXPERT_ROWS
    nb = xs.shape[0] // (tm * nch)
    blk = lambda i, be, na: (jnp.minimum(i, na[0] - 1), 0)
    wsel = lambda i, be, na: (be[jnp.minimum(i, na[0] - 1)], 0, 0)
    return pl.pallas_call(
        functools.partial(_expert_kernel, tm=tm, nch=nch),
        grid_spec=pltpu.PrefetchScalarGridSpec(
            num_scalar_prefetch=2,
            grid=(nb,),
            in_specs=[pl.BlockSpec((tm * nch, LANES), blk),
                      pl.BlockSpec((1, d, f), wsel),
                      pl.BlockSpec((1, d, f), wsel),
                      pl.BlockSpec((1, f, d), wsel)],
            out_specs=pl.BlockSpec((tm * nch, LANES), lambda i, be, na: (i, 0))),
        out_shape=jax.ShapeDtypeStruct(xs.shape, U32),
        compiler_params=_cparams(("arbitrary",)),
        name="moe_experts",
    )(block_expert, nact, xs, w1, w3, w2)


def _combine_kernel(dest_ref, ys_ref, wt_ref, h_ref, w1_ref, w3_ref, w2_ref, x_ref, gt_ref, fn_ref, o_ref,
                    ybuf0, ybuf1, sem0, sem1, *, n_tok, tm, nch, nsteps, final):
    step = pl.program_id(0) * pl.num_programs(1) + pl.program_id(1)
    bufs = ((ybuf0, sem0), (ybuf1, sem1))

    def row_copy(buf, sem, slot, k, j):
        return pltpu.make_async_copy(ys_ref.at[pl.ds(pl.multiple_of(slot * nch, nch), nch)],
                                     buf.at[k, pl.ds(pl.multiple_of(j * nch, nch), nch)], sem)

    def start_gathers(tile, buf, sem):
        base = tile * tm

        def start(j, c):
            for k in range(TOP_K):
                row_copy(buf, sem, dest_ref[k * n_tok + base + j], k, j).start(priority=k % DMA_QUEUES)
            return c
        lax.fori_loop(0, tm, start, 0, unroll=DMA_ISSUE_UNROLL)

    def wait_gathers(buf, sem):
        def wait(j, c):
            for k in range(TOP_K):
                row_copy(buf, sem, 0, k, 0).wait()
            return c
        lax.fori_loop(0, tm, wait, 0)

    @pl.when(step == 0)
    def _():
        start_gathers(0, *bufs[0])

    def consume(cur, nxt):
        start_gathers(jnp.minimum(step + 1, nsteps - 1), *nxt)
        h = h_ref[0]
        a = jnp.dot(h, w1_ref[...], preferred_element_type=F32)
        g = jnp.dot(h, w3_ref[...], preferred_element_type=F32)
        ffn = jnp.dot((_silu(a) * g).astype(BF16), w2_ref[...], preferred_element_type=F32)
        wait_gathers(*cur)
        ybuf = cur[0]
        wts = wt_ref[...]
        los = [None] * nch
        his = [None] * nch
        for k in range(TOP_K):
            wk = wts[:, k:k + 1]
            for c in range(nch):
                lo, hi = _unpack_rows(ybuf[k, pl.ds(c, tm, stride=nch), :])
                los[c] = wk * lo if k == 0 else los[c] + wk * lo
                his[c] = wk * hi if k == 0 else his[c] + wk * hi
        routed = jnp.concatenate(los + his, axis=1)
        y = x_ref[0] + gt_ref[0, 0] * (routed + ffn)
        o_ref[0] = _rms(y, fn_ref[...]) if final else y

    pl.when(step % 2 == 0)(lambda: consume(bufs[0], bufs[1]))
    pl.when(step % 2 == 1)(lambda: consume(bufs[1], bufs[0]))

    @pl.when(step == nsteps - 1)
    def _():
        wait_gathers(*bufs[nsteps % 2])


def _combine(dest, ys, wts, h, w1, w3, w2, x, mod, gt_idx, final_norm, nch, final):
    bsz, s, d = x.shape
    f = w1.shape[1]
    n_tok = bsz * s
    tm = _tile(s, 128)
    ns = s // tm
    return pl.pallas_call(
        functools.partial(_combine_kernel, n_tok=n_tok, tm=tm, nch=nch, nsteps=bsz * ns, final=final),
        grid_spec=pltpu.PrefetchScalarGridSpec(
            num_scalar_prefetch=1,
            grid=(bsz, ns),
            in_specs=[pl.BlockSpec(memory_space=pl.ANY),
                      pl.BlockSpec((tm, SUBLANES), lambda b, i, dst: (b * ns + i, 0)),
                      pl.BlockSpec((1, tm, d), lambda b, i, dst: (b, i, 0)),
                      pl.BlockSpec((d, f), lambda b, i, dst: (0, 0)),
                      pl.BlockSpec((d, f), lambda b, i, dst: (0, 0)),
                      pl.BlockSpec((f, d), lambda b, i, dst: (0, 0)),
                      pl.BlockSpec((1, tm, d), lambda b, i, dst: (b, i, 0)),
                      pl.BlockSpec((1, 1, 1, d), lambda b, i, dst: (gt_idx, b, 0, 0)),
                      pl.BlockSpec((1, d), lambda b, i, dst: (0, 0))],
            out_specs=pl.BlockSpec((1, tm, d), lambda b, i, dst: (b, i, 0)),
            scratch_shapes=[pltpu.VMEM((TOP_K, tm * nch, LANES), U32), pltpu.VMEM((TOP_K, tm * nch, LANES), U32),
                            pltpu.SemaphoreType.DMA, pltpu.SemaphoreType.DMA]),
        out_shape=jax.ShapeDtypeStruct((bsz, s, d), F32),
        compiler_params=_cparams(("arbitrary", "arbitrary")),
        name="moe_combine",
    )(dest, ys, wts, h, w1, w3, w2, x, mod, final_norm.reshape(1, d))


def _mla_weights(w_uq, w_ukv):
    heads = MLA_HEADS
    qr, kvr = w_uq.shape[0], w_ukv.shape[0]
    wq = w_uq.reshape(qr, heads, MLA_NOPE_DIM + MLA_ROPE_DIM)
    wq = jnp.pad(wq, ((0, 0), (0, 0), (0, MLA_QK_PAD - MLA_NOPE_DIM - MLA_ROPE_DIM)))
    wkv = w_ukv.reshape(kvr, heads, MLA_NOPE_DIM + MLA_V_DIM)
    wk = wkv[:, :, :MLA_NOPE_DIM].reshape(kvr, heads * MLA_NOPE_DIM)
    wv = wkv[:, :, MLA_NOPE_DIM:].reshape(kvr, heads * MLA_V_DIM)
    return (wq.reshape(qr, heads * MLA_QK_PAD).astype(BF16), wk.astype(BF16), wv.astype(BF16))


def _layer(x, mod, positions, tabs, l, norm_attn, w_in, diff_lambda, diff_subln, mla_q_norm, mla_w_uq,
           mla_kv_norm, mla_w_ukv, w_out, norm_ffn, router_w, router_bias, exp_w1, exp_w3, exp_w2,
           shared_w1, shared_w3, shared_w2, final_norm, final):
    bsz, s, d = x.shape
    n_tok = bsz * s
    q_rank, kv_rank = mla_w_uq.shape[0], mla_w_ukv.shape[0]
    qk_cols = 2 * DIFF_HEADS * DIFF_HEAD_DIM
    v_cols = DIFF_HEADS * 2 * DIFF_HEAD_DIM
    qkv_cols = 2 * qk_cols + v_cols
    lat_cols = q_rank + kv_rank + MLA_ROPE_DIM
    lam_init = 0.8 - 0.6 * math.exp(-0.3 * l)

    h = _norm_mod(x, norm_attn, mod, 1, 0).reshape(n_tok, d)
    w_qkv = w_in[:, :qkv_cols].astype(BF16)
    w_lat = jnp.pad(w_in[:, qkv_cols:qkv_cols + lat_cols], ((0, 0), (0, LANES - MLA_ROPE_DIM))).astype(BF16)
    w_gate = w_in[:, qkv_cols + lat_cols:].astype(BF16)
    qkv = _matmul(h, w_qkv, BF16, name="qkv_proj", scaled_cols=qk_cols,
                  col_scale=DIFF_HEAD_DIM ** -0.5 * LOG2_E)[0].reshape(bsz, s, qkv_cols)
    cq, ckv, kpe = _latent_proj(h, w_lat, q_rank, kv_rank)
    gates, exp_w3b = _matmul(h, w_gate, BF16, name="gate_proj", sigmoid=True, cast=(exp_w3,))
    gates = gates.reshape(bsz, s, 2 * d)
    o_d, exp_w2b = _diff_attention(qkv, positions, diff_lambda, diff_subln, lam_init, cast=(exp_w2,))

    wq, wk, wv = _mla_weights(mla_w_uq, mla_w_ukv)
    scale = (MLA_NOPE_DIM + MLA_ROPE_DIM) ** -0.5 * LOG2_E
    q_m = _mla_q(cq.reshape(bsz, s, q_rank), mla_q_norm, wq, tabs, scale)
    k_m, v_m = _mla_kv(ckv.reshape(bsz, s, kv_rank), mla_kv_norm, wk, wv, kpe.reshape(bsz, s, LANES), tabs)
    o_m, exp_w1b = _mla_attention(q_m, k_m, v_m, cast=(exp_w1,))

    w_o = w_out.astype(BF16)
    x = _merge(o_d, o_m, w_o[:v_cols], w_o[v_cols:], gates, x, mod, 2)

    n_exp = router_w.shape[1]
    nch = d // (2 * LANES)
    h2, hp, logits_t = _ffn_norm(x, norm_ffn, mod, 4, 3, router_w.T.astype(BF16))
    idx_t, wts_t, rank_t, sizes = _route(logits_t, router_bias)

    blk = EXPERT_ROWS
    sizes = sizes[:, 0]
    padded = (sizes + blk - 1) // blk * blk
    pad_end = jnp.cumsum(padded)
    pad_start = pad_end - padded
    onehot = idx_t[:TOP_K, :, None] == jnp.arange(n_exp, dtype=I32)
    dest = (jnp.sum(jnp.where(onehot, pad_start, 0), axis=-1) + rank_t[:TOP_K]).astype(I32).reshape(-1)
    n_blocks = -(-n_tok * TOP_K // blk) + n_exp
    block_start = jnp.arange(n_blocks, dtype=I32) * blk
    block_expert = jnp.minimum(jnp.sum(pad_end[None, :] <= block_start[:, None], axis=1), n_exp - 1).astype(I32)
    nact = (pad_end[-1:] // blk).astype(I32)
    seg = jnp.concatenate([pad_start + sizes, pad_end, nact]).astype(I32)

    xs = _dispatch(dest, seg, hp, n_blocks, blk, nch, n_exp)
    ys = _experts(block_expert, nact, xs, exp_w1b, exp_w3b, exp_w2b, nch)
    return _combine(dest, ys, wts_t.T, h2, shared_w1.astype(BF16), shared_w3.astype(BF16),
                    shared_w2.astype(BF16), x, mod, 5, final_norm, nch, final)


def kernel(x, c, positions, w_ada, b_ada, norm_attn, w_in, diff_lambda, diff_subln, mla_q_norm, mla_w_uq,
           mla_kv_norm, mla_w_ukv, w_out, norm_ffn, router_w, router_bias, exp_w1, exp_w3, exp_w2,
           shared_w1, shared_w3, shared_w2, final_norm):
    bsz, s, d = x.shape
    depth = w_ada.shape[0]
    tabs = _rope_tables(positions)
    for l in range(depth):
        mod = _ada(c, w_ada[l], b_ada[l])
        mod = mod.reshape(bsz, N_MOD, 1, d).transpose(1, 0, 2, 3)
        x = _layer(x, mod, positions, tabs, l, norm_attn[l], w_in[l], diff_lambda[l], diff_subln[l],
                   mla_q_norm[l], mla_w_uq[l], mla_kv_norm[l], mla_w_ukv[l], w_out[l], norm_ffn[l],
                   router_w[l], router_bias[l], exp_w1[l], exp_w3[l], exp_w2[l],
                   shared_w1[l], shared_w3[l], shared_w2[l], final_norm, l == depth - 1)
    return x
```

```python
import functools
import math

import jax
import jax.numpy as jnp
from jax import lax
from jax.experimental import pallas as pl
from jax.experimental.pallas import tpu as pltpu

F32 = jnp.float32
BF16 = jnp.bfloat16
U32 = jnp.uint32
I32 = jnp.int32

DIFF_HEADS = 8
DIFF_HEAD_DIM = 128
MLA_HEADS = 16
MLA_NOPE_DIM = 128
MLA_ROPE_DIM = 64
MLA_V_DIM = 128
ROPE_THETA = 10000.0
TOP_K = 6
N_GROUPS = 8
TOPK_GROUPS = 4
ROUTED_SCALE = 2.5
NORM_EPS = 1e-6
N_MOD = 6
LOG2_E = math.log2(math.e)

LANES = 128
SUBLANES = 8
MLA_QK_PAD = 256
EXPERT_ROWS = 256
DMA_ISSUE_UNROLL = 4
DMA_QUEUES = 2
ATTN_LOOKAHEAD = 2
VMEM_LIMIT = 56 * 1024 * 1024


def _cparams(sem):
    return pltpu.CompilerParams(dimension_semantics=sem, vmem_limit_bytes=VMEM_LIMIT)


def _tile(n, pref):
    t = min(n, pref)
    assert n % t == 0, (n, pref)
    return t


def _silu(a):
    return a * jax.nn.sigmoid(a)


def _ada_kernel(c_ref, w_ref, b_ref, o_ref):
    c = c_ref[...]
    a = _silu(c).astype(BF16)
    o_ref[...] = jnp.dot(a, w_ref[...].astype(BF16), preferred_element_type=F32) + b_ref[...]


def _ada(c, w, b):
    bsz, d = c.shape
    n = w.shape[1]
    tn = _tile(n, 512)
    return pl.pallas_call(
        _ada_kernel,
        grid=(n // tn,),
        in_specs=[pl.BlockSpec((bsz, d), lambda j: (0, 0)),
                  pl.BlockSpec((d, tn), lambda j: (0, j)),
                  pl.BlockSpec((1, tn), lambda j: (0, j))],
        out_specs=pl.BlockSpec((bsz, tn), lambda j: (0, j)),
        out_shape=jax.ShapeDtypeStruct((bsz, n), F32),
        compiler_params=_cparams(("arbitrary",)),
        name="ada_mod",
    )(c, w, b.reshape(1, n))


def _norm_mod_kernel(x_ref, g_ref, sc_ref, sh_ref, o_ref):
    x = x_ref[0]
    ms = jnp.mean(x * x, axis=-1, keepdims=True)
    y = x * lax.rsqrt(ms + NORM_EPS) * g_ref[...]
    o_ref[0] = (y * (1.0 + sc_ref[0, 0]) + sh_ref[0, 0]).astype(o_ref.dtype)


def _norm_mod(x, gain, mod, sc_idx, sh_idx):
    bsz, s, d = x.shape
    tm = _tile(s, 512)
    return pl.pallas_call(
        _norm_mod_kernel,
        grid=(bsz, s // tm),
        in_specs=[pl.BlockSpec((1, tm, d), lambda b, i: (b, i, 0)),
                  pl.BlockSpec((1, d), lambda b, i: (0, 0)),
                  pl.BlockSpec((1, 1, 1, d), lambda b, i: (sc_idx, b, 0, 0)),
                  pl.BlockSpec((1, 1, 1, d), lambda b, i: (sh_idx, b, 0, 0))],
        out_specs=pl.BlockSpec((1, tm, d), lambda b, i: (b, i, 0)),
        out_shape=jax.ShapeDtypeStruct((bsz, s, d), BF16),
        compiler_params=_cparams(("arbitrary", "arbitrary")),
        name="norm_mod",
    )(x, gain.reshape(1, d), mod, mod)


def _mm_kernel(a_ref, b_ref, *rest, sigmoid, scaled_tiles, col_scale, n_cast):
    cast_in, o_ref, cast_out = rest[:n_cast], rest[n_cast], rest[n_cast + 1:]
    acc = jnp.dot(a_ref[...], b_ref[...], preferred_element_type=F32)
    if sigmoid:
        acc = jax.nn.sigmoid(acc)
    if scaled_tiles:
        acc = acc * jnp.where(pl.program_id(1) < scaled_tiles, col_scale, 1.0)
    o_ref[...] = acc.astype(o_ref.dtype)
    _cast_slabs(cast_in, cast_out)


def _matmul(a, b, out_dtype, *, name, sigmoid=False, scaled_cols=0, col_scale=1.0, cast=(),
            tm_pref=1024, tn_pref=512):
    m, k = a.shape
    n = b.shape[1]
    tm, tn = _tile(m, tm_pref), _tile(n, tn_pref)
    assert scaled_cols % tn == 0
    nj = n // tn
    slabs = [_slabs(w, (m // tm) * nj) for w in cast]
    slab_specs = _slab_specs(slabs, lambda i, j: i * nj + j)
    out = pl.pallas_call(
        functools.partial(_mm_kernel, sigmoid=sigmoid, scaled_tiles=scaled_cols // tn, col_scale=col_scale,
                          n_cast=len(cast)),
        grid=(m // tm, nj),
        in_specs=[pl.BlockSpec((tm, k), lambda i, j: (i, 0)),
                  pl.BlockSpec((k, tn), lambda i, j: (0, j))] + slab_specs,
        out_specs=[pl.BlockSpec((tm, tn), lambda i, j: (i, j))] + slab_specs,
        out_shape=[jax.ShapeDtypeStruct((m, n), out_dtype)] + [jax.ShapeDtypeStruct(w.shape, BF16) for w in slabs],
        compiler_params=_cparams(("arbitrary", "arbitrary")),
        name=name,
    )(a, b, *slabs)
    return [out[0]] + [o.reshape(w.shape) for o, w in zip(out[1:], cast)]


def _latent_kernel(a_ref, b_ref, cq_ref, ckv_ref, kpe_ref, *, q_rank, kv_rank):
    acc = jnp.dot(a_ref[...], b_ref[...], preferred_element_type=F32)
    cq_ref[...] = acc[:, :q_rank].astype(cq_ref.dtype)
    ckv_ref[...] = acc[:, q_rank:q_rank + kv_rank].astype(ckv_ref.dtype)
    kpe_ref[...] = acc[:, q_rank + kv_rank:]


def _latent_proj(h, w_lat, q_rank, kv_rank):
    m, k = h.shape
    n = w_lat.shape[1]
    tm = _tile(m, 512)
    return pl.pallas_call(
        functools.partial(_latent_kernel, q_rank=q_rank, kv_rank=kv_rank),
        grid=(m // tm,),
        in_specs=[pl.BlockSpec((tm, k), lambda i: (i, 0)),
                  pl.BlockSpec((k, n), lambda i: (0, 0))],
        out_specs=[pl.BlockSpec((tm, q_rank), lambda i: (i, 0)),
                   pl.BlockSpec((tm, kv_rank), lambda i: (i, 0)),
                   pl.BlockSpec((tm, LANES), lambda i: (i, 0))],
        out_shape=[jax.ShapeDtypeStruct((m, q_rank), BF16),
                   jax.ShapeDtypeStruct((m, kv_rank), BF16),
                   jax.ShapeDtypeStruct((m, LANES), F32)],
        compiler_params=_cparams(("arbitrary",)),
        name="latent_proj",
    )(h, w_lat)


def _rope_table_kernel(pos_ref, inv_ref, c_ref, s1_ref, s2_ref):
    half = MLA_ROPE_DIM // 2
    ang = pos_ref[0].astype(F32) * inv_ref[...]
    cos, sin = jnp.cos(ang), jnp.sin(ang)
    lane = lax.broadcasted_iota(I32, ang.shape, 1)
    c_ref[0] = jnp.where(lane < 2 * half, cos, 0.0)
    s1_ref[0] = jnp.where(lane < half, -sin, 0.0)
    s2_ref[0] = jnp.where((lane >= half) & (lane < 2 * half), sin, 0.0)


def _rope_tables(positions):
    bsz, s = positions.shape
    half = MLA_ROPE_DIM // 2
    inv = ROPE_THETA ** (-(jnp.arange(LANES, dtype=F32) % half) / half)
    ts = _tile(s, 512)
    spec = pl.BlockSpec((1, ts, LANES), lambda b, i: (b, i, 0))
    shp = jax.ShapeDtypeStruct((bsz, s, LANES), F32)
    return pl.pallas_call(
        _rope_table_kernel,
        grid=(bsz, s // ts),
        in_specs=[pl.BlockSpec((1, ts, 1), lambda b, i: (b, i, 0)),
                  pl.BlockSpec((1, LANES), lambda b, i: (0, 0))],
        out_specs=[spec, spec, spec],
        out_shape=[shp, shp, shp],
        compiler_params=_cparams(("arbitrary", "arbitrary")),
        name="rope_tables",
    )(positions.reshape(bsz, s, 1), inv.reshape(1, LANES))


def _rotate(r, c, s1, s2):
    half = MLA_ROPE_DIM // 2
    return r * c + pltpu.roll(r, LANES - half, 1) * s1 + pltpu.roll(r, half, 1) * s2


def _rms(x, gain):
    ms = jnp.mean(x * x, axis=-1, keepdims=True)
    return x * lax.rsqrt(ms + NORM_EPS) * gain


def _mla_q_kernel(cq_ref, g_ref, w_ref, c_ref, s1_ref, s2_ref, o_ref, *, heads, scale):
    y = _rms(cq_ref[0].astype(F32), g_ref[...]).astype(BF16)
    q = jnp.dot(y, w_ref[...], preferred_element_type=F32)
    c, s1, s2 = c_ref[0], s1_ref[0], s2_ref[0]
    for h in range(heads):
        base = h * MLA_QK_PAD
        o_ref[0, :, base:base + LANES] = (q[:, base:base + LANES] * scale).astype(BF16)
        r = q[:, base + LANES:base + 2 * LANES]
        o_ref[0, :, base + LANES:base + 2 * LANES] = (_rotate(r, c, s1, s2) * scale).astype(BF16)


def _mla_q(cq, gain, w_q, tabs, scale):
    bsz, s, qr = cq.shape
    heads = MLA_HEADS
    n = heads * MLA_QK_PAD
    tm = _tile(s, 512)
    tab_spec = pl.BlockSpec((1, tm, LANES), lambda b, i: (b, i, 0))
    return pl.pallas_call(
        functools.partial(_mla_q_kernel, heads=heads, scale=scale),
        grid=(bsz, s // tm),
        in_specs=[pl.BlockSpec((1, tm, qr), lambda b, i: (b, i, 0)),
                  pl.BlockSpec((1, qr), lambda b, i: (0, 0)),
                  pl.BlockSpec((qr, n), lambda b, i: (0, 0)),
                  tab_spec, tab_spec, tab_spec],
        out_specs=pl.BlockSpec((1, tm, n), lambda b, i: (b, i, 0)),
        out_shape=jax.ShapeDtypeStruct((bsz, s, n), BF16),
        compiler_params=_cparams(("arbitrary", "arbitrary")),
        name="mla_q_prep",
    )(cq, gain.reshape(1, qr), w_q, *tabs)


def _mla_kv_kernel(ckv_ref, g_ref, wk_ref, wv_ref, kpe_ref, c_ref, s1_ref, s2_ref, k_ref, v_ref, *, heads):
    y = _rms(ckv_ref[0].astype(F32), g_ref[...]).astype(BF16)
    kn = jnp.dot(y, wk_ref[...], preferred_element_type=F32)
    v_ref[0] = jnp.dot(y, wv_ref[...], preferred_element_type=F32).astype(BF16)
    kr = _rotate(kpe_ref[0], c_ref[0], s1_ref[0], s2_ref[0]).astype(BF16)
    for h in range(heads):
        base = h * MLA_QK_PAD
        k_ref[0, :, base:base + LANES] = kn[:, h * LANES:(h + 1) * LANES].astype(BF16)
        k_ref[0, :, base + LANES:base + 2 * LANES] = kr


def _mla_kv(ckv, gain, w_k, w_v, kpe, tabs):
    bsz, s, kvr = ckv.shape
    heads = MLA_HEADS
    tm = _tile(s, 512)
    tab_spec = pl.BlockSpec((1, tm, LANES), lambda b, i: (b, i, 0))
    return pl.pallas_call(
        functools.partial(_mla_kv_kernel, heads=heads),
        grid=(bsz, s // tm),
        in_specs=[pl.BlockSpec((1, tm, kvr), lambda b, i: (b, i, 0)),
                  pl.BlockSpec((1, kvr), lambda b, i: (0, 0)),
                  pl.BlockSpec((kvr, heads * MLA_NOPE_DIM), lambda b, i: (0, 0)),
                  pl.BlockSpec((kvr, heads * MLA_V_DIM), lambda b, i: (0, 0)),
                  tab_spec, tab_spec, tab_spec, tab_spec],
        out_specs=[pl.BlockSpec((1, tm, heads * MLA_QK_PAD), lambda b, i: (b, i, 0)),
                   pl.BlockSpec((1, tm, heads * MLA_V_DIM), lambda b, i: (b, i, 0))],
        out_shape=[jax.ShapeDtypeStruct((bsz, s, heads * MLA_QK_PAD), BF16),
                   jax.ShapeDtypeStruct((bsz, s, heads * MLA_V_DIM), BF16)],
        compiler_params=_cparams(("arbitrary", "arbitrary")),
        name="mla_kv_prep",
    )(ckv, gain.reshape(1, kvr), w_k, w_v, kpe, *tabs)


def _qk(q, k):
    return lax.dot_general(q, k, (((1,), (1,)), ((), ())), preferred_element_type=F32)


def _causal_mask(s, qi, ki, tq, tk):
    row = qi * tq + lax.broadcasted_iota(I32, s.shape, 0)
    col = ki * tk + lax.broadcasted_iota(I32, s.shape, 1)
    return jnp.where(row >= col, s, -jnp.inf)


def _lane_chunks(x):
    return [x[:, c * LANES:(c + 1) * LANES] for c in range(x.shape[1] // LANES)]


def _softmax_update(s, v, m_prev, l_prev, acc_prev):
    chunks = _lane_chunks(s)
    cmax = functools.reduce(jnp.maximum, chunks)
    m_new = jnp.maximum(m_prev, jnp.max(cmax, axis=-1, keepdims=True))
    alpha = jnp.exp2(m_prev - m_new)
    ps = [jnp.exp2(c - m_new) for c in chunks]
    psum = functools.reduce(lambda a, b: a + b, ps)
    l_new = alpha * l_prev + jnp.sum(psum, axis=-1, keepdims=True)
    p = jnp.concatenate([c.astype(BF16) for c in ps], axis=1)
    pv = jnp.dot(p, v, preferred_element_type=F32)
    acc_new = jnp.concatenate([alpha * a for a in _lane_chunks(acc_prev)], axis=1) + pv
    return m_new, l_new, acc_new


def _emit_pipelined(score_fns, update_fns):
    n = len(score_fns)
    pending = [score_fns[i]() for i in range(min(ATTN_LOOKAHEAD, n))]
    for i in range(n):
        if i + ATTN_LOOKAHEAD < n:
            pending.append(score_fns[i + ATTN_LOOKAHEAD]())
        update_fns[i](pending[i])
        pending[i] = None


def _diag_mask(s, r):
    rows = s.shape[0]
    off = s.shape[1] - rows
    row = lax.broadcasted_iota(I32, s.shape, 0)
    col = lax.broadcasted_iota(I32, s.shape, 1)
    return jnp.where(col - off <= row, s, -jnp.inf)


def _slabs(w, steps):
    cols = w.shape[-1]
    rows = w.size // cols
    assert rows % (steps * 2 * SUBLANES) == 0, (w.shape, steps)
    return w.reshape(steps, rows // steps, cols)


def _slab_specs(slabs, step_of):
    return [pl.BlockSpec((1,) + w.shape[1:], lambda *g: (step_of(*g), 0, 0)) for w in slabs]


def _cast_slabs(src_refs, dst_refs):
    for src, dst in zip(src_refs, dst_refs):
        dst[...] = src[...].astype(BF16)


def _mla_attn_kernel(q_ref, k_ref, v_ref, *rest, tq, tk, rb, n_cast):
    cast_in, o_ref, cast_out, scr = (rest[:n_cast], rest[n_cast], rest[n_cast + 1:2 * n_cast + 1],
                                     rest[2 * n_cast + 1:])
    qi = pl.program_id(2)
    nr = tq // rb
    state = [scr[3 * r:3 * r + 3] for r in range(nr)]
    for m_scr, l_scr, acc_scr in state:
        m_scr[...] = jnp.full(m_scr.shape, -jnp.inf, F32)
        l_scr[...] = jnp.zeros(l_scr.shape, F32)
        acc_scr[...] = jnp.zeros(acc_scr.shape, F32)

    def rows(r):
        return pl.ds(r * rb, rb)

    def update(r, s, v):
        m_scr, l_scr, acc_scr = state[r]
        m, l, acc = _softmax_update(s, v, m_scr[...], l_scr[...], acc_scr[...])
        m_scr[...] = m
        l_scr[...] = l
        acc_scr[...] = acc

    def full_tile(ki, carry):
        start = pl.multiple_of(ki * tk, tk)
        k = k_ref[0, pl.ds(start, tk), :]
        v = v_ref[0, pl.ds(start, tk), :]
        _emit_pipelined([functools.partial(_qk, q_ref[0, rows(r), :], k) for r in range(nr)],
                        [functools.partial(update, r, v=v) for r in range(nr)])
        return carry
    lax.fori_loop(0, qi * (tq // tk), full_tile, 0)

    _cast_slabs(cast_in, cast_out)
    dstart = pl.multiple_of(qi * tq, tq)
    cols = [pl.ds(dstart, (r + 1) * rb) for r in range(nr)]
    _emit_pipelined(
        [lambda r=r: _diag_mask(_qk(q_ref[0, rows(r), :], k_ref[0, cols[r], :]), r) for r in range(nr)],
        [lambda s, r=r: update(r, s, v_ref[0, cols[r], :]) for r in range(nr)])

    for r, (_, l_scr, acc_scr) in enumerate(state):
        o_ref[0, rows(r), :] = (acc_scr[...] / l_scr[...]).astype(o_ref.dtype)


def _attn_state_scratch(nr, rb, dv):
    return [pltpu.VMEM((rb, w), F32) for _ in range(nr) for w in (LANES, LANES, dv)]


def _mla_attention(q, k, v, cast=()):
    bsz, s, _ = q.shape
    heads = MLA_HEADS
    t = _tile(s, 1024)
    rb = _tile(t, 256)
    nq = s // t
    slabs = [_slabs(w, bsz * heads * nq) for w in cast]
    slab_specs = _slab_specs(slabs, lambda b, h, i: (b * heads + h) * nq + i)
    out = pl.pallas_call(
        functools.partial(_mla_attn_kernel, tq=t, tk=t, rb=rb, n_cast=len(cast)),
        grid=(bsz, heads, nq),
        in_specs=[pl.BlockSpec((1, t, MLA_QK_PAD), lambda b, h, i: (b, i, h)),
                  pl.BlockSpec((1, s, MLA_QK_PAD), lambda b, h, i: (b, 0, h)),
                  pl.BlockSpec((1, s, MLA_V_DIM), lambda b, h, i: (b, 0, h))] + slab_specs,
        out_specs=[pl.BlockSpec((1, t, MLA_V_DIM), lambda b, h, i: (b, i, h))] + slab_specs,
        out_shape=[jax.ShapeDtypeStruct((bsz, s, heads * MLA_V_DIM), BF16)]
                  + [jax.ShapeDtypeStruct(w.shape, BF16) for w in slabs],
        scratch_shapes=_attn_state_scratch(t // rb, rb, MLA_V_DIM),
        compiler_params=_cparams(("arbitrary", "arbitrary", "arbitrary")),
        name="mla_attn",
    )(q, k, v, *slabs)
    return [out[0]] + [o.reshape(w.shape) for o, w in zip(out[1:], cast)]


def _diff_attn_kernel(q_ref, k_ref, v_ref, qpos_ref, kpos_ref, slope_ref, lam_ref, subln_ref, *rest,
                      tq, tk, rb, lam_init, n_cast):
    cast_in, o_ref, cast_out, scr = (rest[:n_cast], rest[n_cast], rest[n_cast + 1:2 * n_cast + 1],
                                     rest[2 * n_cast + 1:])
    qi = pl.program_id(2)
    d = DIFF_HEAD_DIM
    nr = tq // rb
    state = [[scr[6 * r + 3 * g:6 * r + 3 * g + 3] for g in range(2)] for r in range(nr)]
    for r in range(nr):
        for m_scr, l_scr, a_scr in state[r]:
            m_scr[...] = jnp.full(m_scr.shape, -jnp.inf, F32)
            l_scr[...] = jnp.zeros(l_scr.shape, F32)
            a_scr[...] = jnp.zeros(a_scr.shape, F32)
    slope = slope_ref[0, :, 0:1]

    def rows(r):
        return pl.ds(r * rb, rb)

    def kpos(first, count):
        return jnp.concatenate([kpos_ref[0, first + j] for j in range(count)], axis=1)

    chains = [(r, g) for r in range(nr) for g in range(2)]

    def emit(k_of, v_of, kp_of, masked):
        bias = {}

        def score(r, g):
            if g == 0:
                bias[r] = slope * jnp.abs(qpos_ref[0, rows(r), :] - kp_of(r))
            s = _qk(q_ref[0, rows(r), g * d:(g + 1) * d], k_of(r)[:, g * d:(g + 1) * d]) - bias[r]
            return _diag_mask(s, r) if masked else s

        def update(r, g, s):
            m_scr, l_scr, a_scr = state[r][g]
            m, l, acc = _softmax_update(s, v_of(r), m_scr[...], l_scr[...], a_scr[...])
            m_scr[...] = m
            l_scr[...] = l
            a_scr[...] = acc

        _emit_pipelined([functools.partial(score, r, g) for r, g in chains],
                        [functools.partial(update, r, g) for r, g in chains])

    def full_tile(ki, carry):
        start = pl.multiple_of(ki * tk, tk)
        k = k_ref[0, pl.ds(start, tk), :]
        v = v_ref[0, pl.ds(start, tk), :]
        kp = kpos(ki * (tk // rb), tk // rb)
        emit(lambda r: k, lambda r: v, lambda r: kp, False)
        return carry
    lax.fori_loop(0, qi * (tq // tk), full_tile, 0)

    _cast_slabs(cast_in, cast_out)
    dstart = pl.multiple_of(qi * tq, tq)
    cols = [pl.ds(dstart, (r + 1) * rb) for r in range(nr)]
    emit(lambda r: k_ref[0, cols[r], :], lambda r: v_ref[0, cols[r], :], lambda r: kpos(qi * nr, r + 1), True)

    lp = lam_ref[...]
    e1 = jnp.exp(jnp.sum(lp[0:1] * lp[1:2], axis=-1, keepdims=True))
    e2 = jnp.exp(jnp.sum(lp[2:3] * lp[3:4], axis=-1, keepdims=True))
    lam = e1 - e2 + lam_init
    for r in range(nr):
        (_, l1, a1), (_, l2, a2) = state[r]
        o1 = [a / l1[...] for a in _lane_chunks(a1[...])]
        o2 = [a / l2[...] for a in _lane_chunks(a2[...])]
        o = jnp.concatenate([x - lam * y for x, y in zip(o1, o2)], axis=1)
        o_ref[0, rows(r), :] = (_rms(o, subln_ref[...]) * (1.0 - lam_init)).astype(o_ref.dtype)


def _diff_attention(qkv, positions, diff_lambda, subln, lam_init, cast=()):
    bsz, s, _ = qkv.shape
    heads = DIFF_HEADS
    dv = 2 * DIFF_HEAD_DIM
    t = _tile(s, 1024)
    tk = _tile(t, 512)
    rb = _tile(tk, 256)
    nq = s // t
    slopes = 2.0 ** (-8.0 * jnp.arange(1, heads + 1, dtype=F32) / heads) * LOG2_E
    slopes = jnp.broadcast_to(slopes[:, None, None], (heads, 1, LANES))
    posf = positions.astype(F32)
    slabs = [_slabs(w, bsz * heads * nq) for w in cast]
    slab_specs = _slab_specs(slabs, lambda b, h, i: (b * heads + h) * nq + i)
    out = pl.pallas_call(
        functools.partial(_diff_attn_kernel, tq=t, tk=tk, rb=rb, lam_init=lam_init, n_cast=len(cast)),
        grid=(bsz, heads, nq),
        in_specs=[pl.BlockSpec((1, t, dv), lambda b, h, i: (b, i, h)),
                  pl.BlockSpec((1, s, dv), lambda b, h, i: (b, 0, heads + h)),
                  pl.BlockSpec((1, s, dv), lambda b, h, i: (b, 0, 2 * heads + h)),
                  pl.BlockSpec((1, t, 1), lambda b, h, i: (b, i, 0)),
                  pl.BlockSpec((1, s // rb, 1, rb), lambda b, h, i: (b, 0, 0, 0)),
                  pl.BlockSpec((1, 1, LANES), lambda b, h, i: (h, 0, 0)),
                  pl.BlockSpec((4, DIFF_HEAD_DIM), lambda b, h, i: (0, 0)),
                  pl.BlockSpec((1, dv), lambda b, h, i: (0, 0))] + slab_specs,
        out_specs=[pl.BlockSpec((1, t, dv), lambda b, h, i: (b, i, h))] + slab_specs,
        out_shape=[jax.ShapeDtypeStruct((bsz, s, heads * dv), BF16)]
                  + [jax.ShapeDtypeStruct(w.shape, BF16) for w in slabs],
        scratch_shapes=_attn_state_scratch(2 * (t // rb), rb, dv),
        compiler_params=_cparams(("arbitrary", "arbitrary", "arbitrary")),
        name="diff_attn",
    )(qkv, qkv, qkv, posf.reshape(bsz, s, 1), posf.reshape(bsz, s // rb, 1, rb), slopes, diff_lambda,
      subln.reshape(1, dv), *slabs)
    return [out[0]] + [o.reshape(w.shape) for o, w in zip(out[1:], cast)]


def _merge_kernel(od_ref, om_ref, wd_ref, wm_ref, g0_ref, g1_ref, x_ref, gt_ref, o_ref):
    yd = jnp.dot(od_ref[0], wd_ref[...], preferred_element_type=F32)
    ym = jnp.dot(om_ref[0], wm_ref[...], preferred_element_type=F32)
    y = g0_ref[0].astype(F32) * yd + g1_ref[0].astype(F32) * ym
    o_ref[0] = x_ref[0] + gt_ref[0, 0] * y


def _merge(o_d, o_m, w_d, w_m, gates, x, mod, gt_idx):
    bsz, s, d = x.shape
    kd, km = o_d.shape[-1], o_m.shape[-1]
    tm, tn = _tile(s, 1024), _tile(d, 512)
    nj = d // tn
    return pl.pallas_call(
        _merge_kernel,
        grid=(bsz, s // tm, nj),
        in_specs=[pl.BlockSpec((1, tm, kd), lambda b, i, j: (b, i, 0)),
                  pl.BlockSpec((1, tm, km), lambda b, i, j: (b, i, 0)),
                  pl.BlockSpec((kd, tn), lambda b, i, j: (0, j)),
                  pl.BlockSpec((km, tn), lambda b, i, j: (0, j)),
                  pl.BlockSpec((1, tm, tn), lambda b, i, j: (b, i, j)),
                  pl.BlockSpec((1, tm, tn), lambda b, i, j: (b, i, j + nj)),
                  pl.BlockSpec((1, tm, tn), lambda b, i, j: (b, i, j)),
                  pl.BlockSpec((1, 1, 1, tn), lambda b, i, j: (gt_idx, b, 0, j))],
        out_specs=pl.BlockSpec((1, tm, tn), lambda b, i, j: (b, i, j)),
        out_shape=jax.ShapeDtypeStruct((bsz, s, d), F32),
        compiler_params=_cparams(("arbitrary", "arbitrary", "arbitrary")),
        name="out_merge",
    )(o_d, o_m, w_d, w_m, gates, gates, x, mod)


def _pack_rows(y):
    half = y.shape[1] // 2
    bits = lax.bitcast_convert_type(y.astype(BF16).astype(F32), U32)
    return (bits[:, half:] & jnp.uint32(0xFFFF0000)) | (bits[:, :half] >> 16)


def _unpack_rows(w):
    lo = lax.bitcast_convert_type(w << 16, F32)
    hi = lax.bitcast_convert_type(w & jnp.uint32(0xFFFF0000), F32)
    return lo, hi


def _ffn_norm_kernel(x_ref, g_ref, sc_ref, sh_ref, rw_ref, h_ref, hp_ref, lt_ref, *, nch):
    x = x_ref[0]
    y = _rms(x, g_ref[...]) * (1.0 + sc_ref[0, 0]) + sh_ref[0, 0]
    hb = y.astype(BF16)
    h_ref[0] = hb
    lt_ref[...] = lax.dot_general(rw_ref[...], hb, (((1,), (1,)), ((), ())), preferred_element_type=F32)
    packed = _pack_rows(y)
    for c in range(nch):
        hp_ref[pl.ds(c, x.shape[0], stride=nch), :] = packed[:, c * LANES:(c + 1) * LANES]


def _ffn_norm(x, gain, mod, sc_idx, sh_idx, router_wt):
    bsz, s, d = x.shape
    e = router_wt.shape[0]
    nch = d // (2 * LANES)
    tm = _tile(s, 256)
    ns = s // tm
    return pl.pallas_call(
        functools.partial(_ffn_norm_kernel, nch=nch),
        grid=(bsz, ns),
        in_specs=[pl.BlockSpec((1, tm, d), lambda b, i: (b, i, 0)),
                  pl.BlockSpec((1, d), lambda b, i: (0, 0)),
                  pl.BlockSpec((1, 1, 1, d), lambda b, i: (sc_idx, b, 0, 0)),
                  pl.BlockSpec((1, 1, 1, d), lambda b, i: (sh_idx, b, 0, 0)),
                  pl.BlockSpec((e, d), lambda b, i: (0, 0))],
        out_specs=[pl.BlockSpec((1, tm, d), lambda b, i: (b, i, 0)),
                   pl.BlockSpec((tm * nch, LANES), lambda b, i: (b * ns + i, 0)),
                   pl.BlockSpec((e, tm), lambda b, i: (0, b * ns + i))],
        out_shape=[jax.ShapeDtypeStruct((bsz, s, d), BF16),
                   jax.ShapeDtypeStruct((bsz * s * nch, LANES), U32),
                   jax.ShapeDtypeStruct((e, bsz * s), F32)],
        compiler_params=_cparams(("arbitrary", "arbitrary")),
        name="ffn_norm_router",
    )(x, gain.reshape(1, d), mod, mod, router_wt)


def _first_index(hit, iota, axis, size):
    return jnp.min(jnp.where(hit, iota, size), axis=axis, keepdims=True)


def _route_kernel(lt_ref, bias_ref, idx_ref, wt_ref, rank_ref, sizes_ref, cnt_scr, *, n_exp, tn):
    i = pl.program_id(0)
    gsz = n_exp // N_GROUPS

    @pl.when(i == 0)
    def _():
        cnt_scr[...] = jnp.zeros(cnt_scr.shape, F32)

    scores = jax.nn.sigmoid(lt_ref[...])
    sel = scores + bias_ref[...]
    sel3 = sel.reshape(N_GROUPS, gsz, tn)
    j3 = lax.broadcasted_iota(I32, sel3.shape, 1)
    top1 = jnp.max(sel3, axis=1, keepdims=True)
    first = _first_index(sel3 == top1, j3, 1, gsz)
    top2 = jnp.max(jnp.where(j3 == first, -jnp.inf, sel3), axis=1, keepdims=True)
    gscore = (top1 + top2).reshape(N_GROUPS, tn)

    giota = lax.broadcasted_iota(I32, gscore.shape, 0)
    gmask = jnp.zeros(gscore.shape, jnp.bool_)
    for _ in range(TOPK_GROUPS):
        best = jnp.max(gscore, axis=0, keepdims=True)
        gi = _first_index(gscore == best, giota, 0, N_GROUPS)
        hit = giota == gi
        gmask = gmask | hit
        gscore = jnp.where(hit, -jnp.inf, gscore)

    emask = jnp.broadcast_to(gmask.reshape(N_GROUPS, 1, tn), sel3.shape)
    cand = jnp.where(emask, sel3, -jnp.inf).reshape(n_exp, tn)
    eiota = lax.broadcasted_iota(I32, cand.shape, 0)
    hits, idxs, vals = [], [], []
    for _ in range(TOP_K):
        best = jnp.max(cand, axis=0, keepdims=True)
        ei = _first_index(cand == best, eiota, 0, n_exp)
        hit = eiota == ei
        hits.append(hit)
        idxs.append(ei)
        vals.append(jnp.sum(jnp.where(hit, scores, 0.0), axis=0, keepdims=True))
        cand = jnp.where(hit, -jnp.inf, cand)
    total = functools.reduce(lambda a, b: a + b, vals)

    chosen = functools.reduce(lambda a, b: a | b, hits)
    onehot = jnp.where(chosen, 1.0, 0.0)
    r = lax.broadcasted_iota(I32, (tn, tn), 0)
    c = lax.broadcasted_iota(I32, (tn, tn), 1)
    upper = jnp.where(r < c, 1.0, 0.0).astype(BF16)
    before = cnt_scr[...] + jnp.dot(onehot.astype(BF16), upper, preferred_element_type=F32)
    cnt_new = cnt_scr[...] + jnp.sum(onehot, axis=1, keepdims=True)
    cnt_scr[...] = cnt_new

    pad = SUBLANES - TOP_K
    ranks = [jnp.sum(jnp.where(h, before, 0.0), axis=0, keepdims=True).astype(I32) for h in hits]
    zi = [jnp.zeros((pad, tn), I32)]
    idx_ref[...] = jnp.concatenate(idxs + zi, axis=0)
    rank_ref[...] = jnp.concatenate(ranks + zi, axis=0)
    wt_ref[...] = jnp.concatenate([v / total * ROUTED_SCALE for v in vals] + [jnp.zeros((pad, tn), F32)], axis=0)
    sizes_ref[...] = jnp.broadcast_to(cnt_new, sizes_ref.shape).astype(I32)


def _route(logits_t, bias):
    n_exp, n = logits_t.shape
    tn = _tile(n, 512)
    row_spec = pl.BlockSpec((SUBLANES, tn), lambda i: (0, i))
    return pl.pallas_call(
        functools.partial(_route_kernel, n_exp=n_exp, tn=tn),
        grid=(n // tn,),
        in_specs=[pl.BlockSpec((n_exp, tn), lambda i: (0, i)),
                  pl.BlockSpec((n_exp, 1), lambda i: (0, 0))],
        out_specs=[row_spec, row_spec, row_spec, pl.BlockSpec((n_exp, LANES), lambda i: (0, 0))],
        out_shape=[jax.ShapeDtypeStruct((SUBLANES, n), I32),
                   jax.ShapeDtypeStruct((SUBLANES, n), F32),
                   jax.ShapeDtypeStruct((SUBLANES, n), I32),
                   jax.ShapeDtypeStruct((n_exp, LANES), I32)],
        scratch_shapes=[pltpu.VMEM((n_exp, 1), F32)],
        compiler_params=_cparams(("arbitrary",)),
        name="route_topk",
    )(logits_t, bias.reshape(n_exp, 1))


def _dispatch_kernel(dest_ref, seg_ref, hp_ref, xs_ref, zero_scr, sem, zsem, *, n_tok, tc, nch, n_exp, blk, nb):
    i = pl.program_id(0)

    def row_copy(j, slot):
        return pltpu.make_async_copy(hp_ref.at[pl.ds(pl.multiple_of(j * nch, nch), nch)],
                                     xs_ref.at[pl.ds(pl.multiple_of(slot * nch, nch), nch)], sem)

    def zero_row(slot):
        return pltpu.make_async_copy(zero_scr.at[pl.ds(0, nch)],
                                     xs_ref.at[pl.ds(pl.multiple_of(slot * nch, nch), nch)], zsem)

    def zero_block(b):
        return pltpu.make_async_copy(zero_scr,
                                     xs_ref.at[pl.ds(pl.multiple_of(b * (blk * nch), blk * nch), blk * nch)], zsem)

    def start_then_wait(lo, hi, copy):
        def start(v, c):
            copy(v).start()
            return c
        lax.fori_loop(lo, hi, start, 0)

        def wait(v, c):
            copy(v).wait()
            return c
        lax.fori_loop(lo, hi, wait, 0)

    @pl.when(i == 0)
    def _():
        zero_scr[...] = jnp.zeros(zero_scr.shape, U32)

        def per_expert(e, carry):
            start_then_wait(seg_ref[e], seg_ref[n_exp + e], zero_row)
            return carry
        lax.fori_loop(0, n_exp, per_expert, 0)
        start_then_wait(seg_ref[2 * n_exp], nb, zero_block)

    base = i * tc

    def start(j, c):
        for k in range(TOP_K):
            row_copy(j, dest_ref[k * n_tok + base + j]).start(priority=k % DMA_QUEUES)
        return c
    lax.fori_loop(0, tc, start, 0, unroll=DMA_ISSUE_UNROLL)

    def wait(j, c):
        for k in range(TOP_K):
            row_copy(0, 0).wait()
        return c
    lax.fori_loop(0, tc, wait, 0)


def _dispatch(dest, seg, hp, nb, blk, nch, n_exp):
    n_tok = dest.shape[0] // TOP_K
    tc = _tile(n_tok, 256)
    return pl.pallas_call(
        functools.partial(_dispatch_kernel, n_tok=n_tok, tc=tc, nch=nch, n_exp=n_exp, blk=blk, nb=nb),
        grid_spec=pltpu.PrefetchScalarGridSpec(
            num_scalar_prefetch=2,
            grid=(n_tok // tc,),
            in_specs=[pl.BlockSpec((tc * nch, LANES), lambda i, dst, sg: (i, 0))],
            out_specs=pl.BlockSpec(memory_space=pl.ANY),
            scratch_shapes=[pltpu.VMEM((blk * nch, LANES), U32), pltpu.SemaphoreType.DMA,
                            pltpu.SemaphoreType.DMA]),
        out_shape=jax.ShapeDtypeStruct((nb * blk * nch, LANES), U32),
        compiler_params=_cparams(("arbitrary",)),
        name="moe_dispatch",
    )(dest, seg, hp)


def _load_rows(ref, tm, nch):
    los, his = [], []
    for c in range(nch):
        lo, hi = _unpack_rows(ref[pl.ds(c, tm, stride=nch), :])
        los.append(lo.astype(BF16))
        his.append(hi.astype(BF16))
    return jnp.concatenate(los + his, axis=1)


def _expert_kernel(be_ref, nact_ref, xs_ref, w1_ref, w3_ref, w2_ref, ys_ref, *, tm, nch):
    active = pl.program_id(0) < nact_ref[0]

    @pl.when(jnp.logical_not(active))
    def _():
        ys_ref[...] = jnp.zeros(ys_ref.shape, U32)

    @pl.when(active)
    def _():
        x = _load_rows(xs_ref, tm, nch)
        a = jnp.dot(x, w1_ref[0], preferred_element_type=F32)
        b = jnp.dot(x, w3_ref[0], preferred_element_type=F32)
        y = jnp.dot((_silu(a) * b).astype(BF16), w2_ref[0], preferred_element_type=F32)
        packed = _pack_rows(y)
        for c in range(nch):
            ys_ref[pl.ds(c, tm, stride=nch), :] = packed[:, c * LANES:(c + 1) * LANES]


def _experts(block_expert, nact, xs, w1, w3, w2, nch):
    n_exp, d, f = w1.shape
    tm = EXPERT_ROWS
    nb = xs.shape[0] // (tm * nch)
    blk = lambda i, be, na: (jnp.minimum(i, na[0] - 1), 0)
    wsel = lambda i, be, na: (be[jnp.minimum(i, na[0] - 1)], 0, 0)
    return pl.pallas_call(
        functools.partial(_expert_kernel, tm=tm, nch=nch),
        grid_spec=pltpu.PrefetchScalarGridSpec(
            num_scalar_prefetch=2,
            grid=(nb,),
            in_specs=[pl.BlockSpec((tm * nch, LANES), blk),
                      pl.BlockSpec((1, d, f), wsel),
                      pl.BlockSpec((1, d, f), wsel),
                      pl.BlockSpec((1, f, d), wsel)],
            out_specs=pl.BlockSpec((tm * nch, LANES), lambda i, be, na: (i, 0))),
        out_shape=jax.ShapeDtypeStruct(xs.shape, U32),
        compiler_params=_cparams(("arbitrary",)),
        name="moe_experts",
    )(block_expert, nact, xs, w1, w3, w2)


def _shared_ffn_kernel(h_ref, w1_ref, w3_ref, w2_ref, o_ref, *, out_chunk):
    h = h_ref[...]
    a = jnp.dot(h, w1_ref[...], preferred_element_type=F32)
    g = jnp.dot(h, w3_ref[...], preferred_element_type=F32)
    mid = (_silu(a) * g).astype(BF16)
    for c in range(0, o_ref.shape[1], out_chunk):
        o_ref[:, c:c + out_chunk] = jnp.dot(mid, w2_ref[:, c:c + out_chunk],
                                            preferred_element_type=F32).astype(o_ref.dtype)


def _shared_ffn(h, w1, w3, w2):
    n, d = h.shape
    f = w1.shape[1]
    tm = _tile(n, 512)
    return pl.pallas_call(
        functools.partial(_shared_ffn_kernel, out_chunk=_tile(d, 1024)),
        grid=(n // tm,),
        in_specs=[pl.BlockSpec((tm, d), lambda i: (i, 0)),
                  pl.BlockSpec((d, f), lambda i: (0, 0)),
                  pl.BlockSpec((d, f), lambda i: (0, 0)),
                  pl.BlockSpec((f, d), lambda i: (0, 0))],
        out_specs=pl.BlockSpec((tm, d), lambda i: (i, 0)),
        out_shape=jax.ShapeDtypeStruct((n, d), BF16),
        compiler_params=_cparams(("arbitrary",)),
        name="shared_ffn",
    )(h, w1, w3, w2)


def _combine_kernel(dest_ref, ys_ref, wt_ref, shared_ref, x_ref, gt_ref, fn_ref, o_ref,
                    ybuf0, ybuf1, sem0, sem1, *, n_tok, tm, nch, nsteps, final):
    step = pl.program_id(0) * pl.num_programs(1) + pl.program_id(1)
    bufs = ((ybuf0, sem0), (ybuf1, sem1))

    def row_copy(buf, sem, slot, k, j):
        return pltpu.make_async_copy(ys_ref.at[pl.ds(pl.multiple_of(slot * nch, nch), nch)],
                                     buf.at[k, :, j], sem)

    def start_gathers(tile, buf, sem, inline=False):
        base = tile * tm

        def start(j, c):
            for k in range(TOP_K):
                row_copy(buf, sem, dest_ref[k * n_tok + base + j], k, j).start(priority=k % DMA_QUEUES)
            return c
        if inline:
            for j in range(tm):
                start(j, 0)
        else:
            lax.fori_loop(0, tm, start, 0, unroll=DMA_ISSUE_UNROLL)

    def wait_gathers(buf, sem):
        def wait(j, c):
            for k in range(TOP_K):
                row_copy(buf, sem, 0, k, 0).wait()
            return c
        lax.fori_loop(0, tm, wait, 0)

    @pl.when(step == 0)
    def _():
        start_gathers(0, *bufs[0])

    def consume(cur, nxt):
        wait_gathers(*cur)
        start_gathers(jnp.minimum(step + 1, nsteps - 1), *nxt, inline=True)
        ybuf = cur[0]
        wts = wt_ref[...]
        los = [None] * nch
        his = [None] * nch
        for k in range(TOP_K):
            wk = jnp.broadcast_to(wts[:, k:k + 1], (tm, LANES))
            for c in range(nch):
                lo, hi = _unpack_rows(ybuf[k, c])
                los[c] = wk * lo if k == 0 else los[c] + wk * lo
                his[c] = wk * hi if k == 0 else his[c] + wk * hi
        routed = jnp.concatenate(los + his, axis=1)
        y = x_ref[0] + gt_ref[0, 0] * (routed + shared_ref[0].astype(F32))
        o_ref[0] = _rms(y, fn_ref[...]) if final else y

    pl.when(step % 2 == 0)(lambda: consume(bufs[0], bufs[1]))
    pl.when(step % 2 == 1)(lambda: consume(bufs[1], bufs[0]))

    @pl.when(step == nsteps - 1)
    def _():
        wait_gathers(*bufs[nsteps % 2])


def _combine(dest, ys, wts, shared, x, mod, gt_idx, final_norm, nch, final):
    bsz, s, d = x.shape
    n_tok = bsz * s
    tm = _tile(s, 128)
    ns = s // tm
    return pl.pallas_call(
        functools.partial(_combine_kernel, n_tok=n_tok, tm=tm, nch=nch, nsteps=bsz * ns, final=final),
        grid_spec=pltpu.PrefetchScalarGridSpec(
            num_scalar_prefetch=1,
            grid=(bsz, ns),
            in_specs=[pl.BlockSpec(memory_space=pl.ANY),
                      pl.BlockSpec((tm, SUBLANES), lambda b, i, dst: (b * ns + i, 0)),
                      pl.BlockSpec((1, tm, d), lambda b, i, dst: (b, i, 0)),
                      pl.BlockSpec((1, tm, d), lambda b, i, dst: (b, i, 0)),
                      pl.BlockSpec((1, 1, 1, d), lambda b, i, dst: (gt_idx, b, 0, 0)),
                      pl.BlockSpec((1, d), lambda b, i, dst: (0, 0))],
            out_specs=pl.BlockSpec((1, tm, d), lambda b, i, dst: (b, i, 0)),
            scratch_shapes=[pltpu.VMEM((TOP_K, nch, tm, LANES), U32), pltpu.VMEM((TOP_K, nch, tm, LANES), U32),
                            pltpu.SemaphoreType.DMA, pltpu.SemaphoreType.DMA]),
        out_shape=jax.ShapeDtypeStruct((bsz, s, d), F32),
        compiler_params=_cparams(("arbitrary", "arbitrary")),
        name="moe_combine",
    )(dest, ys, wts, shared, x, mod, final_norm.reshape(1, d))


def _mla_weights(w_uq, w_ukv):
    heads = MLA_HEADS
    qr, kvr = w_uq.shape[0], w_ukv.shape[0]
    wq = w_uq.reshape(qr, heads, MLA_NOPE_DIM + MLA_ROPE_DIM)
    wq = jnp.pad(wq, ((0, 0), (0, 0), (0, MLA_QK_PAD - MLA_NOPE_DIM - MLA_ROPE_DIM)))
    wkv = w_ukv.reshape(kvr, heads, MLA_NOPE_DIM + MLA_V_DIM)
    wk = wkv[:, :, :MLA_NOPE_DIM].reshape(kvr, heads * MLA_NOPE_DIM)
    wv = wkv[:, :, MLA_NOPE_DIM:].reshape(kvr, heads * MLA_V_DIM)
    return (wq.reshape(qr, heads * MLA_QK_PAD).astype(BF16), wk.astype(BF16), wv.astype(BF16))


def _layer(x, mod, positions, tabs, l, norm_attn, w_in, diff_lambda, diff_subln, mla_q_norm, mla_w_uq,
           mla_kv_norm, mla_w_ukv, w_out, norm_ffn, router_w, router_bias, exp_w1, exp_w3, exp_w2,
           shared_w1, shared_w3, shared_w2, final_norm, final):
    bsz, s, d = x.shape
    n_tok = bsz * s
    q_rank, kv_rank = mla_w_uq.shape[0], mla_w_ukv.shape[0]
    qk_cols = 2 * DIFF_HEADS * DIFF_HEAD_DIM
    v_cols = DIFF_HEADS * 2 * DIFF_HEAD_DIM
    qkv_cols = 2 * qk_cols + v_cols
    lat_cols = q_rank + kv_rank + MLA_ROPE_DIM
    lam_init = 0.8 - 0.6 * math.exp(-0.3 * l)

    h = _norm_mod(x, norm_attn, mod, 1, 0).reshape(n_tok, d)
    w_qkv = w_in[:, :qkv_cols].astype(BF16)
    w_lat = jnp.pad(w_in[:, qkv_cols:qkv_cols + lat_cols], ((0, 0), (0, LANES - MLA_ROPE_DIM))).astype(BF16)
    w_gate = w_in[:, qkv_cols + lat_cols:].astype(BF16)
    qkv = _matmul(h, w_qkv, BF16, name="qkv_proj", scaled_cols=qk_cols,
                  col_scale=DIFF_HEAD_DIM ** -0.5 * LOG2_E)[0].reshape(bsz, s, qkv_cols)
    cq, ckv, kpe = _latent_proj(h, w_lat, q_rank, kv_rank)
    gates, exp_w3b = _matmul(h, w_gate, BF16, name="gate_proj", sigmoid=True, cast=(exp_w3,))
    gates = gates.reshape(bsz, s, 2 * d)
    o_d, exp_w2b = _diff_attention(qkv, positions, diff_lambda, diff_subln, lam_init, cast=(exp_w2,))

    wq, wk, wv = _mla_weights(mla_w_uq, mla_w_ukv)
    scale = (MLA_NOPE_DIM + MLA_ROPE_DIM) ** -0.5 * LOG2_E
    q_m = _mla_q(cq.reshape(bsz, s, q_rank), mla_q_norm, wq, tabs, scale)
    k_m, v_m = _mla_kv(ckv.reshape(bsz, s, kv_rank), mla_kv_norm, wk, wv, kpe.reshape(bsz, s, LANES), tabs)
    o_m, exp_w1b = _mla_attention(q_m, k_m, v_m, cast=(exp_w1,))

    w_o = w_out.astype(BF16)
    x = _merge(o_d, o_m, w_o[:v_cols], w_o[v_cols:], gates, x, mod, 2)

    n_exp = router_w.shape[1]
    nch = d // (2 * LANES)
    h2, hp, logits_t = _ffn_norm(x, norm_ffn, mod, 4, 3, router_w.T.astype(BF16))
    idx_t, wts_t, rank_t, sizes = _route(logits_t, router_bias)

    blk = EXPERT_ROWS
    sizes = sizes[:, 0]
    padded = (sizes + blk - 1) // blk * blk
    pad_end = jnp.cumsum(padded)
    pad_start = pad_end - padded
    onehot = idx_t[:TOP_K, :, None] == jnp.arange(n_exp, dtype=I32)
    dest = (jnp.sum(jnp.where(onehot, pad_start, 0), axis=-1) + rank_t[:TOP_K]).astype(I32).reshape(-1)
    n_blocks = -(-n_tok * TOP_K // blk) + n_exp
    block_start = jnp.arange(n_blocks, dtype=I32) * blk
    block_expert = jnp.minimum(jnp.sum(pad_end[None, :] <= block_start[:, None], axis=1), n_exp - 1).astype(I32)
    nact = (pad_end[-1:] // blk).astype(I32)
    seg = jnp.concatenate([pad_start + sizes, pad_end, nact]).astype(I32)

    xs = _dispatch(dest, seg, hp, n_blocks, blk, nch, n_exp)
    ys = _experts(block_expert, nact, xs, exp_w1b, exp_w3b, exp_w2b, nch)
    shared = _shared_ffn(h2.reshape(n_tok, d), shared_w1.astype(BF16), shared_w3.astype(BF16),
                         shared_w2.astype(BF16)).reshape(bsz, s, d)
    return _combine(dest, ys, wts_t.T, shared, x, mod, 5, final_norm, nch, final)


def kernel(x, c, positions, w_ada, b_ada, norm_attn, w_in, diff_lambda, diff_subln, mla_q_norm, mla_w_uq,
           mla_kv_norm, mla_w_ukv, w_out, norm_ffn, router_w, router_bias, exp_w1, exp_w3, exp_w2,
           shared_w1, shared_w3, shared_w2, final_norm):
    bsz, s, d = x.shape
    depth = w_ada.shape[0]
    tabs = _rope_tables(positions)
    for l in range(depth):
        mod = _ada(c, w_ada[l], b_ada[l])
        mod = mod.reshape(bsz, N_MOD, 1, d).transpose(1, 0, 2, 3)
        x = _layer(x, mod, positions, tabs, l, norm_attn[l], w_in[l], diff_lambda[l], diff_subln[l],
                   mla_q_norm[l], mla_w_uq[l], mla_kv_norm[l], mla_w_ukv[l], w_out[l], norm_ffn[l],
                   router_w[l], router_bias[l], exp_w1[l], exp_w3[l], exp_w2[l],
                   shared_w1[l], shared_w3[l], shared_w2[l], final_norm, l == depth - 1)
    return x
```

```python
import functools
import math

import jax
import jax.numpy as jnp
from jax import lax
from jax.experimental import pallas as pl
from jax.experimental.pallas import tpu as pltpu

F32 = jnp.float32
BF16 = jnp.bfloat16
U32 = jnp.uint32
I32 = jnp.int32

DIFF_HEADS = 8
DIFF_HEAD_DIM = 128
MLA_HEADS = 16
MLA_NOPE_DIM = 128
MLA_ROPE_DIM = 64
MLA_V_DIM = 128
ROPE_THETA = 10000.0
TOP_K = 6
N_GROUPS = 8
TOPK_GROUPS = 4
ROUTED_SCALE = 2.5
NORM_EPS = 1e-6
N_MOD = 6
LOG2_E = math.log2(math.e)

LANES = 128
SUBLANES = 8
MLA_QK_PAD = 256
EXPERT_ROWS = 256
DMA_ISSUE_UNROLL = 4
DMA_QUEUES = 2
ATTN_LOOKAHEAD = 2
VMEM_LIMIT = 56 * 1024 * 1024


def _cparams(sem):
    return pltpu.CompilerParams(dimension_semantics=sem, vmem_limit_bytes=VMEM_LIMIT)


def _tile(n, pref):
    t = min(n, pref)
    assert n % t == 0, (n, pref)
    return t


def _silu(a):
    return a * jax.nn.sigmoid(a)


def _ada_kernel(c_ref, w_ref, b_ref, o_ref):
    c = c_ref[...]
    a = _silu(c).astype(BF16)
    o_ref[...] = jnp.dot(a, w_ref[...].astype(BF16), preferred_element_type=F32) + b_ref[...]


def _ada(c, w, b):
    bsz, d = c.shape
    n = w.shape[1]
    tn = _tile(n, 512)
    return pl.pallas_call(
        _ada_kernel,
        grid=(n // tn,),
        in_specs=[pl.BlockSpec((bsz, d), lambda j: (0, 0)),
                  pl.BlockSpec((d, tn), lambda j: (0, j)),
                  pl.BlockSpec((1, tn), lambda j: (0, j))],
        out_specs=pl.BlockSpec((bsz, tn), lambda j: (0, j)),
        out_shape=jax.ShapeDtypeStruct((bsz, n), F32),
        compiler_params=_cparams(("arbitrary",)),
        name="ada_mod",
    )(c, w, b.reshape(1, n))


def _norm_mod_kernel(x_ref, g_ref, sc_ref, sh_ref, o_ref):
    x = x_ref[0]
    ms = jnp.mean(x * x, axis=-1, keepdims=True)
    y = x * lax.rsqrt(ms + NORM_EPS) * g_ref[...]
    o_ref[0] = (y * (1.0 + sc_ref[0, 0]) + sh_ref[0, 0]).astype(o_ref.dtype)


def _norm_mod(x, gain, mod, sc_idx, sh_idx):
    bsz, s, d = x.shape
    tm = _tile(s, 512)
    return pl.pallas_call(
        _norm_mod_kernel,
        grid=(bsz, s // tm),
        in_specs=[pl.BlockSpec((1, tm, d), lambda b, i: (b, i, 0)),
                  pl.BlockSpec((1, d), lambda b, i: (0, 0)),
                  pl.BlockSpec((1, 1, 1, d), lambda b, i: (sc_idx, b, 0, 0)),
                  pl.BlockSpec((1, 1, 1, d), lambda b, i: (sh_idx, b, 0, 0))],
        out_specs=pl.BlockSpec((1, tm, d), lambda b, i: (b, i, 0)),
        out_shape=jax.ShapeDtypeStruct((bsz, s, d), BF16),
        compiler_params=_cparams(("arbitrary", "arbitrary")),
        name="norm_mod",
    )(x, gain.reshape(1, d), mod, mod)


def _mm_kernel(a_ref, b_ref, *rest, sigmoid, scaled_tiles, col_scale, n_cast):
    cast_in, o_ref, cast_out = rest[:n_cast], rest[n_cast], rest[n_cast + 1:]
    acc = jnp.dot(a_ref[...], b_ref[...], preferred_element_type=F32)
    if sigmoid:
        acc = jax.nn.sigmoid(acc)
    if scaled_tiles:
        acc = acc * jnp.where(pl.program_id(1) < scaled_tiles, col_scale, 1.0)
    o_ref[...] = acc.astype(o_ref.dtype)
    _cast_slabs(cast_in, cast_out)


def _matmul(a, b, out_dtype, *, name, sigmoid=False, scaled_cols=0, col_scale=1.0, cast=(),
            tm_pref=1024, tn_pref=512):
    m, k = a.shape
    n = b.shape[1]
    tm, tn = _tile(m, tm_pref), _tile(n, tn_pref)
    assert scaled_cols % tn == 0
    nj = n // tn
    slabs = [_slabs(w, (m // tm) * nj) for w in cast]
    slab_specs = _slab_specs(slabs, lambda i, j: i * nj + j)
    out = pl.pallas_call(
        functools.partial(_mm_kernel, sigmoid=sigmoid, scaled_tiles=scaled_cols // tn, col_scale=col_scale,
                          n_cast=len(cast)),
        grid=(m // tm, nj),
        in_specs=[pl.BlockSpec((tm, k), lambda i, j: (i, 0)),
                  pl.BlockSpec((k, tn), lambda i, j: (0, j))] + slab_specs,
        out_specs=[pl.BlockSpec((tm, tn), lambda i, j: (i, j))] + slab_specs,
        out_shape=[jax.ShapeDtypeStruct((m, n), out_dtype)] + [jax.ShapeDtypeStruct(w.shape, BF16) for w in slabs],
        compiler_params=_cparams(("arbitrary", "arbitrary")),
        name=name,
    )(a, b, *slabs)
    return [out[0]] + [o.reshape(w.shape) for o, w in zip(out[1:], cast)]


def _latent_kernel(a_ref, b_ref, cq_ref, ckv_ref, kpe_ref, *, q_rank, kv_rank):
    acc = jnp.dot(a_ref[...], b_ref[...], preferred_element_type=F32)
    cq_ref[...] = acc[:, :q_rank].astype(cq_ref.dtype)
    ckv_ref[...] = acc[:, q_rank:q_rank + kv_rank].astype(ckv_ref.dtype)
    kpe_ref[...] = acc[:, q_rank + kv_rank:]


def _latent_proj(h, w_lat, q_rank, kv_rank):
    m, k = h.shape
    n = w_lat.shape[1]
    tm = _tile(m, 512)
    return pl.pallas_call(
        functools.partial(_latent_kernel, q_rank=q_rank, kv_rank=kv_rank),
        grid=(m // tm,),
        in_specs=[pl.BlockSpec((tm, k), lambda i: (i, 0)),
                  pl.BlockSpec((k, n), lambda i: (0, 0))],
        out_specs=[pl.BlockSpec((tm, q_rank), lambda i: (i, 0)),
                   pl.BlockSpec((tm, kv_rank), lambda i: (i, 0)),
                   pl.BlockSpec((tm, LANES), lambda i: (i, 0))],
        out_shape=[jax.ShapeDtypeStruct((m, q_rank), BF16),
                   jax.ShapeDtypeStruct((m, kv_rank), BF16),
                   jax.ShapeDtypeStruct((m, LANES), F32)],
        compiler_params=_cparams(("arbitrary",)),
        name="latent_proj",
    )(h, w_lat)


def _rope_table_kernel(pos_ref, inv_ref, c_ref, s1_ref, s2_ref):
    half = MLA_ROPE_DIM // 2
    ang = pos_ref[0].astype(F32) * inv_ref[...]
    cos, sin = jnp.cos(ang), jnp.sin(ang)
    lane = lax.broadcasted_iota(I32, ang.shape, 1)
    c_ref[0] = jnp.where(lane < 2 * half, cos, 0.0)
    s1_ref[0] = jnp.where(lane < half, -sin, 0.0)
    s2_ref[0] = jnp.where((lane >= half) & (lane < 2 * half), sin, 0.0)


def _rope_tables(positions):
    bsz, s = positions.shape
    half = MLA_ROPE_DIM // 2
    inv = ROPE_THETA ** (-(jnp.arange(LANES, dtype=F32) % half) / half)
    ts = _tile(s, 512)
    spec = pl.BlockSpec((1, ts, LANES), lambda b, i: (b, i, 0))
    shp = jax.ShapeDtypeStruct((bsz, s, LANES), F32)
    return pl.pallas_call(
        _rope_table_kernel,
        grid=(bsz, s // ts),
        in_specs=[pl.BlockSpec((1, ts, 1), lambda b, i: (b, i, 0)),
                  pl.BlockSpec((1, LANES), lambda b, i: (0, 0))],
        out_specs=[spec, spec, spec],
        out_shape=[shp, shp, shp],
        compiler_params=_cparams(("arbitrary", "arbitrary")),
        name="rope_tables",
    )(positions.reshape(bsz, s, 1), inv.reshape(1, LANES))


def _rotate(r, c, s1, s2):
    half = MLA_ROPE_DIM // 2
    return r * c + pltpu.roll(r, LANES - half, 1) * s1 + pltpu.roll(r, half, 1) * s2


def _rms(x, gain):
    ms = jnp.mean(x * x, axis=-1, keepdims=True)
    return x * lax.rsqrt(ms + NORM_EPS) * gain


def _mla_q_kernel(cq_ref, g_ref, w_ref, c_ref, s1_ref, s2_ref, o_ref, *, heads, scale):
    y = _rms(cq_ref[0].astype(F32), g_ref[...]).astype(BF16)
    q = jnp.dot(y, w_ref[...], preferred_element_type=F32)
    c, s1, s2 = c_ref[0], s1_ref[0], s2_ref[0]
    for h in range(heads):
        base = h * MLA_QK_PAD
        o_ref[0, :, base:base + LANES] = (q[:, base:base + LANES] * scale).astype(BF16)
        r = q[:, base + LANES:base + 2 * LANES]
        o_ref[0, :, base + LANES:base + 2 * LANES] = (_rotate(r, c, s1, s2) * scale).astype(BF16)


def _mla_q(cq, gain, w_q, tabs, scale):
    bsz, s, qr = cq.shape
    heads = MLA_HEADS
    n = heads * MLA_QK_PAD
    tm = _tile(s, 512)
    tab_spec = pl.BlockSpec((1, tm, LANES), lambda b, i: (b, i, 0))
    return pl.pallas_call(
        functools.partial(_mla_q_kernel, heads=heads, scale=scale),
        grid=(bsz, s // tm),
        in_specs=[pl.BlockSpec((1, tm, qr), lambda b, i: (b, i, 0)),
                  pl.BlockSpec((1, qr), lambda b, i: (0, 0)),
                  pl.BlockSpec((qr, n), lambda b, i: (0, 0)),
                  tab_spec, tab_spec, tab_spec],
        out_specs=pl.BlockSpec((1, tm, n), lambda b, i: (b, i, 0)),
        out_shape=jax.ShapeDtypeStruct((bsz, s, n), BF16),
        compiler_params=_cparams(("arbitrary", "arbitrary")),
        name="mla_q_prep",
    )(cq, gain.reshape(1, qr), w_q, *tabs)


def _mla_kv_kernel(ckv_ref, g_ref, wk_ref, wv_ref, kpe_ref, c_ref, s1_ref, s2_ref, k_ref, v_ref, *, heads):
    y = _rms(ckv_ref[0].astype(F32), g_ref[...]).astype(BF16)
    kn = jnp.dot(y, wk_ref[...], preferred_element_type=F32)
    v_ref[0] = jnp.dot(y, wv_ref[...], preferred_element_type=F32).astype(BF16)
    kr = _rotate(kpe_ref[0], c_ref[0], s1_ref[0], s2_ref[0]).astype(BF16)
    for h in range(heads):
        base = h * MLA_QK_PAD
        k_ref[0, :, base:base + LANES] = kn[:, h * LANES:(h + 1) * LANES].astype(BF16)
        k_ref[0, :, base + LANES:base + 2 * LANES] = kr


def _mla_kv(ckv, gain, w_k, w_v, kpe, tabs):
    bsz, s, kvr = ckv.shape
    heads = MLA_HEADS
    tm = _tile(s, 512)
    tab_spec = pl.BlockSpec((1, tm, LANES), lambda b, i: (b, i, 0))
    return pl.pallas_call(
        functools.partial(_mla_kv_kernel, heads=heads),
        grid=(bsz, s // tm),
        in_specs=[pl.BlockSpec((1, tm, kvr), lambda b, i: (b, i, 0)),
                  pl.BlockSpec((1, kvr), lambda b, i: (0, 0)),
                  pl.BlockSpec((kvr, heads * MLA_NOPE_DIM), lambda b, i: (0, 0)),
                  pl.BlockSpec((kvr, heads * MLA_V_DIM), lambda b, i: (0, 0)),
                  tab_spec, tab_spec, tab_spec, tab_spec],
        out_specs=[pl.BlockSpec((1, tm, heads * MLA_QK_PAD), lambda b, i: (b, i, 0)),
                   pl.BlockSpec((1, tm, heads * MLA_V_DIM), lambda b, i: (b, i, 0))],
        out_shape=[jax.ShapeDtypeStruct((bsz, s, heads * MLA_QK_PAD), BF16),
                   jax.ShapeDtypeStruct((bsz, s, heads * MLA_V_DIM), BF16)],
        compiler_params=_cparams(("arbitrary", "arbitrary")),
        name="mla_kv_prep",
    )(ckv, gain.reshape(1, kvr), w_k, w_v, kpe, *tabs)


def _qk(q, k):
    return lax.dot_general(q, k, (((1,), (1,)), ((), ())), preferred_element_type=F32)


def _causal_mask(s, qi, ki, tq, tk):
    row = qi * tq + lax.broadcasted_iota(I32, s.shape, 0)
    col = ki * tk + lax.broadcasted_iota(I32, s.shape, 1)
    return jnp.where(row >= col, s, -jnp.inf)


def _lane_chunks(x):
    return [x[:, c * LANES:(c + 1) * LANES] for c in range(x.shape[1] // LANES)]


def _softmax_update(s, v, m_prev, l_prev, acc_prev):
    chunks = _lane_chunks(s)
    cmax = functools.reduce(jnp.maximum, chunks)
    m_new = jnp.maximum(m_prev, jnp.max(cmax, axis=-1, keepdims=True))
    alpha = jnp.exp2(m_prev - m_new)
    ps = [jnp.exp2(c - m_new) for c in chunks]
    psum = functools.reduce(lambda a, b: a + b, ps)
    l_new = alpha * l_prev + jnp.sum(psum, axis=-1, keepdims=True)
    p = jnp.concatenate([c.astype(BF16) for c in ps], axis=1)
    pv = jnp.dot(p, v, preferred_element_type=F32)
    acc_new = jnp.concatenate([alpha * a for a in _lane_chunks(acc_prev)], axis=1) + pv
    return m_new, l_new, acc_new


def _emit_pipelined(score_fns, update_fns):
    n = len(score_fns)
    pending = [score_fns[i]() for i in range(min(ATTN_LOOKAHEAD, n))]
    for i in range(n):
        if i + ATTN_LOOKAHEAD < n:
            pending.append(score_fns[i + ATTN_LOOKAHEAD]())
        update_fns[i](pending[i])
        pending[i] = None


def _diag_mask(s, r):
    rows = s.shape[0]
    off = s.shape[1] - rows
    row = lax.broadcasted_iota(I32, s.shape, 0)
    col = lax.broadcasted_iota(I32, s.shape, 1)
    return jnp.where(col - off <= row, s, -jnp.inf)


def _slabs(w, steps):
    cols = w.shape[-1]
    rows = w.size // cols
    assert rows % (steps * 2 * SUBLANES) == 0, (w.shape, steps)
    return w.reshape(steps, rows // steps, cols)


def _slab_specs(slabs, step_of):
    return [pl.BlockSpec((1,) + w.shape[1:], lambda *g: (step_of(*g), 0, 0)) for w in slabs]


def _cast_slabs(src_refs, dst_refs):
    for src, dst in zip(src_refs, dst_refs):
        dst[...] = src[...].astype(BF16)


def _mla_attn_kernel(q_ref, k_ref, v_ref, *rest, tq, tk, rb, n_cast):
    cast_in, o_ref, cast_out, scr = (rest[:n_cast], rest[n_cast], rest[n_cast + 1:2 * n_cast + 1],
                                     rest[2 * n_cast + 1:])
    qi = pl.program_id(2)
    nr = tq // rb
    state = [scr[3 * r:3 * r + 3] for r in range(nr)]
    for m_scr, l_scr, acc_scr in state:
        m_scr[...] = jnp.full(m_scr.shape, -jnp.inf, F32)
        l_scr[...] = jnp.zeros(l_scr.shape, F32)
        acc_scr[...] = jnp.zeros(acc_scr.shape, F32)

    def rows(r):
        return pl.ds(r * rb, rb)

    def update(r, s, v):
        m_scr, l_scr, acc_scr = state[r]
        m, l, acc = _softmax_update(s, v, m_scr[...], l_scr[...], acc_scr[...])
        m_scr[...] = m
        l_scr[...] = l
        acc_scr[...] = acc

    def full_tile(ki, carry):
        start = pl.multiple_of(ki * tk, tk)
        k = k_ref[0, pl.ds(start, tk), :]
        v = v_ref[0, pl.ds(start, tk), :]
        _emit_pipelined([functools.partial(_qk, q_ref[0, rows(r), :], k) for r in range(nr)],
                        [functools.partial(update, r, v=v) for r in range(nr)])
        return carry
    lax.fori_loop(0, qi * (tq // tk), full_tile, 0)

    _cast_slabs(cast_in, cast_out)
    dstart = pl.multiple_of(qi * tq, tq)
    cols = [pl.ds(dstart, (r + 1) * rb) for r in range(nr)]
    _emit_pipelined(
        [lambda r=r: _diag_mask(_qk(q_ref[0, rows(r), :], k_ref[0, cols[r], :]), r) for r in range(nr)],
        [lambda s, r=r: update(r, s, v_ref[0, cols[r], :]) for r in range(nr)])

    for r, (_, l_scr, acc_scr) in enumerate(state):
        o_ref[0, rows(r), :] = (acc_scr[...] / l_scr[...]).astype(o_ref.dtype)


def _attn_state_scratch(nr, rb, dv):
    return [pltpu.VMEM((rb, w), F32) for _ in range(nr) for w in (LANES, LANES, dv)]


def _mla_attention(q, k, v, cast=()):
    bsz, s, _ = q.shape
    heads = MLA_HEADS
    t = _tile(s, 1024)
    rb = _tile(t, 256)
    nq = s // t
    slabs = [_slabs(w, bsz * heads * nq) for w in cast]
    slab_specs = _slab_specs(slabs, lambda b, h, i: (b * heads + h) * nq + i)
    out = pl.pallas_call(
        functools.partial(_mla_attn_kernel, tq=t, tk=t, rb=rb, n_cast=len(cast)),
        grid=(bsz, heads, nq),
        in_specs=[pl.BlockSpec((1, t, MLA_QK_PAD), lambda b, h, i: (b, i, h)),
                  pl.BlockSpec((1, s, MLA_QK_PAD), lambda b, h, i: (b, 0, h)),
                  pl.BlockSpec((1, s, MLA_V_DIM), lambda b, h, i: (b, 0, h))] + slab_specs,
        out_specs=[pl.BlockSpec((1, t, MLA_V_DIM), lambda b, h, i: (b, i, h))] + slab_specs,
        out_shape=[jax.ShapeDtypeStruct((bsz, s, heads * MLA_V_DIM), BF16)]
                  + [jax.ShapeDtypeStruct(w.shape, BF16) for w in slabs],
        scratch_shapes=_attn_state_scratch(t // rb, rb, MLA_V_DIM),
        compiler_params=_cparams(("arbitrary", "arbitrary", "arbitrary")),
        name="mla_attn",
    )(q, k, v, *slabs)
    return [out[0]] + [o.reshape(w.shape) for o, w in zip(out[1:], cast)]


def _diff_attn_kernel(q_ref, k_ref, v_ref, qpos_ref, kpos_ref, slope_ref, lam_ref, subln_ref, *rest,
                      tq, tk, rb, lam_init, n_cast):
    cast_in, o_ref, cast_out, scr = (rest[:n_cast], rest[n_cast], rest[n_cast + 1:2 * n_cast + 1],
                                     rest[2 * n_cast + 1:])
    qi = pl.program_id(2)
    d = DIFF_HEAD_DIM
    nr = tq // rb
    state = [[scr[6 * r + 3 * g:6 * r + 3 * g + 3] for g in range(2)] for r in range(nr)]
    for r in range(nr):
        for m_scr, l_scr, a_scr in state[r]:
            m_scr[...] = jnp.full(m_scr.shape, -jnp.inf, F32)
            l_scr[...] = jnp.zeros(l_scr.shape, F32)
            a_scr[...] = jnp.zeros(a_scr.shape, F32)
    slope = slope_ref[0, :, 0:1]

    def rows(r):
        return pl.ds(r * rb, rb)

    def kpos(first, count):
        return jnp.concatenate([kpos_ref[0, first + j] for j in range(count)], axis=1)

    chains = [(r, g) for r in range(nr) for g in range(2)]

    def emit(k_of, v_of, kp_of, masked):
        bias = {}

        def score(r, g):
            if g == 0:
                bias[r] = slope * jnp.abs(qpos_ref[0, rows(r), :] - kp_of(r))
            s = _qk(q_ref[0, rows(r), g * d:(g + 1) * d], k_of(r)[:, g * d:(g + 1) * d]) - bias[r]
            return _diag_mask(s, r) if masked else s

        def update(r, g, s):
            m_scr, l_scr, a_scr = state[r][g]
            m, l, acc = _softmax_update(s, v_of(r), m_scr[...], l_scr[...], a_scr[...])
            m_scr[...] = m
            l_scr[...] = l
            a_scr[...] = acc

        _emit_pipelined([functools.partial(score, r, g) for r, g in chains],
                        [functools.partial(update, r, g) for r, g in chains])

    def full_tile(ki, carry):
        start = pl.multiple_of(ki * tk, tk)
        k = k_ref[0, pl.ds(start, tk), :]
        v = v_ref[0, pl.ds(start, tk), :]
        kp = kpos(ki * (tk // rb), tk // rb)
        emit(lambda r: k, lambda r: v, lambda r: kp, False)
        return carry
    lax.fori_loop(0, qi * (tq // tk), full_tile, 0)

    _cast_slabs(cast_in, cast_out)
    dstart = pl.multiple_of(qi * tq, tq)
    cols = [pl.ds(dstart, (r + 1) * rb) for r in range(nr)]
    emit(lambda r: k_ref[0, cols[r], :], lambda r: v_ref[0, cols[r], :], lambda r: kpos(qi * nr, r + 1), True)

    lp = lam_ref[...]
    e1 = jnp.exp(jnp.sum(lp[0:1] * lp[1:2], axis=-1, keepdims=True))
    e2 = jnp.exp(jnp.sum(lp[2:3] * lp[3:4], axis=-1, keepdims=True))
    lam = e1 - e2 + lam_init
    for r in range(nr):
        (_, l1, a1), (_, l2, a2) = state[r]
        o1 = [a / l1[...] for a in _lane_chunks(a1[...])]
        o2 = [a / l2[...] for a in _lane_chunks(a2[...])]
        o = jnp.concatenate([x - lam * y for x, y in zip(o1, o2)], axis=1)
        o_ref[0, rows(r), :] = (_rms(o, subln_ref[...]) * (1.0 - lam_init)).astype(o_ref.dtype)


def _diff_attention(qkv, positions, diff_lambda, subln, lam_init, cast=()):
    bsz, s, _ = qkv.shape
    heads = DIFF_HEADS
    dv = 2 * DIFF_HEAD_DIM
    t = _tile(s, 1024)
    tk = _tile(t, 512)
    rb = _tile(tk, 256)
    nq = s // t
    slopes = 2.0 ** (-8.0 * jnp.arange(1, heads + 1, dtype=F32) / heads) * LOG2_E
    slopes = jnp.broadcast_to(slopes[:, None, None], (heads, 1, LANES))
    posf = positions.astype(F32)
    slabs = [_slabs(w, bsz * heads * nq) for w in cast]
    slab_specs = _slab_specs(slabs, lambda b, h, i: (b * heads + h) * nq + i)
    out = pl.pallas_call(
        functools.partial(_diff_attn_kernel, tq=t, tk=tk, rb=rb, lam_init=lam_init, n_cast=len(cast)),
        grid=(bsz, heads, nq),
        in_specs=[pl.BlockSpec((1, t, dv), lambda b, h, i: (b, i, h)),
                  pl.BlockSpec((1, s, dv), lambda b, h, i: (b, 0, heads + h)),
                  pl.BlockSpec((1, s, dv), lambda b, h, i: (b, 0, 2 * heads + h)),
                  pl.BlockSpec((1, t, 1), lambda b, h, i: (b, i, 0)),
                  pl.BlockSpec((1, s // rb, 1, rb), lambda b, h, i: (b, 0, 0, 0)),
                  pl.BlockSpec((1, 1, LANES), lambda b, h, i: (h, 0, 0)),
                  pl.BlockSpec((4, DIFF_HEAD_DIM), lambda b, h, i: (0, 0)),
                  pl.BlockSpec((1, dv), lambda b, h, i: (0, 0))] + slab_specs,
        out_specs=[pl.BlockSpec((1, t, dv), lambda b, h, i: (b, i, h))] + slab_specs,
        out_shape=[jax.ShapeDtypeStruct((bsz, s, heads * dv), BF16)]
                  + [jax.ShapeDtypeStruct(w.shape, BF16) for w in slabs],
        scratch_shapes=_attn_state_scratch(2 * (t // rb), rb, dv),
        compiler_params=_cparams(("arbitrary", "arbitrary", "arbitrary")),
        name="diff_attn",
    )(qkv, qkv, qkv, posf.reshape(bsz, s, 1), posf.reshape(bsz, s // rb, 1, rb), slopes, diff_lambda,
      subln.reshape(1, dv), *slabs)
    return [out[0]] + [o.reshape(w.shape) for o, w in zip(out[1:], cast)]


def _merge_kernel(od_ref, om_ref, wd_ref, wm_ref, g0_ref, g1_ref, x_ref, gt_ref, o_ref):
    yd = jnp.dot(od_ref[0], wd_ref[...], preferred_element_type=F32)
    ym = jnp.dot(om_ref[0], wm_ref[...], preferred_element_type=F32)
    y = g0_ref[0].astype(F32) * yd + g1_ref[0].astype(F32) * ym
    o_ref[0] = x_ref[0] + gt_ref[0, 0] * y


def _merge(o_d, o_m, w_d, w_m, gates, x, mod, gt_idx):
    bsz, s, d = x.shape
    kd, km = o_d.shape[-1], o_m.shape[-1]
    tm, tn = _tile(s, 1024), _tile(d, 512)
    nj = d // tn
    return pl.pallas_call(
        _merge_kernel,
        grid=(bsz, s // tm, nj),
        in_specs=[pl.BlockSpec((1, tm, kd), lambda b, i, j: (b, i, 0)),
                  pl.BlockSpec((1, tm, km), lambda b, i, j: (b, i, 0)),
                  pl.BlockSpec((kd, tn), lambda b, i, j: (0, j)),
                  pl.BlockSpec((km, tn), lambda b, i, j: (0, j)),
                  pl.BlockSpec((1, tm, tn), lambda b, i, j: (b, i, j)),
                  pl.BlockSpec((1, tm, tn), lambda b, i, j: (b, i, j + nj)),
                  pl.BlockSpec((1, tm, tn), lambda b, i, j: (b, i, j)),
                  pl.BlockSpec((1, 1, 1, tn), lambda b, i, j: (gt_idx, b, 0, j))],
        out_specs=pl.BlockSpec((1, tm, tn), lambda b, i, j: (b, i, j)),
        out_shape=jax.ShapeDtypeStruct((bsz, s, d), F32),
        compiler_params=_cparams(("arbitrary", "arbitrary", "arbitrary")),
        name="out_merge",
    )(o_d, o_m, w_d, w_m, gates, gates, x, mod)


def _pack_rows(y):
    half = y.shape[1] // 2
    bits = lax.bitcast_convert_type(y.astype(BF16).astype(F32), U32)
    return (bits[:, half:] & jnp.uint32(0xFFFF0000)) | (bits[:, :half] >> 16)


def _unpack_rows(w):
    lo = lax.bitcast_convert_type(w << 16, F32)
    hi = lax.bitcast_convert_type(w & jnp.uint32(0xFFFF0000), F32)
    return lo, hi


def _ffn_norm_kernel(x_ref, g_ref, sc_ref, sh_ref, rw_ref, h_ref, hp_ref, lt_ref, *, nch):
    x = x_ref[0]
    y = _rms(x, g_ref[...]) * (1.0 + sc_ref[0, 0]) + sh_ref[0, 0]
    hb = y.astype(BF16)
    h_ref[0] = hb
    lt_ref[...] = lax.dot_general(rw_ref[...], hb, (((1,), (1,)), ((), ())), preferred_element_type=F32)
    packed = _pack_rows(y)
    for c in range(nch):
        hp_ref[pl.ds(c, x.shape[0], stride=nch), :] = packed[:, c * LANES:(c + 1) * LANES]


def _ffn_norm(x, gain, mod, sc_idx, sh_idx, router_wt):
    bsz, s, d = x.shape
    e = router_wt.shape[0]
    nch = d // (2 * LANES)
    tm = _tile(s, 256)
    ns = s // tm
    return pl.pallas_call(
        functools.partial(_ffn_norm_kernel, nch=nch),
        grid=(bsz, ns),
        in_specs=[pl.BlockSpec((1, tm, d), lambda b, i: (b, i, 0)),
                  pl.BlockSpec((1, d), lambda b, i: (0, 0)),
                  pl.BlockSpec((1, 1, 1, d), lambda b, i: (sc_idx, b, 0, 0)),
                  pl.BlockSpec((1, 1, 1, d), lambda b, i: (sh_idx, b, 0, 0)),
                  pl.BlockSpec((e, d), lambda b, i: (0, 0))],
        out_specs=[pl.BlockSpec((1, tm, d), lambda b, i: (b, i, 0)),
                   pl.BlockSpec((tm * nch, LANES), lambda b, i: (b * ns + i, 0)),
                   pl.BlockSpec((e, tm), lambda b, i: (0, b * ns + i))],
        out_shape=[jax.ShapeDtypeStruct((bsz, s, d), BF16),
                   jax.ShapeDtypeStruct((bsz * s * nch, LANES), U32),
                   jax.ShapeDtypeStruct((e, bsz * s), F32)],
        compiler_params=_cparams(("arbitrary", "arbitrary")),
        name="ffn_norm_router",
    )(x, gain.reshape(1, d), mod, mod, router_wt)


def _first_index(hit, iota, axis, size):
    return jnp.min(jnp.where(hit, iota, size), axis=axis, keepdims=True)


def _route_kernel(lt_ref, bias_ref, idx_ref, wt_ref, rank_ref, sizes_ref, cnt_scr, *, n_exp, tn):
    i = pl.program_id(0)
    gsz = n_exp // N_GROUPS

    @pl.when(i == 0)
    def _():
        cnt_scr[...] = jnp.zeros(cnt_scr.shape, F32)

    scores = jax.nn.sigmoid(lt_ref[...])
    sel = scores + bias_ref[...]
    sel3 = sel.reshape(N_GROUPS, gsz, tn)
    j3 = lax.broadcasted_iota(I32, sel3.shape, 1)
    top1 = jnp.max(sel3, axis=1, keepdims=True)
    first = _first_index(sel3 == top1, j3, 1, gsz)
    top2 = jnp.max(jnp.where(j3 == first, -jnp.inf, sel3), axis=1, keepdims=True)
    gscore = (top1 + top2).reshape(N_GROUPS, tn)

    giota = lax.broadcasted_iota(I32, gscore.shape, 0)
    gmask = jnp.zeros(gscore.shape, jnp.bool_)
    for _ in range(TOPK_GROUPS):
        best = jnp.max(gscore, axis=0, keepdims=True)
        gi = _first_index(gscore == best, giota, 0, N_GROUPS)
        hit = giota == gi
        gmask = gmask | hit
        gscore = jnp.where(hit, -jnp.inf, gscore)

    emask = jnp.broadcast_to(gmask.reshape(N_GROUPS, 1, tn), sel3.shape)
    cand = jnp.where(emask, sel3, -jnp.inf).reshape(n_exp, tn)
    eiota = lax.broadcasted_iota(I32, cand.shape, 0)
    hits, idxs, vals = [], [], []
    for _ in range(TOP_K):
        best = jnp.max(cand, axis=0, keepdims=True)
        ei = _first_index(cand == best, eiota, 0, n_exp)
        hit = eiota == ei
        hits.append(hit)
        idxs.append(ei)
        vals.append(jnp.sum(jnp.where(hit, scores, 0.0), axis=0, keepdims=True))
        cand = jnp.where(hit, -jnp.inf, cand)
    total = functools.reduce(lambda a, b: a + b, vals)

    chosen = functools.reduce(lambda a, b: a | b, hits)
    onehot = jnp.where(chosen, 1.0, 0.0)
    r = lax.broadcasted_iota(I32, (tn, tn), 0)
    c = lax.broadcasted_iota(I32, (tn, tn), 1)
    upper = jnp.where(r < c, 1.0, 0.0).astype(BF16)
    before = cnt_scr[...] + jnp.dot(onehot.astype(BF16), upper, preferred_element_type=F32)
    cnt_new = cnt_scr[...] + jnp.sum(onehot, axis=1, keepdims=True)
    cnt_scr[...] = cnt_new

    pad = SUBLANES - TOP_K
    ranks = [jnp.sum(jnp.where(h, before, 0.0), axis=0, keepdims=True).astype(I32) for h in hits]
    zi = [jnp.zeros((pad, tn), I32)]
    idx_ref[...] = jnp.concatenate(idxs + zi, axis=0)
    rank_ref[...] = jnp.concatenate(ranks + zi, axis=0)
    wt_ref[...] = jnp.concatenate([v / total * ROUTED_SCALE for v in vals] + [jnp.zeros((pad, tn), F32)], axis=0)
    sizes_ref[...] = jnp.broadcast_to(cnt_new, sizes_ref.shape).astype(I32)


def _route(logits_t, bias):
    n_exp, n = logits_t.shape
    tn = _tile(n, 512)
    row_spec = pl.BlockSpec((SUBLANES, tn), lambda i: (0, i))
    return pl.pallas_call(
        functools.partial(_route_kernel, n_exp=n_exp, tn=tn),
        grid=(n // tn,),
        in_specs=[pl.BlockSpec((n_exp, tn), lambda i: (0, i)),
                  pl.BlockSpec((n_exp, 1), lambda i: (0, 0))],
        out_specs=[row_spec, row_spec, row_spec, pl.BlockSpec((n_exp, LANES), lambda i: (0, 0))],
        out_shape=[jax.ShapeDtypeStruct((SUBLANES, n), I32),
                   jax.ShapeDtypeStruct((SUBLANES, n), F32),
                   jax.ShapeDtypeStruct((SUBLANES, n), I32),
                   jax.ShapeDtypeStruct((n_exp, LANES), I32)],
        scratch_shapes=[pltpu.VMEM((n_exp, 1), F32)],
        compiler_params=_cparams(("arbitrary",)),
        name="route_topk",
    )(logits_t, bias.reshape(n_exp, 1))


def _expert_kernel(be_ref, nact_ref, tok_ref, hp_ref, w1_ref, w3_ref, w2_ref, ys_ref, xbuf0, xbuf1, sem0, sem1,
                   *, tm, nch):
    i = pl.program_id(0)
    nact = nact_ref[0]
    bufs = ((xbuf0, sem0), (xbuf1, sem1))

    def row_copy(buf, sem, tok, j):
        return pltpu.make_async_copy(hp_ref.at[pl.ds(pl.multiple_of(tok * nch, nch), nch)], buf.at[:, j], sem)

    def start_gathers(block, buf, sem, inline):
        base = block * tm

        def start_pair(jj, c):
            for q in range(DMA_QUEUES):
                j = jj * DMA_QUEUES + q
                row_copy(buf, sem, tok_ref[base + j], j).start(priority=q)
            return c
        if inline:
            for jj in range(tm // DMA_QUEUES):
                start_pair(jj, 0)
        else:
            lax.fori_loop(0, tm // DMA_QUEUES, start_pair, 0, unroll=DMA_ISSUE_UNROLL)

    def wait_gathers(buf, sem):
        for _ in range(tm):
            row_copy(buf, sem, 0, 0).wait()

    @pl.when(i == 0)
    def _():
        start_gathers(0, *bufs[0], inline=False)

    def compute(cur, nxt):
        wait_gathers(*cur)
        start_gathers(jnp.minimum(i + 1, nact - 1), *nxt, inline=True)
        xbuf = cur[0]
        halves = [_unpack_rows(xbuf[c]) for c in range(nch)]
        x = jnp.concatenate([lo.astype(BF16) for lo, _ in halves] + [hi.astype(BF16) for _, hi in halves], axis=1)
        a = jnp.dot(x, w1_ref[0], preferred_element_type=F32)
        b = jnp.dot(x, w3_ref[0], preferred_element_type=F32)
        y = jnp.dot((_silu(a) * b).astype(BF16), w2_ref[0], preferred_element_type=F32)
        packed = _pack_rows(y)
        for c in range(nch):
            ys_ref[pl.ds(c, tm, stride=nch), :] = packed[:, c * LANES:(c + 1) * LANES]

        @pl.when(i == nact - 1)
        def _():
            wait_gathers(*nxt)

    active = i < nact
    pl.when(jnp.logical_and(active, i % 2 == 0))(lambda: compute(bufs[0], bufs[1]))
    pl.when(jnp.logical_and(active, i % 2 == 1))(lambda: compute(bufs[1], bufs[0]))

    @pl.when(jnp.logical_not(active))
    def _():
        ys_ref[...] = jnp.zeros(ys_ref.shape, U32)


def _experts(block_expert, nact, row_tok, hp, w1, w3, w2, nch):
    n_exp, d, f = w1.shape
    tm = EXPERT_ROWS
    nb = row_tok.shape[0] // tm
    wsel = lambda i, be, na, tk: (be[jnp.minimum(i, na[0] - 1)], 0, 0)
    return pl.pallas_call(
        functools.partial(_expert_kernel, tm=tm, nch=nch),
        grid_spec=pltpu.PrefetchScalarGridSpec(
            num_scalar_prefetch=3,
            grid=(nb,),
            in_specs=[pl.BlockSpec(memory_space=pl.ANY),
                      pl.BlockSpec((1, d, f), wsel),
                      pl.BlockSpec((1, d, f), wsel),
                      pl.BlockSpec((1, f, d), wsel)],
            out_specs=pl.BlockSpec((tm * nch, LANES), lambda i, be, na, tk: (i, 0)),
            scratch_shapes=[pltpu.VMEM((nch, tm, LANES), U32), pltpu.VMEM((nch, tm, LANES), U32),
                            pltpu.SemaphoreType.DMA, pltpu.SemaphoreType.DMA]),
        out_shape=jax.ShapeDtypeStruct((nb * tm * nch, LANES), U32),
        compiler_params=_cparams(("arbitrary",)),
        name="moe_experts",
    )(block_expert, nact, row_tok, hp, w1, w3, w2)


def _shared_ffn_kernel(h_ref, w1_ref, w3_ref, w2_ref, o_ref, *, out_chunk):
    h = h_ref[...]
    a = jnp.dot(h, w1_ref[...], preferred_element_type=F32)
    g = jnp.dot(h, w3_ref[...], preferred_element_type=F32)
    mid = (_silu(a) * g).astype(BF16)
    for c in range(0, o_ref.shape[1], out_chunk):
        o_ref[:, c:c + out_chunk] = jnp.dot(mid, w2_ref[:, c:c + out_chunk],
                                            preferred_element_type=F32).astype(o_ref.dtype)


def _shared_ffn(h, w1, w3, w2):
    n, d = h.shape
    f = w1.shape[1]
    tm = _tile(n, 512)
    return pl.pallas_call(
        functools.partial(_shared_ffn_kernel, out_chunk=_tile(d, 1024)),
        grid=(n // tm,),
        in_specs=[pl.BlockSpec((tm, d), lambda i: (i, 0)),
                  pl.BlockSpec((d, f), lambda i: (0, 0)),
                  pl.BlockSpec((d, f), lambda i: (0, 0)),
                  pl.BlockSpec((f, d), lambda i: (0, 0))],
        out_specs=pl.BlockSpec((tm, d), lambda i: (i, 0)),
        out_shape=jax.ShapeDtypeStruct((n, d), BF16),
        compiler_params=_cparams(("arbitrary",)),
        name="shared_ffn",
    )(h, w1, w3, w2)


def _combine_kernel(dest_ref, ys_ref, wt_ref, shared_ref, x_ref, gt_ref, fn_ref, o_ref,
                    ybuf0, ybuf1, sem0, sem1, *, n_tok, tm, nch, nsteps, final):
    step = pl.program_id(0) * pl.num_programs(1) + pl.program_id(1)
    bufs = ((ybuf0, sem0), (ybuf1, sem1))

    def row_copy(buf, sem, slot, k, j):
        return pltpu.make_async_copy(ys_ref.at[pl.ds(pl.multiple_of(slot * nch, nch), nch)],
                                     buf.at[k, :, j], sem)

    def start_gathers(tile, buf, sem, inline=False):
        base = tile * tm

        def start(j, c):
            for k in range(TOP_K):
                row_copy(buf, sem, dest_ref[k * n_tok + base + j], k, j).start(priority=k % DMA_QUEUES)
            return c
        if inline:
            for j in range(tm):
                start(j, 0)
        else:
            lax.fori_loop(0, tm, start, 0, unroll=DMA_ISSUE_UNROLL)

    def wait_gathers(buf, sem):
        for _ in range(tm * TOP_K):
            row_copy(buf, sem, 0, 0, 0).wait()

    @pl.when(step == 0)
    def _():
        start_gathers(0, *bufs[0])

    def consume(cur, nxt):
        wait_gathers(*cur)
        start_gathers(jnp.minimum(step + 1, nsteps - 1), *nxt, inline=True)
        ybuf = cur[0]
        wts = wt_ref[...]
        los = [None] * nch
        his = [None] * nch
        for k in range(TOP_K):
            wk = jnp.broadcast_to(wts[:, k:k + 1], (tm, LANES))
            for c in range(nch):
                lo, hi = _unpack_rows(ybuf[k, c])
                los[c] = wk * lo if k == 0 else los[c] + wk * lo
                his[c] = wk * hi if k == 0 else his[c] + wk * hi
        routed = jnp.concatenate(los + his, axis=1)
        y = x_ref[0] + gt_ref[0, 0] * (routed + shared_ref[0].astype(F32))
        o_ref[0] = _rms(y, fn_ref[...]) if final else y

    pl.when(step % 2 == 0)(lambda: consume(bufs[0], bufs[1]))
    pl.when(step % 2 == 1)(lambda: consume(bufs[1], bufs[0]))

    @pl.when(step == nsteps - 1)
    def _():
        wait_gathers(*bufs[nsteps % 2])


def _combine(dest, ys, wts, shared, x, mod, gt_idx, final_norm, nch, final):
    bsz, s, d = x.shape
    n_tok = bsz * s
    tm = _tile(s, 128)
    ns = s // tm
    return pl.pallas_call(
        functools.partial(_combine_kernel, n_tok=n_tok, tm=tm, nch=nch, nsteps=bsz * ns, final=final),
        grid_spec=pltpu.PrefetchScalarGridSpec(
            num_scalar_prefetch=1,
            grid=(bsz, ns),
            in_specs=[pl.BlockSpec(memory_space=pl.ANY),
                      pl.BlockSpec((tm, SUBLANES), lambda b, i, dst: (b * ns + i, 0)),
                      pl.BlockSpec((1, tm, d), lambda b, i, dst: (b, i, 0)),
                      pl.BlockSpec((1, tm, d), lambda b, i, dst: (b, i, 0)),
                      pl.BlockSpec((1, 1, 1, d), lambda b, i, dst: (gt_idx, b, 0, 0)),
                      pl.BlockSpec((1, d), lambda b, i, dst: (0, 0))],
            out_specs=pl.BlockSpec((1, tm, d), lambda b, i, dst: (b, i, 0)),
            scratch_shapes=[pltpu.VMEM((TOP_K, nch, tm, LANES), U32), pltpu.VMEM((TOP_K, nch, tm, LANES), U32),
                            pltpu.SemaphoreType.DMA, pltpu.SemaphoreType.DMA]),
        out_shape=jax.ShapeDtypeStruct((bsz, s, d), F32),
        compiler_params=_cparams(("arbitrary", "arbitrary")),
        name="moe_combine",
    )(dest, ys, wts, shared, x, mod, final_norm.reshape(1, d))


def _mla_weights(w_uq, w_ukv):
    heads = MLA_HEADS
    qr, kvr = w_uq.shape[0], w_ukv.shape[0]
    wq = w_uq.reshape(qr, heads, MLA_NOPE_DIM + MLA_ROPE_DIM)
    wq = jnp.pad(wq, ((0, 0), (0, 0), (0, MLA_QK_PAD - MLA_NOPE_DIM - MLA_ROPE_DIM)))
    wkv = w_ukv.reshape(kvr, heads, MLA_NOPE_DIM + MLA_V_DIM)
    wk = wkv[:, :, :MLA_NOPE_DIM].reshape(kvr, heads * MLA_NOPE_DIM)
    wv = wkv[:, :, MLA_NOPE_DIM:].reshape(kvr, heads * MLA_V_DIM)
    return (wq.reshape(qr, heads * MLA_QK_PAD).astype(BF16), wk.astype(BF16), wv.astype(BF16))


def _layer(x, mod, positions, tabs, l, norm_attn, w_in, diff_lambda, diff_subln, mla_q_norm, mla_w_uq,
           mla_kv_norm, mla_w_ukv, w_out, norm_ffn, router_w, router_bias, exp_w1, exp_w3, exp_w2,
           shared_w1, shared_w3, shared_w2, final_norm, final):
    bsz, s, d = x.shape
    n_tok = bsz * s
    q_rank, kv_rank = mla_w_uq.shape[0], mla_w_ukv.shape[0]
    qk_cols = 2 * DIFF_HEADS * DIFF_HEAD_DIM
    v_cols = DIFF_HEADS * 2 * DIFF_HEAD_DIM
    qkv_cols = 2 * qk_cols + v_cols
    lat_cols = q_rank + kv_rank + MLA_ROPE_DIM
    lam_init = 0.8 - 0.6 * math.exp(-0.3 * l)

    h = _norm_mod(x, norm_attn, mod, 1, 0).reshape(n_tok, d)
    w_qkv = w_in[:, :qkv_cols].astype(BF16)
    w_lat = jnp.pad(w_in[:, qkv_cols:qkv_cols + lat_cols], ((0, 0), (0, LANES - MLA_ROPE_DIM))).astype(BF16)
    w_gate = w_in[:, qkv_cols + lat_cols:].astype(BF16)
    qkv = _matmul(h, w_qkv, BF16, name="qkv_proj", scaled_cols=qk_cols,
                  col_scale=DIFF_HEAD_DIM ** -0.5 * LOG2_E)[0].reshape(bsz, s, qkv_cols)
    cq, ckv, kpe = _latent_proj(h, w_lat, q_rank, kv_rank)
    gates, exp_w3b = _matmul(h, w_gate, BF16, name="gate_proj", sigmoid=True, cast=(exp_w3,))
    gates = gates.reshape(bsz, s, 2 * d)
    o_d, exp_w2b = _diff_attention(qkv, positions, diff_lambda, diff_subln, lam_init, cast=(exp_w2,))

    wq, wk, wv = _mla_weights(mla_w_uq, mla_w_ukv)
    scale = (MLA_NOPE_DIM + MLA_ROPE_DIM) ** -0.5 * LOG2_E
    q_m = _mla_q(cq.reshape(bsz, s, q_rank), mla_q_norm, wq, tabs, scale)
    k_m, v_m = _mla_kv(ckv.reshape(bsz, s, kv_rank), mla_kv_norm, wk, wv, kpe.reshape(bsz, s, LANES), tabs)
    o_m, exp_w1b = _mla_attention(q_m, k_m, v_m, cast=(exp_w1,))

    w_o = w_out.astype(BF16)
    x = _merge(o_d, o_m, w_o[:v_cols], w_o[v_cols:], gates, x, mod, 2)

    n_exp = router_w.shape[1]
    nch = d // (2 * LANES)
    h2, hp, logits_t = _ffn_norm(x, norm_ffn, mod, 4, 3, router_w.T.astype(BF16))
    idx_t, wts_t, rank_t, sizes = _route(logits_t, router_bias)

    blk = EXPERT_ROWS
    sizes = sizes[:, 0]
    padded = (sizes + blk - 1) // blk * blk
    pad_end = jnp.cumsum(padded)
    pad_start = pad_end - padded
    onehot = idx_t[:TOP_K, :, None] == jnp.arange(n_exp, dtype=I32)
    dest = (jnp.sum(jnp.where(onehot, pad_start, 0), axis=-1) + rank_t[:TOP_K]).astype(I32).reshape(-1)
    n_blocks = -(-n_tok * TOP_K // blk) + n_exp
    block_start = jnp.arange(n_blocks, dtype=I32) * blk
    block_expert = jnp.minimum(jnp.sum(pad_end[None, :] <= block_start[:, None], axis=1), n_exp - 1).astype(I32)
    nact = (pad_end[-1:] // blk).astype(I32)
    tok = jnp.tile(jnp.arange(n_tok, dtype=I32), TOP_K)
    row_tok = jnp.zeros((n_blocks * blk,), I32).at[dest].set(tok, unique_indices=True)

    ys = _experts(block_expert, nact, row_tok, hp, exp_w1b, exp_w3b, exp_w2b, nch)
    shared = _shared_ffn(h2.reshape(n_tok, d), shared_w1.astype(BF16), shared_w3.astype(BF16),
                         shared_w2.astype(BF16)).reshape(bsz, s, d)
    return _combine(dest, ys, wts_t.T, shared, x, mod, 5, final_norm, nch, final)


def kernel(x, c, positions, w_ada, b_ada, norm_attn, w_in, diff_lambda, diff_subln, mla_q_norm, mla_w_uq,
           mla_kv_norm, mla_w_ukv, w_out, norm_ffn, router_w, router_bias, exp_w1, exp_w3, exp_w2,
           shared_w1, shared_w3, shared_w2, final_norm):
    bsz, s, d = x.shape
    depth = w_ada.shape[0]
    tabs = _rope_tables(positions)
    for l in range(depth):
        mod = _ada(c, w_ada[l], b_ada[l])
        mod = mod.reshape(bsz, N_MOD, 1, d).transpose(1, 0, 2, 3)
        x = _layer(x, mod, positions, tabs, l, norm_attn[l], w_in[l], diff_lambda[l], diff_subln[l],
                   mla_q_norm[l], mla_w_uq[l], mla_kv_norm[l], mla_w_ukv[l], w_out[l], norm_ffn[l],
                   router_w[l], router_bias[l], exp_w1[l], exp_w3[l], exp_w2[l],
                   shared_w1[l], shared_w3[l], shared_w2[l], final_norm, l == depth - 1)
    return x
```

```python
import functools
import math

import jax
import jax.numpy as jnp
from jax import lax
from jax.experimental import pallas as pl
from jax.experimental.pallas import tpu as pltpu

F32 = jnp.float32
BF16 = jnp.bfloat16
U32 = jnp.uint32
I32 = jnp.int32

DIFF_HEADS = 8
DIFF_HEAD_DIM = 128
MLA_HEADS = 16
MLA_NOPE_DIM = 128
MLA_ROPE_DIM = 64
MLA_V_DIM = 128
ROPE_THETA = 10000.0
TOP_K = 6
N_GROUPS = 8
TOPK_GROUPS = 4
ROUTED_SCALE = 2.5
NORM_EPS = 1e-6
N_MOD = 6
LOG2_E = math.log2(math.e)

LANES = 128
SUBLANES = 8
MLA_QK_PAD = 256
EXPERT_ROWS = 256
DMA_ISSUE_UNROLL = 4
DMA_QUEUES = 2
ATTN_LOOKAHEAD = 2
VMEM_LIMIT = 56 * 1024 * 1024


def _cparams(sem):
    return pltpu.CompilerParams(dimension_semantics=sem, vmem_limit_bytes=VMEM_LIMIT)


def _tile(n, pref):
    t = min(n, pref)
    assert n % t == 0, (n, pref)
    return t


def _silu(a):
    return a * jax.nn.sigmoid(a)


def _ada_kernel(c_ref, w_ref, b_ref, o_ref):
    c = c_ref[...]
    a = _silu(c).astype(BF16)
    o_ref[...] = jnp.dot(a, w_ref[...].astype(BF16), preferred_element_type=F32) + b_ref[...]


def _ada(c, w, b):
    bsz, d = c.shape
    n = w.shape[1]
    tn = _tile(n, 512)
    return pl.pallas_call(
        _ada_kernel,
        grid=(n // tn,),
        in_specs=[pl.BlockSpec((bsz, d), lambda j: (0, 0)),
                  pl.BlockSpec((d, tn), lambda j: (0, j)),
                  pl.BlockSpec((1, tn), lambda j: (0, j))],
        out_specs=pl.BlockSpec((bsz, tn), lambda j: (0, j)),
        out_shape=jax.ShapeDtypeStruct((bsz, n), F32),
        compiler_params=_cparams(("arbitrary",)),
        name="ada_mod",
    )(c, w, b.reshape(1, n))


def _norm_mod_kernel(x_ref, g_ref, sc_ref, sh_ref, o_ref):
    x = x_ref[0]
    ms = jnp.mean(x * x, axis=-1, keepdims=True)
    y = x * lax.rsqrt(ms + NORM_EPS) * g_ref[...]
    o_ref[0] = (y * (1.0 + sc_ref[0, 0]) + sh_ref[0, 0]).astype(o_ref.dtype)


def _norm_mod(x, gain, mod, sc_idx, sh_idx):
    bsz, s, d = x.shape
    tm = _tile(s, 512)
    return pl.pallas_call(
        _norm_mod_kernel,
        grid=(bsz, s // tm),
        in_specs=[pl.BlockSpec((1, tm, d), lambda b, i: (b, i, 0)),
                  pl.BlockSpec((1, d), lambda b, i: (0, 0)),
                  pl.BlockSpec((1, 1, 1, d), lambda b, i: (sc_idx, b, 0, 0)),
                  pl.BlockSpec((1, 1, 1, d), lambda b, i: (sh_idx, b, 0, 0))],
        out_specs=pl.BlockSpec((1, tm, d), lambda b, i: (b, i, 0)),
        out_shape=jax.ShapeDtypeStruct((bsz, s, d), BF16),
        compiler_params=_cparams(("arbitrary", "arbitrary")),
        name="norm_mod",
    )(x, gain.reshape(1, d), mod, mod)


def _mm_kernel(a_ref, b_ref, *rest, sigmoid, scaled_tiles, col_scale, n_cast):
    cast_in, o_ref, cast_out = rest[:n_cast], rest[n_cast], rest[n_cast + 1:]
    acc = jnp.dot(a_ref[...], b_ref[...], preferred_element_type=F32)
    if sigmoid:
        acc = jax.nn.sigmoid(acc)
    if scaled_tiles:
        acc = acc * jnp.where(pl.program_id(1) < scaled_tiles, col_scale, 1.0)
    o_ref[...] = acc.astype(o_ref.dtype)
    _cast_slabs(cast_in, cast_out)


def _matmul(a, b, out_dtype, *, name, sigmoid=False, scaled_cols=0, col_scale=1.0, cast=(),
            tm_pref=1024, tn_pref=512):
    m, k = a.shape
    n = b.shape[1]
    tm, tn = _tile(m, tm_pref), _tile(n, tn_pref)
    assert scaled_cols % tn == 0
    nj = n // tn
    slabs = [_slabs(w, (m // tm) * nj) for w in cast]
    slab_specs = _slab_specs(slabs, lambda i, j: i * nj + j)
    out = pl.pallas_call(
        functools.partial(_mm_kernel, sigmoid=sigmoid, scaled_tiles=scaled_cols // tn, col_scale=col_scale,
                          n_cast=len(cast)),
        grid=(m // tm, nj),
        in_specs=[pl.BlockSpec((tm, k), lambda i, j: (i, 0)),
                  pl.BlockSpec((k, tn), lambda i, j: (0, j))] + slab_specs,
        out_specs=[pl.BlockSpec((tm, tn), lambda i, j: (i, j))] + slab_specs,
        out_shape=[jax.ShapeDtypeStruct((m, n), out_dtype)] + [jax.ShapeDtypeStruct(w.shape, BF16) for w in slabs],
        compiler_params=_cparams(("arbitrary", "arbitrary")),
        name=name,
    )(a, b, *slabs)
    return [out[0]] + [o.reshape(w.shape) for o, w in zip(out[1:], cast)]


def _latent_kernel(a_ref, b_ref, cq_ref, ckv_ref, kpe_ref, *, q_rank, kv_rank):
    acc = jnp.dot(a_ref[...], b_ref[...], preferred_element_type=F32)
    cq_ref[...] = acc[:, :q_rank].astype(cq_ref.dtype)
    ckv_ref[...] = acc[:, q_rank:q_rank + kv_rank].astype(ckv_ref.dtype)
    kpe_ref[...] = acc[:, q_rank + kv_rank:]


def _latent_proj(h, w_lat, q_rank, kv_rank):
    m, k = h.shape
    n = w_lat.shape[1]
    tm = _tile(m, 512)
    return pl.pallas_call(
        functools.partial(_latent_kernel, q_rank=q_rank, kv_rank=kv_rank),
        grid=(m // tm,),
        in_specs=[pl.BlockSpec((tm, k), lambda i: (i, 0)),
                  pl.BlockSpec((k, n), lambda i: (0, 0))],
        out_specs=[pl.BlockSpec((tm, q_rank), lambda i: (i, 0)),
                   pl.BlockSpec((tm, kv_rank), lambda i: (i, 0)),
                   pl.BlockSpec((tm, LANES), lambda i: (i, 0))],
        out_shape=[jax.ShapeDtypeStruct((m, q_rank), BF16),
                   jax.ShapeDtypeStruct((m, kv_rank), BF16),
                   jax.ShapeDtypeStruct((m, LANES), F32)],
        compiler_params=_cparams(("arbitrary",)),
        name="latent_proj",
    )(h, w_lat)


def _rope_table_kernel(pos_ref, inv_ref, c_ref, s1_ref, s2_ref):
    half = MLA_ROPE_DIM // 2
    ang = pos_ref[0].astype(F32) * inv_ref[...]
    cos, sin = jnp.cos(ang), jnp.sin(ang)
    lane = lax.broadcasted_iota(I32, ang.shape, 1)
    c_ref[0] = jnp.where(lane < 2 * half, cos, 0.0)
    s1_ref[0] = jnp.where(lane < half, -sin, 0.0)
    s2_ref[0] = jnp.where((lane >= half) & (lane < 2 * half), sin, 0.0)


def _rope_tables(positions):
    bsz, s = positions.shape
    half = MLA_ROPE_DIM // 2
    inv = ROPE_THETA ** (-(jnp.arange(LANES, dtype=F32) % half) / half)
    ts = _tile(s, 512)
    spec = pl.BlockSpec((1, ts, LANES), lambda b, i: (b, i, 0))
    shp = jax.ShapeDtypeStruct((bsz, s, LANES), F32)
    return pl.pallas_call(
        _rope_table_kernel,
        grid=(bsz, s // ts),
        in_specs=[pl.BlockSpec((1, ts, 1), lambda b, i: (b, i, 0)),
                  pl.BlockSpec((1, LANES), lambda b, i: (0, 0))],
        out_specs=[spec, spec, spec],
        out_shape=[shp, shp, shp],
        compiler_params=_cparams(("arbitrary", "arbitrary")),
        name="rope_tables",
    )(positions.reshape(bsz, s, 1), inv.reshape(1, LANES))


def _rotate(r, c, s1, s2):
    half = MLA_ROPE_DIM // 2
    return r * c + pltpu.roll(r, LANES - half, 1) * s1 + pltpu.roll(r, half, 1) * s2


def _rms(x, gain):
    ms = jnp.mean(x * x, axis=-1, keepdims=True)
    return x * lax.rsqrt(ms + NORM_EPS) * gain


def _mla_q_kernel(cq_ref, g_ref, w_ref, c_ref, s1_ref, s2_ref, o_ref, *, heads, scale):
    y = _rms(cq_ref[0].astype(F32), g_ref[...]).astype(BF16)
    q = jnp.dot(y, w_ref[...], preferred_element_type=F32)
    c, s1, s2 = c_ref[0], s1_ref[0], s2_ref[0]
    for h in range(heads):
        base = h * MLA_QK_PAD
        o_ref[0, :, base:base + LANES] = (q[:, base:base + LANES] * scale).astype(BF16)
        r = q[:, base + LANES:base + 2 * LANES]
        o_ref[0, :, base + LANES:base + 2 * LANES] = (_rotate(r, c, s1, s2) * scale).astype(BF16)


def _mla_q(cq, gain, w_q, tabs, scale):
    bsz, s, qr = cq.shape
    heads = MLA_HEADS
    n = heads * MLA_QK_PAD
    tm = _tile(s, 512)
    tab_spec = pl.BlockSpec((1, tm, LANES), lambda b, i: (b, i, 0))
    return pl.pallas_call(
        functools.partial(_mla_q_kernel, heads=heads, scale=scale),
        grid=(bsz, s // tm),
        in_specs=[pl.BlockSpec((1, tm, qr), lambda b, i: (b, i, 0)),
                  pl.BlockSpec((1, qr), lambda b, i: (0, 0)),
                  pl.BlockSpec((qr, n), lambda b, i: (0, 0)),
                  tab_spec, tab_spec, tab_spec],
        out_specs=pl.BlockSpec((1, tm, n), lambda b, i: (b, i, 0)),
        out_shape=jax.ShapeDtypeStruct((bsz, s, n), BF16),
        compiler_params=_cparams(("arbitrary", "arbitrary")),
        name="mla_q_prep",
    )(cq, gain.reshape(1, qr), w_q, *tabs)


def _mla_kv_kernel(ckv_ref, g_ref, wk_ref, wv_ref, kpe_ref, c_ref, s1_ref, s2_ref, k_ref, v_ref, *, heads):
    y = _rms(ckv_ref[0].astype(F32), g_ref[...]).astype(BF16)
    kn = jnp.dot(y, wk_ref[...], preferred_element_type=F32)
    v_ref[0] = jnp.dot(y, wv_ref[...], preferred_element_type=F32).astype(BF16)
    kr = _rotate(kpe_ref[0], c_ref[0], s1_ref[0], s2_ref[0]).astype(BF16)
    for h in range(heads):
        base = h * MLA_QK_PAD
        k_ref[0, :, base:base + LANES] = kn[:, h * LANES:(h + 1) * LANES].astype(BF16)
        k_ref[0, :, base + LANES:base + 2 * LANES] = kr


def _mla_kv(ckv, gain, w_k, w_v, kpe, tabs):
    bsz, s, kvr = ckv.shape
    heads = MLA_HEADS
    tm = _tile(s, 512)
    tab_spec = pl.BlockSpec((1, tm, LANES), lambda b, i: (b, i, 0))
    return pl.pallas_call(
        functools.partial(_mla_kv_kernel, heads=heads),
        grid=(bsz, s // tm),
        in_specs=[pl.BlockSpec((1, tm, kvr), lambda b, i: (b, i, 0)),
                  pl.BlockSpec((1, kvr), lambda b, i: (0, 0)),
                  pl.BlockSpec((kvr, heads * MLA_NOPE_DIM), lambda b, i: (0, 0)),
                  pl.BlockSpec((kvr, heads * MLA_V_DIM), lambda b, i: (0, 0)),
                  tab_spec, tab_spec, tab_spec, tab_spec],
        out_specs=[pl.BlockSpec((1, tm, heads * MLA_QK_PAD), lambda b, i: (b, i, 0)),
                   pl.BlockSpec((1, tm, heads * MLA_V_DIM), lambda b, i: (b, i, 0))],
        out_shape=[jax.ShapeDtypeStruct((bsz, s, heads * MLA_QK_PAD), BF16),
                   jax.ShapeDtypeStruct((bsz, s, heads * MLA_V_DIM), BF16)],
        compiler_params=_cparams(("arbitrary", "arbitrary")),
        name="mla_kv_prep",
    )(ckv, gain.reshape(1, kvr), w_k, w_v, kpe, *tabs)


def _qk(q, k):
    return lax.dot_general(q, k, (((1,), (1,)), ((), ())), preferred_element_type=F32)


def _causal_mask(s, qi, ki, tq, tk):
    row = qi * tq + lax.broadcasted_iota(I32, s.shape, 0)
    col = ki * tk + lax.broadcasted_iota(I32, s.shape, 1)
    return jnp.where(row >= col, s, -jnp.inf)


def _lane_chunks(x):
    return [x[:, c * LANES:(c + 1) * LANES] for c in range(x.shape[1] // LANES)]


def _softmax_update(s, v, m_prev, l_prev, acc_prev):
    chunks = _lane_chunks(s)
    cmax = functools.reduce(jnp.maximum, chunks)
    m_new = jnp.maximum(m_prev, jnp.max(cmax, axis=-1, keepdims=True))
    alpha = jnp.exp2(m_prev - m_new)
    ps = [jnp.exp2(c - m_new) for c in chunks]
    psum = functools.reduce(lambda a, b: a + b, ps)
    l_new = alpha * l_prev + jnp.sum(psum, axis=-1, keepdims=True)
    p = jnp.concatenate([c.astype(BF16) for c in ps], axis=1)
    pv = jnp.dot(p, v, preferred_element_type=F32)
    acc_new = jnp.concatenate([alpha * a for a in _lane_chunks(acc_prev)], axis=1) + pv
    return m_new, l_new, acc_new


def _emit_pipelined(score_fns, update_fns):
    n = len(score_fns)
    pending = [score_fns[i]() for i in range(min(ATTN_LOOKAHEAD, n))]
    for i in range(n):
        if i + ATTN_LOOKAHEAD < n:
            pending.append(score_fns[i + ATTN_LOOKAHEAD]())
        update_fns[i](pending[i])
        pending[i] = None


def _diag_mask(s, r):
    rows = s.shape[0]
    off = s.shape[1] - rows
    row = lax.broadcasted_iota(I32, s.shape, 0)
    col = lax.broadcasted_iota(I32, s.shape, 1)
    return jnp.where(col - off <= row, s, -jnp.inf)


def _slabs(w, steps):
    cols = w.shape[-1]
    rows = w.size // cols
    assert rows % (steps * 2 * SUBLANES) == 0, (w.shape, steps)
    return w.reshape(steps, rows // steps, cols)


def _slab_specs(slabs, step_of):
    return [pl.BlockSpec((1,) + w.shape[1:], lambda *g: (step_of(*g), 0, 0)) for w in slabs]


def _cast_slabs(src_refs, dst_refs):
    for src, dst in zip(src_refs, dst_refs):
        dst[...] = src[...].astype(BF16)


def _mla_attn_kernel(q_ref, k_ref, v_ref, *rest, tq, tk, rb, n_cast):
    cast_in, o_ref, cast_out, scr = (rest[:n_cast], rest[n_cast], rest[n_cast + 1:2 * n_cast + 1],
                                     rest[2 * n_cast + 1:])
    qi = pl.program_id(2)
    nr = tq // rb
    state = [scr[3 * r:3 * r + 3] for r in range(nr)]
    for m_scr, l_scr, acc_scr in state:
        m_scr[...] = jnp.full(m_scr.shape, -jnp.inf, F32)
        l_scr[...] = jnp.zeros(l_scr.shape, F32)
        acc_scr[...] = jnp.zeros(acc_scr.shape, F32)

    def rows(r):
        return pl.ds(r * rb, rb)

    def update(r, s, v):
        m_scr, l_scr, acc_scr = state[r]
        m, l, acc = _softmax_update(s, v, m_scr[...], l_scr[...], acc_scr[...])
        m_scr[...] = m
        l_scr[...] = l
        acc_scr[...] = acc

    def full_tile(ki, carry):
        start = pl.multiple_of(ki * tk, tk)
        k = k_ref[0, pl.ds(start, tk), :]
        v = v_ref[0, pl.ds(start, tk), :]
        _emit_pipelined([functools.partial(_qk, q_ref[0, rows(r), :], k) for r in range(nr)],
                        [functools.partial(update, r, v=v) for r in range(nr)])
        return carry
    lax.fori_loop(0, qi * (tq // tk), full_tile, 0)

    _cast_slabs(cast_in, cast_out)
    dstart = pl.multiple_of(qi * tq, tq)
    cols = [pl.ds(dstart, (r + 1) * rb) for r in range(nr)]
    _emit_pipelined(
        [lambda r=r: _diag_mask(_qk(q_ref[0, rows(r), :], k_ref[0, cols[r], :]), r) for r in range(nr)],
        [lambda s, r=r: update(r, s, v_ref[0, cols[r], :]) for r in range(nr)])

    for r, (_, l_scr, acc_scr) in enumerate(state):
        o_ref[0, rows(r), :] = (acc_scr[...] / l_scr[...]).astype(o_ref.dtype)


def _attn_state_scratch(nr, rb, dv):
    return [pltpu.VMEM((rb, w), F32) for _ in range(nr) for w in (LANES, LANES, dv)]


def _mla_attention(q, k, v, cast=()):
    bsz, s, _ = q.shape
    heads = MLA_HEADS
    t = _tile(s, 1024)
    rb = _tile(t, 256)
    nq = s // t
    slabs = [_slabs(w, bsz * heads * nq) for w in cast]
    slab_specs = _slab_specs(slabs, lambda b, h, i: (b * heads + h) * nq + i)
    out = pl.pallas_call(
        functools.partial(_mla_attn_kernel, tq=t, tk=t, rb=rb, n_cast=len(cast)),
        grid=(bsz, heads, nq),
        in_specs=[pl.BlockSpec((1, t, MLA_QK_PAD), lambda b, h, i: (b, i, h)),
                  pl.BlockSpec((1, s, MLA_QK_PAD), lambda b, h, i: (b, 0, h)),
                  pl.BlockSpec((1, s, MLA_V_DIM), lambda b, h, i: (b, 0, h))] + slab_specs,
        out_specs=[pl.BlockSpec((1, t, MLA_V_DIM), lambda b, h, i: (b, i, h))] + slab_specs,
        out_shape=[jax.ShapeDtypeStruct((bsz, s, heads * MLA_V_DIM), BF16)]
                  + [jax.ShapeDtypeStruct(w.shape, BF16) for w in slabs],
        scratch_shapes=_attn_state_scratch(t // rb, rb, MLA_V_DIM),
        compiler_params=_cparams(("arbitrary", "arbitrary", "arbitrary")),
        name="mla_attn",
    )(q, k, v, *slabs)
    return [out[0]] + [o.reshape(w.shape) for o, w in zip(out[1:], cast)]


def _diff_attn_kernel(q_ref, k_ref, v_ref, qpos_ref, kpos_ref, slope_ref, lam_ref, subln_ref, *rest,
                      tq, tk, rb, lam_init, n_cast):
    cast_in, o_ref, cast_out, scr = (rest[:n_cast], rest[n_cast], rest[n_cast + 1:2 * n_cast + 1],
                                     rest[2 * n_cast + 1:])
    qi = pl.program_id(2)
    d = DIFF_HEAD_DIM
    nr = tq // rb
    state = [[scr[6 * r + 3 * g:6 * r + 3 * g + 3] for g in range(2)] for r in range(nr)]
    for r in range(nr):
        for m_scr, l_scr, a_scr in state[r]:
            m_scr[...] = jnp.full(m_scr.shape, -jnp.inf, F32)
            l_scr[...] = jnp.zeros(l_scr.shape, F32)
            a_scr[...] = jnp.zeros(a_scr.shape, F32)
    slope = slope_ref[0, :, 0:1]

    def rows(r):
        return pl.ds(r * rb, rb)

    def kpos(first, count):
        return jnp.concatenate([kpos_ref[0, first + j] for j in range(count)], axis=1)

    chains = [(r, g) for r in range(nr) for g in range(2)]

    def emit(k_of, v_of, kp_of, masked):
        bias = {}

        def score(r, g):
            if g == 0:
                bias[r] = slope * jnp.abs(qpos_ref[0, rows(r), :] - kp_of(r))
            s = _qk(q_ref[0, rows(r), g * d:(g + 1) * d], k_of(r)[:, g * d:(g + 1) * d]) - bias[r]
            return _diag_mask(s, r) if masked else s

        def update(r, g, s):
            m_scr, l_scr, a_scr = state[r][g]
            m, l, acc = _softmax_update(s, v_of(r), m_scr[...], l_scr[...], a_scr[...])
            m_scr[...] = m
            l_scr[...] = l
            a_scr[...] = acc

        _emit_pipelined([functools.partial(score, r, g) for r, g in chains],
                        [functools.partial(update, r, g) for r, g in chains])

    def full_tile(ki, carry):
        start = pl.multiple_of(ki * tk, tk)
        k = k_ref[0, pl.ds(start, tk), :]
        v = v_ref[0, pl.ds(start, tk), :]
        kp = kpos(ki * (tk // rb), tk // rb)
        emit(lambda r: k, lambda r: v, lambda r: kp, False)
        return carry
    lax.fori_loop(0, qi * (tq // tk), full_tile, 0)

    _cast_slabs(cast_in, cast_out)
    dstart = pl.multiple_of(qi * tq, tq)
    cols = [pl.ds(dstart, (r + 1) * rb) for r in range(nr)]
    emit(lambda r: k_ref[0, cols[r], :], lambda r: v_ref[0, cols[r], :], lambda r: kpos(qi * nr, r + 1), True)

    lp = lam_ref[...]
    e1 = jnp.exp(jnp.sum(lp[0:1] * lp[1:2], axis=-1, keepdims=True))
    e2 = jnp.exp(jnp.sum(lp[2:3] * lp[3:4], axis=-1, keepdims=True))
    lam = e1 - e2 + lam_init
    for r in range(nr):
        (_, l1, a1), (_, l2, a2) = state[r]
        o1 = [a / l1[...] for a in _lane_chunks(a1[...])]
        o2 = [a / l2[...] for a in _lane_chunks(a2[...])]
        o = jnp.concatenate([x - lam * y for x, y in zip(o1, o2)], axis=1)
        o_ref[0, rows(r), :] = (_rms(o, subln_ref[...]) * (1.0 - lam_init)).astype(o_ref.dtype)


def _diff_attention(qkv, positions, diff_lambda, subln, lam_init, cast=()):
    bsz, s, _ = qkv.shape
    heads = DIFF_HEADS
    dv = 2 * DIFF_HEAD_DIM
    t = _tile(s, 1024)
    tk = _tile(t, 1024)
    rb = _tile(tk, 256)
    nq = s // t
    slopes = 2.0 ** (-8.0 * jnp.arange(1, heads + 1, dtype=F32) / heads) * LOG2_E
    slopes = jnp.broadcast_to(slopes[:, None, None], (heads, 1, LANES))
    posf = positions.astype(F32)
    slabs = [_slabs(w, bsz * heads * nq) for w in cast]
    slab_specs = _slab_specs(slabs, lambda b, h, i: (b * heads + h) * nq + i)
    out = pl.pallas_call(
        functools.partial(_diff_attn_kernel, tq=t, tk=tk, rb=rb, lam_init=lam_init, n_cast=len(cast)),
        grid=(bsz, heads, nq),
        in_specs=[pl.BlockSpec((1, t, dv), lambda b, h, i: (b, i, h)),
                  pl.BlockSpec((1, s, dv), lambda b, h, i: (b, 0, heads + h)),
                  pl.BlockSpec((1, s, dv), lambda b, h, i: (b, 0, 2 * heads + h)),
                  pl.BlockSpec((1, t, 1), lambda b, h, i: (b, i, 0)),
                  pl.BlockSpec((1, s // rb, 1, rb), lambda b, h, i: (b, 0, 0, 0)),
                  pl.BlockSpec((1, 1, LANES), lambda b, h, i: (h, 0, 0)),
                  pl.BlockSpec((4, DIFF_HEAD_DIM), lambda b, h, i: (0, 0)),
                  pl.BlockSpec((1, dv), lambda b, h, i: (0, 0))] + slab_specs,
        out_specs=[pl.BlockSpec((1, t, dv), lambda b, h, i: (b, i, h))] + slab_specs,
        out_shape=[jax.ShapeDtypeStruct((bsz, s, heads * dv), BF16)]
                  + [jax.ShapeDtypeStruct(w.shape, BF16) for w in slabs],
        scratch_shapes=_attn_state_scratch(2 * (t // rb), rb, dv),
        compiler_params=_cparams(("arbitrary", "arbitrary", "arbitrary")),
        name="diff_attn",
    )(qkv, qkv, qkv, posf.reshape(bsz, s, 1), posf.reshape(bsz, s // rb, 1, rb), slopes, diff_lambda,
      subln.reshape(1, dv), *slabs)
    return [out[0]] + [o.reshape(w.shape) for o, w in zip(out[1:], cast)]


def _merge_kernel(od_ref, om_ref, wd_ref, wm_ref, g0_ref, g1_ref, x_ref, gt_ref, o_ref):
    yd = jnp.dot(od_ref[0], wd_ref[...], preferred_element_type=F32)
    ym = jnp.dot(om_ref[0], wm_ref[...], preferred_element_type=F32)
    y = g0_ref[0].astype(F32) * yd + g1_ref[0].astype(F32) * ym
    o_ref[0] = x_ref[0] + gt_ref[0, 0] * y


def _merge(o_d, o_m, w_d, w_m, gates, x, mod, gt_idx):
    bsz, s, d = x.shape
    kd, km = o_d.shape[-1], o_m.shape[-1]
    tm, tn = _tile(s, 1024), _tile(d, 512)
    nj = d // tn
    return pl.pallas_call(
        _merge_kernel,
        grid=(bsz, s // tm, nj),
        in_specs=[pl.BlockSpec((1, tm, kd), lambda b, i, j: (b, i, 0)),
                  pl.BlockSpec((1, tm, km), lambda b, i, j: (b, i, 0)),
                  pl.BlockSpec((kd, tn), lambda b, i, j: (0, j)),
                  pl.BlockSpec((km, tn), lambda b, i, j: (0, j)),
                  pl.BlockSpec((1, tm, tn), lambda b, i, j: (b, i, j)),
                  pl.BlockSpec((1, tm, tn), lambda b, i, j: (b, i, j + nj)),
                  pl.BlockSpec((1, tm, tn), lambda b, i, j: (b, i, j)),
                  pl.BlockSpec((1, 1, 1, tn), lambda b, i, j: (gt_idx, b, 0, j))],
        out_specs=pl.BlockSpec((1, tm, tn), lambda b, i, j: (b, i, j)),
        out_shape=jax.ShapeDtypeStruct((bsz, s, d), F32),
        compiler_params=_cparams(("arbitrary", "arbitrary", "arbitrary")),
        name="out_merge",
    )(o_d, o_m, w_d, w_m, gates, gates, x, mod)


def _pack_rows(y):
    half = y.shape[1] // 2
    bits = lax.bitcast_convert_type(y.astype(BF16).astype(F32), U32)
    return (bits[:, half:] & jnp.uint32(0xFFFF0000)) | (bits[:, :half] >> 16)


def _unpack_rows(w):
    lo = lax.bitcast_convert_type(w << 16, F32)
    hi = lax.bitcast_convert_type(w & jnp.uint32(0xFFFF0000), F32)
    return lo, hi


def _ffn_norm_kernel(x_ref, g_ref, sc_ref, sh_ref, rw_ref, h_ref, hp_ref, lt_ref, *, nch):
    x = x_ref[0]
    y = _rms(x, g_ref[...]) * (1.0 + sc_ref[0, 0]) + sh_ref[0, 0]
    hb = y.astype(BF16)
    h_ref[0] = hb
    lt_ref[...] = lax.dot_general(rw_ref[...], hb, (((1,), (1,)), ((), ())), preferred_element_type=F32)
    packed = _pack_rows(y)
    for c in range(nch):
        hp_ref[pl.ds(c, x.shape[0], stride=nch), :] = packed[:, c * LANES:(c + 1) * LANES]


def _ffn_norm(x, gain, mod, sc_idx, sh_idx, router_wt):
    bsz, s, d = x.shape
    e = router_wt.shape[0]
    nch = d // (2 * LANES)
    tm = _tile(s, 256)
    ns = s // tm
    return pl.pallas_call(
        functools.partial(_ffn_norm_kernel, nch=nch),
        grid=(bsz, ns),
        in_specs=[pl.BlockSpec((1, tm, d), lambda b, i: (b, i, 0)),
                  pl.BlockSpec((1, d), lambda b, i: (0, 0)),
                  pl.BlockSpec((1, 1, 1, d), lambda b, i: (sc_idx, b, 0, 0)),
                  pl.BlockSpec((1, 1, 1, d), lambda b, i: (sh_idx, b, 0, 0)),
                  pl.BlockSpec((e, d), lambda b, i: (0, 0))],
        out_specs=[pl.BlockSpec((1, tm, d), lambda b, i: (b, i, 0)),
                   pl.BlockSpec((tm * nch, LANES), lambda b, i: (b * ns + i, 0)),
                   pl.BlockSpec((e, tm), lambda b, i: (0, b * ns + i))],
        out_shape=[jax.ShapeDtypeStruct((bsz, s, d), BF16),
                   jax.ShapeDtypeStruct((bsz * s * nch, LANES), U32),
                   jax.ShapeDtypeStruct((e, bsz * s), F32)],
        compiler_params=_cparams(("arbitrary", "arbitrary")),
        name="ffn_norm_router",
    )(x, gain.reshape(1, d), mod, mod, router_wt)


def _first_index(hit, iota, axis, size):
    return jnp.min(jnp.where(hit, iota, size), axis=axis, keepdims=True)


def _route_kernel(lt_ref, bias_ref, idx_ref, wt_ref, rank_ref, sizes_ref, cnt_scr, *, n_exp, tn):
    i = pl.program_id(0)
    gsz = n_exp // N_GROUPS

    @pl.when(i == 0)
    def _():
        cnt_scr[...] = jnp.zeros(cnt_scr.shape, F32)

    scores = jax.nn.sigmoid(lt_ref[...])
    sel = scores + bias_ref[...]
    sel3 = sel.reshape(N_GROUPS, gsz, tn)
    j3 = lax.broadcasted_iota(I32, sel3.shape, 1)
    top1 = jnp.max(sel3, axis=1, keepdims=True)
    first = _first_index(sel3 == top1, j3, 1, gsz)
    top2 = jnp.max(jnp.where(j3 == first, -jnp.inf, sel3), axis=1, keepdims=True)
    gscore = (top1 + top2).reshape(N_GROUPS, tn)

    giota = lax.broadcasted_iota(I32, gscore.shape, 0)
    gmask = jnp.zeros(gscore.shape, jnp.bool_)
    for _ in range(TOPK_GROUPS):
        best = jnp.max(gscore, axis=0, keepdims=True)
        gi = _first_index(gscore == best, giota, 0, N_GROUPS)
        hit = giota == gi
        gmask = gmask | hit
        gscore = jnp.where(hit, -jnp.inf, gscore)

    emask = jnp.broadcast_to(gmask.reshape(N_GROUPS, 1, tn), sel3.shape)
    cand = jnp.where(emask, sel3, -jnp.inf).reshape(n_exp, tn)
    eiota = lax.broadcasted_iota(I32, cand.shape, 0)
    hits, idxs, vals = [], [], []
    for _ in range(TOP_K):
        best = jnp.max(cand, axis=0, keepdims=True)
        ei = _first_index(cand == best, eiota, 0, n_exp)
        hit = eiota == ei
        hits.append(hit)
        idxs.append(ei)
        vals.append(jnp.sum(jnp.where(hit, scores, 0.0), axis=0, keepdims=True))
        cand = jnp.where(hit, -jnp.inf, cand)
    total = functools.reduce(lambda a, b: a + b, vals)

    chosen = functools.reduce(lambda a, b: a | b, hits)
    onehot = jnp.where(chosen, 1.0, 0.0)
    r = lax.broadcasted_iota(I32, (tn, tn), 0)
    c = lax.broadcasted_iota(I32, (tn, tn), 1)
    upper = jnp.where(r < c, 1.0, 0.0).astype(BF16)
    before = cnt_scr[...] + jnp.dot(onehot.astype(BF16), upper, preferred_element_type=F32)
    cnt_new = cnt_scr[...] + jnp.sum(onehot, axis=1, keepdims=True)
    cnt_scr[...] = cnt_new

    pad = SUBLANES - TOP_K
    ranks = [jnp.sum(jnp.where(h, before, 0.0), axis=0, keepdims=True).astype(I32) for h in hits]
    zi = [jnp.zeros((pad, tn), I32)]
    idx_ref[...] = jnp.concatenate(idxs + zi, axis=0)
    rank_ref[...] = jnp.concatenate(ranks + zi, axis=0)
    wt_ref[...] = jnp.concatenate([v / total * ROUTED_SCALE for v in vals] + [jnp.zeros((pad, tn), F32)], axis=0)
    sizes_ref[...] = jnp.broadcast_to(cnt_new, sizes_ref.shape).astype(I32)


def _route(logits_t, bias):
    n_exp, n = logits_t.shape
    tn = _tile(n, 512)
    row_spec = pl.BlockSpec((SUBLANES, tn), lambda i: (0, i))
    return pl.pallas_call(
        functools.partial(_route_kernel, n_exp=n_exp, tn=tn),
        grid=(n // tn,),
        in_specs=[pl.BlockSpec((n_exp, tn), lambda i: (0, i)),
                  pl.BlockSpec((n_exp, 1), lambda i: (0, 0))],
        out_specs=[row_spec, row_spec, row_spec, pl.BlockSpec((n_exp, LANES), lambda i: (0, 0))],
        out_shape=[jax.ShapeDtypeStruct((SUBLANES, n), I32),
                   jax.ShapeDtypeStruct((SUBLANES, n), F32),
                   jax.ShapeDtypeStruct((SUBLANES, n), I32),
                   jax.ShapeDtypeStruct((n_exp, LANES), I32)],
        scratch_shapes=[pltpu.VMEM((n_exp, 1), F32)],
        compiler_params=_cparams(("arbitrary",)),
        name="route_topk",
    )(logits_t, bias.reshape(n_exp, 1))


def _slot_rows(ref, slot, blk, nch):
    assert blk & (blk - 1) == 0
    b = jnp.right_shift(slot, blk.bit_length() - 1)
    r = jnp.bitwise_and(slot, blk - 1)
    return ref.at[pl.ds(pl.multiple_of(b * nch, nch), nch), r]


def _dispatch_kernel(dest_ref, seg_ref, hp_ref, xs_ref, zero_scr, sem, zsem, *, n_tok, tc, nch, n_exp, blk, nb):
    i = pl.program_id(0)

    def row_copy(j, slot):
        return pltpu.make_async_copy(hp_ref.at[pl.ds(pl.multiple_of(j * nch, nch), nch)],
                                     _slot_rows(xs_ref, slot, blk, nch), sem)

    def zero_row(slot):
        return pltpu.make_async_copy(zero_scr.at[:, 0], _slot_rows(xs_ref, slot, blk, nch), zsem)

    def zero_block(b):
        return pltpu.make_async_copy(zero_scr, xs_ref.at[pl.ds(pl.multiple_of(b * nch, nch), nch)], zsem)

    def start_then_wait(lo, hi, copy):
        def start(v, c):
            copy(v).start()
            return c
        lax.fori_loop(lo, hi, start, 0)

        def wait(v, c):
            copy(v).wait()
            return c
        lax.fori_loop(lo, hi, wait, 0)

    @pl.when(i == 0)
    def _():
        zero_scr[...] = jnp.zeros(zero_scr.shape, U32)

        def per_expert(e, carry):
            start_then_wait(seg_ref[e], seg_ref[n_exp + e], zero_row)
            return carry
        lax.fori_loop(0, n_exp, per_expert, 0)
        start_then_wait(seg_ref[2 * n_exp], nb, zero_block)

    base = i * tc

    def start(j, c):
        for k in range(TOP_K):
            row_copy(j, dest_ref[k * n_tok + base + j]).start(priority=k % DMA_QUEUES)
        return c
    lax.fori_loop(0, tc, start, 0, unroll=DMA_ISSUE_UNROLL)

    for _ in range(tc * TOP_K):
        row_copy(0, 0).wait()


def _dispatch(dest, seg, hp, nb, blk, nch, n_exp):
    n_tok = dest.shape[0] // TOP_K
    tc = _tile(n_tok, 256)
    return pl.pallas_call(
        functools.partial(_dispatch_kernel, n_tok=n_tok, tc=tc, nch=nch, n_exp=n_exp, blk=blk, nb=nb),
        grid_spec=pltpu.PrefetchScalarGridSpec(
            num_scalar_prefetch=2,
            grid=(n_tok // tc,),
            in_specs=[pl.BlockSpec((tc * nch, LANES), lambda i, dst, sg: (i, 0))],
            out_specs=pl.BlockSpec(memory_space=pl.ANY),
            scratch_shapes=[pltpu.VMEM((nch, blk, LANES), U32), pltpu.SemaphoreType.DMA,
                            pltpu.SemaphoreType.DMA]),
        out_shape=jax.ShapeDtypeStruct((nb * nch, blk, LANES), U32),
        compiler_params=_cparams(("arbitrary",)),
        name="moe_dispatch",
    )(dest, seg, hp)


def _load_rows(ref, nch):
    los, his = [], []
    for c in range(nch):
        lo, hi = _unpack_rows(ref[c])
        los.append(lo.astype(BF16))
        his.append(hi.astype(BF16))
    return jnp.concatenate(los + his, axis=1)


def _expert_kernel(be_ref, nact_ref, xs_ref, w1_ref, w3_ref, w2_ref, ys_ref, *, nch):
    active = pl.program_id(0) < nact_ref[0]

    @pl.when(jnp.logical_not(active))
    def _():
        ys_ref[...] = jnp.zeros(ys_ref.shape, U32)

    @pl.when(active)
    def _():
        x = _load_rows(xs_ref, nch)
        a = jnp.dot(x, w1_ref[0], preferred_element_type=F32)
        b = jnp.dot(x, w3_ref[0], preferred_element_type=F32)
        y = jnp.dot((_silu(a) * b).astype(BF16), w2_ref[0], preferred_element_type=F32)
        packed = _pack_rows(y)
        for c in range(nch):
            ys_ref[c] = packed[:, c * LANES:(c + 1) * LANES]


def _experts(block_expert, nact, xs, w1, w3, w2, nch):
    n_exp, d, f = w1.shape
    tm = xs.shape[1]
    nb = xs.shape[0] // nch
    blk = lambda i, be, na: (jnp.minimum(i, na[0] - 1), 0, 0)
    wsel = lambda i, be, na: (be[jnp.minimum(i, na[0] - 1)], 0, 0)
    return pl.pallas_call(
        functools.partial(_expert_kernel, nch=nch),
        grid_spec=pltpu.PrefetchScalarGridSpec(
            num_scalar_prefetch=2,
            grid=(nb,),
            in_specs=[pl.BlockSpec((nch, tm, LANES), blk),
                      pl.BlockSpec((1, d, f), wsel),
                      pl.BlockSpec((1, d, f), wsel),
                      pl.BlockSpec((1, f, d), wsel)],
            out_specs=pl.BlockSpec((nch, tm, LANES), lambda i, be, na: (i, 0, 0))),
        out_shape=jax.ShapeDtypeStruct(xs.shape, U32),
        compiler_params=_cparams(("arbitrary",)),
        name="moe_experts",
    )(block_expert, nact, xs, w1, w3, w2)


def _shared_ffn_kernel(h_ref, w1_ref, w3_ref, w2_ref, o_ref, *, out_chunk):
    h = h_ref[...]
    a = jnp.dot(h, w1_ref[...], preferred_element_type=F32)
    g = jnp.dot(h, w3_ref[...], preferred_element_type=F32)
    mid = (_silu(a) * g).astype(BF16)
    for c in range(0, o_ref.shape[1], out_chunk):
        o_ref[:, c:c + out_chunk] = jnp.dot(mid, w2_ref[:, c:c + out_chunk],
                                            preferred_element_type=F32).astype(o_ref.dtype)


def _shared_ffn(h, w1, w3, w2):
    n, d = h.shape
    f = w1.shape[1]
    tm = _tile(n, 512)
    return pl.pallas_call(
        functools.partial(_shared_ffn_kernel, out_chunk=_tile(d, 1024)),
        grid=(n // tm,),
        in_specs=[pl.BlockSpec((tm, d), lambda i: (i, 0)),
                  pl.BlockSpec((d, f), lambda i: (0, 0)),
                  pl.BlockSpec((d, f), lambda i: (0, 0)),
                  pl.BlockSpec((f, d), lambda i: (0, 0))],
        out_specs=pl.BlockSpec((tm, d), lambda i: (i, 0)),
        out_shape=jax.ShapeDtypeStruct((n, d), BF16),
        compiler_params=_cparams(("arbitrary",)),
        name="shared_ffn",
    )(h, w1, w3, w2)


def _combine_kernel(dest_ref, ys_ref, wt_ref, shared_ref, x_ref, gt_ref, fn_ref, o_ref,
                    ybuf0, ybuf1, sem0, sem1, *, n_tok, tm, nch, nsteps, final):
    step = pl.program_id(0) * pl.num_programs(1) + pl.program_id(1)
    bufs = ((ybuf0, sem0), (ybuf1, sem1))

    def row_copy(buf, sem, slot, k, j):
        return pltpu.make_async_copy(_slot_rows(ys_ref, slot, ys_ref.shape[1], nch), buf.at[k, :, j], sem)

    def start_gathers(tile, buf, sem, inline=False):
        base = tile * tm

        def start(j, c):
            for k in range(TOP_K):
                row_copy(buf, sem, dest_ref[k * n_tok + base + j], k, j).start(priority=k % DMA_QUEUES)
            return c
        if inline:
            for j in range(tm):
                start(j, 0)
        else:
            lax.fori_loop(0, tm, start, 0, unroll=DMA_ISSUE_UNROLL)

    def wait_gathers(buf, sem):
        for _ in range(tm * TOP_K):
            row_copy(buf, sem, 0, 0, 0).wait()

    @pl.when(step == 0)
    def _():
        start_gathers(0, *bufs[0])

    def consume(cur, nxt):
        wait_gathers(*cur)
        start_gathers(jnp.minimum(step + 1, nsteps - 1), *nxt, inline=True)
        ybuf = cur[0]
        wts = wt_ref[...]
        los = [None] * nch
        his = [None] * nch
        for k in range(TOP_K):
            wk = jnp.broadcast_to(wts[:, k:k + 1], (tm, LANES))
            for c in range(nch):
                lo, hi = _unpack_rows(ybuf[k, c])
                los[c] = wk * lo if k == 0 else los[c] + wk * lo
                his[c] = wk * hi if k == 0 else his[c] + wk * hi
        routed = jnp.concatenate(los + his, axis=1)
        y = x_ref[0] + gt_ref[0, 0] * (routed + shared_ref[0].astype(F32))
        o_ref[0] = _rms(y, fn_ref[...]) if final else y

    pl.when(step % 2 == 0)(lambda: consume(bufs[0], bufs[1]))
    pl.when(step % 2 == 1)(lambda: consume(bufs[1], bufs[0]))

    @pl.when(step == nsteps - 1)
    def _():
        wait_gathers(*bufs[nsteps % 2])


def _combine(dest, ys, wts, shared, x, mod, gt_idx, final_norm, nch, final):
    bsz, s, d = x.shape
    n_tok = bsz * s
    tm = _tile(s, 128)
    ns = s // tm
    return pl.pallas_call(
        functools.partial(_combine_kernel, n_tok=n_tok, tm=tm, nch=nch, nsteps=bsz * ns, final=final),
        grid_spec=pltpu.PrefetchScalarGridSpec(
            num_scalar_prefetch=1,
            grid=(bsz, ns),
            in_specs=[pl.BlockSpec(memory_space=pl.ANY),
                      pl.BlockSpec((tm, SUBLANES), lambda b, i, dst: (b * ns + i, 0)),
                      pl.BlockSpec((1, tm, d), lambda b, i, dst: (b, i, 0)),
                      pl.BlockSpec((1, tm, d), lambda b, i, dst: (b, i, 0)),
                      pl.BlockSpec((1, 1, 1, d), lambda b, i, dst: (gt_idx, b, 0, 0)),
                      pl.BlockSpec((1, d), lambda b, i, dst: (0, 0))],
            out_specs=pl.BlockSpec((1, tm, d), lambda b, i, dst: (b, i, 0)),
            scratch_shapes=[pltpu.VMEM((TOP_K, nch, tm, LANES), U32), pltpu.VMEM((TOP_K, nch, tm, LANES), U32),
                            pltpu.SemaphoreType.DMA, pltpu.SemaphoreType.DMA]),
        out_shape=jax.ShapeDtypeStruct((bsz, s, d), F32),
        compiler_params=_cparams(("arbitrary", "arbitrary")),
        name="moe_combine",
    )(dest, ys, wts, shared, x, mod, final_norm.reshape(1, d))


def _mla_weights(w_uq, w_ukv):
    heads = MLA_HEADS
    qr, kvr = w_uq.shape[0], w_ukv.shape[0]
    wq = w_uq.reshape(qr, heads, MLA_NOPE_DIM + MLA_ROPE_DIM)
    wq = jnp.pad(wq, ((0, 0), (0, 0), (0, MLA_QK_PAD - MLA_NOPE_DIM - MLA_ROPE_DIM)))
    wkv = w_ukv.reshape(kvr, heads, MLA_NOPE_DIM + MLA_V_DIM)
    wk = wkv[:, :, :MLA_NOPE_DIM].reshape(kvr, heads * MLA_NOPE_DIM)
    wv = wkv[:, :, MLA_NOPE_DIM:].reshape(kvr, heads * MLA_V_DIM)
    return (wq.reshape(qr, heads * MLA_QK_PAD).astype(BF16), wk.astype(BF16), wv.astype(BF16))


def _layer(x, mod, positions, tabs, l, norm_attn, w_in, diff_lambda, diff_subln, mla_q_norm, mla_w_uq,
           mla_kv_norm, mla_w_ukv, w_out, norm_ffn, router_w, router_bias, exp_w1, exp_w3, exp_w2,
           shared_w1, shared_w3, shared_w2, final_norm, final):
    bsz, s, d = x.shape
    n_tok = bsz * s
    q_rank, kv_rank = mla_w_uq.shape[0], mla_w_ukv.shape[0]
    qk_cols = 2 * DIFF_HEADS * DIFF_HEAD_DIM
    v_cols = DIFF_HEADS * 2 * DIFF_HEAD_DIM
    qkv_cols = 2 * qk_cols + v_cols
    lat_cols = q_rank + kv_rank + MLA_ROPE_DIM
    lam_init = 0.8 - 0.6 * math.exp(-0.3 * l)

    h = _norm_mod(x, norm_attn, mod, 1, 0).reshape(n_tok, d)
    w_qkv = w_in[:, :qkv_cols].astype(BF16)
    w_lat = jnp.pad(w_in[:, qkv_cols:qkv_cols + lat_cols], ((0, 0), (0, LANES - MLA_ROPE_DIM))).astype(BF16)
    w_gate = w_in[:, qkv_cols + lat_cols:].astype(BF16)
    qkv = _matmul(h, w_qkv, BF16, name="qkv_proj", scaled_cols=qk_cols,
                  col_scale=DIFF_HEAD_DIM ** -0.5 * LOG2_E)[0].reshape(bsz, s, qkv_cols)
    cq, ckv, kpe = _latent_proj(h, w_lat, q_rank, kv_rank)
    gates, exp_w3b = _matmul(h, w_gate, BF16, name="gate_proj", sigmoid=True, cast=(exp_w3,))
    gates = gates.reshape(bsz, s, 2 * d)
    o_d, exp_w2b = _diff_attention(qkv, positions, diff_lambda, diff_subln, lam_init, cast=(exp_w2,))

    wq, wk, wv = _mla_weights(mla_w_uq, mla_w_ukv)
    scale = (MLA_NOPE_DIM + MLA_ROPE_DIM) ** -0.5 * LOG2_E
    q_m = _mla_q(cq.reshape(bsz, s, q_rank), mla_q_norm, wq, tabs, scale)
    k_m, v_m = _mla_kv(ckv.reshape(bsz, s, kv_rank), mla_kv_norm, wk, wv, kpe.reshape(bsz, s, LANES), tabs)
    o_m, exp_w1b = _mla_attention(q_m, k_m, v_m, cast=(exp_w1,))

    w_o = w_out.astype(BF16)
    x = _merge(o_d, o_m, w_o[:v_cols], w_o[v_cols:], gates, x, mod, 2)

    n_exp = router_w.shape[1]
    nch = d // (2 * LANES)
    h2, hp, logits_t = _ffn_norm(x, norm_ffn, mod, 4, 3, router_w.T.astype(BF16))
    idx_t, wts_t, rank_t, sizes = _route(logits_t, router_bias)

    blk = EXPERT_ROWS
    sizes = sizes[:, 0]
    padded = (sizes + blk - 1) // blk * blk
    pad_end = jnp.cumsum(padded)
    pad_start = pad_end - padded
    onehot = idx_t[:TOP_K, :, None] == jnp.arange(n_exp, dtype=I32)
    dest = (jnp.sum(jnp.where(onehot, pad_start, 0), axis=-1) + rank_t[:TOP_K]).astype(I32).reshape(-1)
    n_blocks = -(-n_tok * TOP_K // blk) + n_exp
    block_start = jnp.arange(n_blocks, dtype=I32) * blk
    block_expert = jnp.minimum(jnp.sum(pad_end[None, :] <= block_start[:, None], axis=1), n_exp - 1).astype(I32)
    nact = (pad_end[-1:] // blk).astype(I32)
    seg = jnp.concatenate([pad_start + sizes, pad_end, nact]).astype(I32)

    xs = _dispatch(dest, seg, hp, n_blocks, blk, nch, n_exp)
    ys = _experts(block_expert, nact, xs, exp_w1b, exp_w3b, exp_w2b, nch)
    shared = _shared_ffn(h2.reshape(n_tok, d), shared_w1.astype(BF16), shared_w3.astype(BF16),
                         shared_w2.astype(BF16)).reshape(bsz, s, d)
    return _combine(dest, ys, wts_t.T, shared, x, mod, 5, final_norm, nch, final)


def kernel(x, c, positions, w_ada, b_ada, norm_attn, w_in, diff_lambda, diff_subln, mla_q_norm, mla_w_uq,
           mla_kv_norm, mla_w_ukv, w_out, norm_ffn, router_w, router_bias, exp_w1, exp_w3, exp_w2,
           shared_w1, shared_w3, shared_w2, final_norm):
    bsz, s, d = x.shape
    depth = w_ada.shape[0]
    tabs = _rope_tables(positions)
    for l in range(depth):
        mod = _ada(c, w_ada[l], b_ada[l])
        mod = mod.reshape(bsz, N_MOD, 1, d).transpose(1, 0, 2, 3)
        x = _layer(x, mod, positions, tabs, l, norm_attn[l], w_in[l], diff_lambda[l], diff_subln[l],
                   mla_q_norm[l], mla_w_uq[l], mla_kv_norm[l], mla_w_ukv[l], w_out[l], norm_ffn[l],
                   router_w[l], router_bias[l], exp_w1[l], exp_w3[l], exp_w2[l],
                   shared_w1[l], shared_w3[l], shared_w2[l], final_norm, l == depth - 1)
    return x
```

```python
import functools
import math

import jax
import jax.numpy as jnp
from jax import lax
from jax.experimental import pallas as pl
from jax.experimental.pallas import tpu as pltpu

F32 = jnp.float32
BF16 = jnp.bfloat16
U32 = jnp.uint32
I32 = jnp.int32

DIFF_HEADS = 8
DIFF_HEAD_DIM = 128
MLA_HEADS = 16
MLA_NOPE_DIM = 128
MLA_ROPE_DIM = 64
MLA_V_DIM = 128
ROPE_THETA = 10000.0
TOP_K = 6
N_GROUPS = 8
TOPK_GROUPS = 4
ROUTED_SCALE = 2.5
NORM_EPS = 1e-6
N_MOD = 6
LOG2_E = math.log2(math.e)

LANES = 128
SUBLANES = 8
MLA_QK_PAD = 256
EXPERT_ROWS = 256
DMA_ISSUE_UNROLL = 4
DMA_QUEUES = 2
ATTN_LOOKAHEAD = 2
VMEM_LIMIT = 56 * 1024 * 1024


def _cparams(sem):
    return pltpu.CompilerParams(dimension_semantics=sem, vmem_limit_bytes=VMEM_LIMIT)


def _tile(n, pref):
    t = min(n, pref)
    assert n % t == 0, (n, pref)
    return t


def _silu(a):
    return a * jax.nn.sigmoid(a)


def _ada_kernel(c_ref, w_ref, b_ref, o_ref):
    c = c_ref[...]
    a = _silu(c).astype(BF16)
    o_ref[...] = jnp.dot(a, w_ref[...].astype(BF16), preferred_element_type=F32) + b_ref[...]


def _ada(c, w, b):
    bsz, d = c.shape
    n = w.shape[1]
    tn = _tile(n, 512)
    return pl.pallas_call(
        _ada_kernel,
        grid=(n // tn,),
        in_specs=[pl.BlockSpec((bsz, d), lambda j: (0, 0)),
                  pl.BlockSpec((d, tn), lambda j: (0, j)),
                  pl.BlockSpec((1, tn), lambda j: (0, j))],
        out_specs=pl.BlockSpec((bsz, tn), lambda j: (0, j)),
        out_shape=jax.ShapeDtypeStruct((bsz, n), F32),
        compiler_params=_cparams(("arbitrary",)),
        name="ada_mod",
    )(c, w, b.reshape(1, n))


def _norm_mod_kernel(x_ref, g_ref, sc_ref, sh_ref, o_ref):
    x = x_ref[0]
    ms = jnp.mean(x * x, axis=-1, keepdims=True)
    y = x * lax.rsqrt(ms + NORM_EPS) * g_ref[...]
    o_ref[0] = (y * (1.0 + sc_ref[0, 0]) + sh_ref[0, 0]).astype(o_ref.dtype)


def _norm_mod(x, gain, mod, sc_idx, sh_idx):
    bsz, s, d = x.shape
    tm = _tile(s, 512)
    return pl.pallas_call(
        _norm_mod_kernel,
        grid=(bsz, s // tm),
        in_specs=[pl.BlockSpec((1, tm, d), lambda b, i: (b, i, 0)),
                  pl.BlockSpec((1, d), lambda b, i: (0, 0)),
                  pl.BlockSpec((1, 1, 1, d), lambda b, i: (sc_idx, b, 0, 0)),
                  pl.BlockSpec((1, 1, 1, d), lambda b, i: (sh_idx, b, 0, 0))],
        out_specs=pl.BlockSpec((1, tm, d), lambda b, i: (b, i, 0)),
        out_shape=jax.ShapeDtypeStruct((bsz, s, d), BF16),
        compiler_params=_cparams(("arbitrary", "arbitrary")),
        name="norm_mod",
    )(x, gain.reshape(1, d), mod, mod)


def _w_in_split_kernel(w_ref, qkv_ref, lat_ref, gate_ref, *, qkv_cols, lat_cols):
    qkv_ref[...] = w_ref[:, :qkv_cols].astype(BF16)
    lat = w_ref[:, qkv_cols:qkv_cols + lat_cols]
    pad = jnp.zeros((lat.shape[0], lat_ref.shape[1] - lat_cols), F32)
    lat_ref[...] = jnp.concatenate([lat, pad], axis=1).astype(BF16)
    gate_ref[...] = w_ref[:, qkv_cols + lat_cols:].astype(BF16)


def _w_in_split(w_in, qkv_cols, lat_cols):
    d, n = w_in.shape
    gate_cols = n - qkv_cols - lat_cols
    lat_pad = -(-lat_cols // LANES) * LANES
    tr = _tile(d, 128)
    return pl.pallas_call(
        functools.partial(_w_in_split_kernel, qkv_cols=qkv_cols, lat_cols=lat_cols),
        grid=(d // tr,),
        in_specs=[pl.BlockSpec((tr, n), lambda i: (i, 0))],
        out_specs=[pl.BlockSpec((tr, qkv_cols), lambda i: (i, 0)),
                   pl.BlockSpec((tr, lat_pad), lambda i: (i, 0)),
                   pl.BlockSpec((tr, gate_cols), lambda i: (i, 0))],
        out_shape=[jax.ShapeDtypeStruct((d, qkv_cols), BF16),
                   jax.ShapeDtypeStruct((d, lat_pad), BF16),
                   jax.ShapeDtypeStruct((d, gate_cols), BF16)],
        compiler_params=_cparams(("arbitrary",)),
        name="w_in_split",
    )(w_in)


def _mm_kernel(a_ref, b_ref, *rest, sigmoid, scaled_tiles, col_scale, n_cast):
    cast_in, o_ref, cast_out = rest[:n_cast], rest[n_cast], rest[n_cast + 1:]
    acc = jnp.dot(a_ref[...], b_ref[...], preferred_element_type=F32)
    if sigmoid:
        acc = jax.nn.sigmoid(acc)
    if scaled_tiles:
        acc = acc * jnp.where(pl.program_id(1) < scaled_tiles, col_scale, 1.0)
    o_ref[...] = acc.astype(o_ref.dtype)
    _cast_slabs(cast_in, cast_out)


def _matmul(a, b, out_dtype, *, name, sigmoid=False, scaled_cols=0, col_scale=1.0, cast=(),
            tm_pref=1024, tn_pref=512):
    m, k = a.shape
    n = b.shape[1]
    tm, tn = _tile(m, tm_pref), _tile(n, tn_pref)
    assert scaled_cols % tn == 0
    nj = n // tn
    slabs = [_slabs(w, (m // tm) * nj) for w in cast]
    slab_specs = _slab_specs(slabs, lambda i, j: i * nj + j)
    out = pl.pallas_call(
        functools.partial(_mm_kernel, sigmoid=sigmoid, scaled_tiles=scaled_cols // tn, col_scale=col_scale,
                          n_cast=len(cast)),
        grid=(m // tm, nj),
        in_specs=[pl.BlockSpec((tm, k), lambda i, j: (i, 0)),
                  pl.BlockSpec((k, tn), lambda i, j: (0, j))] + slab_specs,
        out_specs=[pl.BlockSpec((tm, tn), lambda i, j: (i, j))] + slab_specs,
        out_shape=[jax.ShapeDtypeStruct((m, n), out_dtype)] + [jax.ShapeDtypeStruct(w.shape, BF16) for w in slabs],
        compiler_params=_cparams(("arbitrary", "arbitrary")),
        name=name,
    )(a, b, *slabs)
    return [out[0]] + [o.reshape(w.shape) for o, w in zip(out[1:], cast)]


def _latent_kernel(a_ref, b_ref, cq_ref, ckv_ref, kpe_ref, *, q_rank, kv_rank):
    acc = jnp.dot(a_ref[...], b_ref[...], preferred_element_type=F32)
    cq_ref[...] = acc[:, :q_rank].astype(cq_ref.dtype)
    ckv_ref[...] = acc[:, q_rank:q_rank + kv_rank].astype(ckv_ref.dtype)
    kpe_ref[...] = acc[:, q_rank + kv_rank:]


def _latent_proj(h, w_lat, q_rank, kv_rank):
    m, k = h.shape
    n = w_lat.shape[1]
    tm = _tile(m, 512)
    return pl.pallas_call(
        functools.partial(_latent_kernel, q_rank=q_rank, kv_rank=kv_rank),
        grid=(m // tm,),
        in_specs=[pl.BlockSpec((tm, k), lambda i: (i, 0)),
                  pl.BlockSpec((k, n), lambda i: (0, 0))],
        out_specs=[pl.BlockSpec((tm, q_rank), lambda i: (i, 0)),
                   pl.BlockSpec((tm, kv_rank), lambda i: (i, 0)),
                   pl.BlockSpec((tm, LANES), lambda i: (i, 0))],
        out_shape=[jax.ShapeDtypeStruct((m, q_rank), BF16),
                   jax.ShapeDtypeStruct((m, kv_rank), BF16),
                   jax.ShapeDtypeStruct((m, LANES), F32)],
        compiler_params=_cparams(("arbitrary",)),
        name="latent_proj",
    )(h, w_lat)


def _rope_table_kernel(pos_ref, inv_ref, c_ref, s1_ref, s2_ref):
    half = MLA_ROPE_DIM // 2
    ang = pos_ref[0].astype(F32) * inv_ref[...]
    cos, sin = jnp.cos(ang), jnp.sin(ang)
    lane = lax.broadcasted_iota(I32, ang.shape, 1)
    c_ref[0] = jnp.where(lane < 2 * half, cos, 0.0)
    s1_ref[0] = jnp.where(lane < half, -sin, 0.0)
    s2_ref[0] = jnp.where((lane >= half) & (lane < 2 * half), sin, 0.0)


def _rope_tables(positions):
    bsz, s = positions.shape
    half = MLA_ROPE_DIM // 2
    inv = ROPE_THETA ** (-(jnp.arange(LANES, dtype=F32) % half) / half)
    ts = _tile(s, 512)
    spec = pl.BlockSpec((1, ts, LANES), lambda b, i: (b, i, 0))
    shp = jax.ShapeDtypeStruct((bsz, s, LANES), F32)
    return pl.pallas_call(
        _rope_table_kernel,
        grid=(bsz, s // ts),
        in_specs=[pl.BlockSpec((1, ts, 1), lambda b, i: (b, i, 0)),
                  pl.BlockSpec((1, LANES), lambda b, i: (0, 0))],
        out_specs=[spec, spec, spec],
        out_shape=[shp, shp, shp],
        compiler_params=_cparams(("arbitrary", "arbitrary")),
        name="rope_tables",
    )(positions.reshape(bsz, s, 1), inv.reshape(1, LANES))


def _rotate(r, c, s1, s2):
    half = MLA_ROPE_DIM // 2
    return r * c + pltpu.roll(r, LANES - half, 1) * s1 + pltpu.roll(r, half, 1) * s2


def _rms(x, gain):
    ms = jnp.mean(x * x, axis=-1, keepdims=True)
    return x * lax.rsqrt(ms + NORM_EPS) * gain


def _mla_q_kernel(cq_ref, g_ref, w_ref, c_ref, s1_ref, s2_ref, o_ref, *, heads, scale):
    y = _rms(cq_ref[0].astype(F32), g_ref[...]).astype(BF16)
    q = jnp.dot(y, w_ref[...], preferred_element_type=F32)
    c, s1, s2 = c_ref[0], s1_ref[0], s2_ref[0]
    for h in range(heads):
        base = h * MLA_QK_PAD
        o_ref[0, :, base:base + LANES] = (q[:, base:base + LANES] * scale).astype(BF16)
        r = q[:, base + LANES:base + 2 * LANES]
        o_ref[0, :, base + LANES:base + 2 * LANES] = (_rotate(r, c, s1, s2) * scale).astype(BF16)


def _mla_q(cq, gain, w_q, tabs, scale):
    bsz, s, qr = cq.shape
    heads = MLA_HEADS
    n = heads * MLA_QK_PAD
    tm = _tile(s, 512)
    tab_spec = pl.BlockSpec((1, tm, LANES), lambda b, i: (b, i, 0))
    return pl.pallas_call(
        functools.partial(_mla_q_kernel, heads=heads, scale=scale),
        grid=(bsz, s // tm),
        in_specs=[pl.BlockSpec((1, tm, qr), lambda b, i: (b, i, 0)),
                  pl.BlockSpec((1, qr), lambda b, i: (0, 0)),
                  pl.BlockSpec((qr, n), lambda b, i: (0, 0)),
                  tab_spec, tab_spec, tab_spec],
        out_specs=pl.BlockSpec((1, tm, n), lambda b, i: (b, i, 0)),
        out_shape=jax.ShapeDtypeStruct((bsz, s, n), BF16),
        compiler_params=_cparams(("arbitrary", "arbitrary")),
        name="mla_q_prep",
    )(cq, gain.reshape(1, qr), w_q, *tabs)


def _mla_kv_kernel(ckv_ref, g_ref, wk_ref, wv_ref, kpe_ref, c_ref, s1_ref, s2_ref, k_ref, v_ref, *, heads):
    y = _rms(ckv_ref[0].astype(F32), g_ref[...]).astype(BF16)
    kn = jnp.dot(y, wk_ref[...], preferred_element_type=F32)
    v_ref[0] = jnp.dot(y, wv_ref[...], preferred_element_type=F32).astype(BF16)
    kr = _rotate(kpe_ref[0], c_ref[0], s1_ref[0], s2_ref[0]).astype(BF16)
    for h in range(heads):
        base = h * MLA_QK_PAD
        k_ref[0, :, base:base + LANES] = kn[:, h * LANES:(h + 1) * LANES].astype(BF16)
        k_ref[0, :, base + LANES:base + 2 * LANES] = kr


def _mla_kv(ckv, gain, w_k, w_v, kpe, tabs):
    bsz, s, kvr = ckv.shape
    heads = MLA_HEADS
    tm = _tile(s, 512)
    tab_spec = pl.BlockSpec((1, tm, LANES), lambda b, i: (b, i, 0))
    return pl.pallas_call(
        functools.partial(_mla_kv_kernel, heads=heads),
        grid=(bsz, s // tm),
        in_specs=[pl.BlockSpec((1, tm, kvr), lambda b, i: (b, i, 0)),
                  pl.BlockSpec((1, kvr), lambda b, i: (0, 0)),
                  pl.BlockSpec((kvr, heads * MLA_NOPE_DIM), lambda b, i: (0, 0)),
                  pl.BlockSpec((kvr, heads * MLA_V_DIM), lambda b, i: (0, 0)),
                  tab_spec, tab_spec, tab_spec, tab_spec],
        out_specs=[pl.BlockSpec((1, tm, heads * MLA_QK_PAD), lambda b, i: (b, i, 0)),
                   pl.BlockSpec((1, tm, heads * MLA_V_DIM), lambda b, i: (b, i, 0))],
        out_shape=[jax.ShapeDtypeStruct((bsz, s, heads * MLA_QK_PAD), BF16),
                   jax.ShapeDtypeStruct((bsz, s, heads * MLA_V_DIM), BF16)],
        compiler_params=_cparams(("arbitrary", "arbitrary")),
        name="mla_kv_prep",
    )(ckv, gain.reshape(1, kvr), w_k, w_v, kpe, *tabs)


def _qk(q, k):
    return lax.dot_general(q, k, (((1,), (1,)), ((), ())), preferred_element_type=F32)


def _causal_mask(s, qi, ki, tq, tk):
    row = qi * tq + lax.broadcasted_iota(I32, s.shape, 0)
    col = ki * tk + lax.broadcasted_iota(I32, s.shape, 1)
    return jnp.where(row >= col, s, -jnp.inf)


def _lane_chunks(x):
    return [x[:, c * LANES:(c + 1) * LANES] for c in range(x.shape[1] // LANES)]


def _softmax_update(s, v, m_prev, l_prev, acc_prev):
    chunks = _lane_chunks(s)
    cmax = functools.reduce(jnp.maximum, chunks)
    m_new = jnp.maximum(m_prev, jnp.max(cmax, axis=-1, keepdims=True))
    alpha = jnp.exp2(m_prev - m_new)
    ps = [jnp.exp2(c - m_new) for c in chunks]
    psum = functools.reduce(lambda a, b: a + b, ps)
    l_new = alpha * l_prev + jnp.sum(psum, axis=-1, keepdims=True)
    p = jnp.concatenate([c.astype(BF16) for c in ps], axis=1)
    pv = jnp.dot(p, v, preferred_element_type=F32)
    acc_new = jnp.concatenate([alpha * a for a in _lane_chunks(acc_prev)], axis=1) + pv
    return m_new, l_new, acc_new


def _emit_pipelined(score_fns, update_fns):
    n = len(score_fns)
    pending = [score_fns[i]() for i in range(min(ATTN_LOOKAHEAD, n))]
    for i in range(n):
        if i + ATTN_LOOKAHEAD < n:
            pending.append(score_fns[i + ATTN_LOOKAHEAD]())
        update_fns[i](pending[i])
        pending[i] = None


def _diag_mask(s, r):
    rows = s.shape[0]
    off = s.shape[1] - rows
    row = lax.broadcasted_iota(I32, s.shape, 0)
    col = lax.broadcasted_iota(I32, s.shape, 1)
    return jnp.where(col - off <= row, s, -jnp.inf)


def _slabs(w, steps):
    cols = w.shape[-1]
    rows = w.size // cols
    assert rows % (steps * 2 * SUBLANES) == 0, (w.shape, steps)
    return w.reshape(steps, rows // steps, cols)


def _slab_specs(slabs, step_of):
    return [pl.BlockSpec((1,) + w.shape[1:], lambda *g: (step_of(*g), 0, 0)) for w in slabs]


def _cast_slabs(src_refs, dst_refs):
    for src, dst in zip(src_refs, dst_refs):
        dst[...] = src[...].astype(BF16)


def _mla_attn_kernel(q_ref, k_ref, v_ref, *rest, tq, tk, rb, n_cast):
    cast_in, o_ref, cast_out, scr = (rest[:n_cast], rest[n_cast], rest[n_cast + 1:2 * n_cast + 1],
                                     rest[2 * n_cast + 1:])
    qi = pl.program_id(2)
    nr = tq // rb
    state = [scr[3 * r:3 * r + 3] for r in range(nr)]
    for m_scr, l_scr, acc_scr in state:
        m_scr[...] = jnp.full(m_scr.shape, -jnp.inf, F32)
        l_scr[...] = jnp.zeros(l_scr.shape, F32)
        acc_scr[...] = jnp.zeros(acc_scr.shape, F32)

    def rows(r):
        return pl.ds(r * rb, rb)

    def update(r, s, v):
        m_scr, l_scr, acc_scr = state[r]
        m, l, acc = _softmax_update(s, v, m_scr[...], l_scr[...], acc_scr[...])
        m_scr[...] = m
        l_scr[...] = l
        acc_scr[...] = acc

    def full_tile(ki, carry):
        start = pl.multiple_of(ki * tk, tk)
        k = k_ref[0, pl.ds(start, tk), :]
        v = v_ref[0, pl.ds(start, tk), :]
        _emit_pipelined([functools.partial(_qk, q_ref[0, rows(r), :], k) for r in range(nr)],
                        [functools.partial(update, r, v=v) for r in range(nr)])
        return carry
    lax.fori_loop(0, qi * (tq // tk), full_tile, 0)

    _cast_slabs(cast_in, cast_out)
    dstart = pl.multiple_of(qi * tq, tq)
    cols = [pl.ds(dstart, (r + 1) * rb) for r in range(nr)]
    _emit_pipelined(
        [lambda r=r: _diag_mask(_qk(q_ref[0, rows(r), :], k_ref[0, cols[r], :]), r) for r in range(nr)],
        [lambda s, r=r: update(r, s, v_ref[0, cols[r], :]) for r in range(nr)])

    for r, (_, l_scr, acc_scr) in enumerate(state):
        o_ref[0, rows(r), :] = (acc_scr[...] / l_scr[...]).astype(o_ref.dtype)


def _attn_state_scratch(nr, rb, dv):
    return [pltpu.VMEM((rb, w), F32) for _ in range(nr) for w in (LANES, LANES, dv)]


def _mla_attention(q, k, v, cast=()):
    bsz, s, _ = q.shape
    heads = MLA_HEADS
    t = _tile(s, 1024)
    rb = _tile(t, 256)
    nq = s // t
    slabs = [_slabs(w, bsz * heads * nq) for w in cast]
    slab_specs = _slab_specs(slabs, lambda b, h, i: (b * heads + h) * nq + i)
    out = pl.pallas_call(
        functools.partial(_mla_attn_kernel, tq=t, tk=t, rb=rb, n_cast=len(cast)),
        grid=(bsz, heads, nq),
        in_specs=[pl.BlockSpec((1, t, MLA_QK_PAD), lambda b, h, i: (b, i, h)),
                  pl.BlockSpec((1, s, MLA_QK_PAD), lambda b, h, i: (b, 0, h)),
                  pl.BlockSpec((1, s, MLA_V_DIM), lambda b, h, i: (b, 0, h))] + slab_specs,
        out_specs=[pl.BlockSpec((1, t, MLA_V_DIM), lambda b, h, i: (b, i, h))] + slab_specs,
        out_shape=[jax.ShapeDtypeStruct((bsz, s, heads * MLA_V_DIM), BF16)]
                  + [jax.ShapeDtypeStruct(w.shape, BF16) for w in slabs],
        scratch_shapes=_attn_state_scratch(t // rb, rb, MLA_V_DIM),
        compiler_params=_cparams(("arbitrary", "arbitrary", "arbitrary")),
        name="mla_attn",
    )(q, k, v, *slabs)
    return [out[0]] + [o.reshape(w.shape) for o, w in zip(out[1:], cast)]


def _diff_attn_kernel(q_ref, k_ref, v_ref, qpos_ref, kpos_ref, slope_ref, lam_ref, subln_ref, *rest,
                      tq, tk, rb, lam_init, n_cast):
    cast_in, o_ref, cast_out, scr = (rest[:n_cast], rest[n_cast], rest[n_cast + 1:2 * n_cast + 1],
                                     rest[2 * n_cast + 1:])
    qi = pl.program_id(2)
    d = DIFF_HEAD_DIM
    nr = tq // rb
    state = [[scr[6 * r + 3 * g:6 * r + 3 * g + 3] for g in range(2)] for r in range(nr)]
    for r in range(nr):
        for m_scr, l_scr, a_scr in state[r]:
            m_scr[...] = jnp.full(m_scr.shape, -jnp.inf, F32)
            l_scr[...] = jnp.zeros(l_scr.shape, F32)
            a_scr[...] = jnp.zeros(a_scr.shape, F32)
    slope = slope_ref[0, :, 0:1]

    def rows(r):
        return pl.ds(r * rb, rb)

    def kpos(first, count):
        return jnp.concatenate([kpos_ref[0, first + j] for j in range(count)], axis=1)

    chains = [(r, g) for r in range(nr) for g in range(2)]

    def emit(k_of, v_of, kp_of, masked):
        bias = {}

        def score(r, g):
            if g == 0:
                bias[r] = slope * jnp.abs(qpos_ref[0, rows(r), :] - kp_of(r))
            s = _qk(q_ref[0, rows(r), g * d:(g + 1) * d], k_of(r)[:, g * d:(g + 1) * d]) - bias[r]
            return _diag_mask(s, r) if masked else s

        def update(r, g, s):
            m_scr, l_scr, a_scr = state[r][g]
            m, l, acc = _softmax_update(s, v_of(r), m_scr[...], l_scr[...], a_scr[...])
            m_scr[...] = m
            l_scr[...] = l
            a_scr[...] = acc

        _emit_pipelined([functools.partial(score, r, g) for r, g in chains],
                        [functools.partial(update, r, g) for r, g in chains])

    def full_tile(ki, carry):
        start = pl.multiple_of(ki * tk, tk)
        k = k_ref[0, pl.ds(start, tk), :]
        v = v_ref[0, pl.ds(start, tk), :]
        kp = kpos(ki * (tk // rb), tk // rb)
        emit(lambda r: k, lambda r: v, lambda r: kp, False)
        return carry
    lax.fori_loop(0, qi * (tq // tk), full_tile, 0)

    _cast_slabs(cast_in, cast_out)
    dstart = pl.multiple_of(qi * tq, tq)
    cols = [pl.ds(dstart, (r + 1) * rb) for r in range(nr)]
    emit(lambda r: k_ref[0, cols[r], :], lambda r: v_ref[0, cols[r], :], lambda r: kpos(qi * nr, r + 1), True)

    lp = lam_ref[...]
    e1 = jnp.exp(jnp.sum(lp[0:1] * lp[1:2], axis=-1, keepdims=True))
    e2 = jnp.exp(jnp.sum(lp[2:3] * lp[3:4], axis=-1, keepdims=True))
    lam = e1 - e2 + lam_init
    for r in range(nr):
        (_, l1, a1), (_, l2, a2) = state[r]
        o1 = [a / l1[...] for a in _lane_chunks(a1[...])]
        o2 = [a / l2[...] for a in _lane_chunks(a2[...])]
        o = jnp.concatenate([x - lam * y for x, y in zip(o1, o2)], axis=1)
        o_ref[0, rows(r), :] = (_rms(o, subln_ref[...]) * (1.0 - lam_init)).astype(o_ref.dtype)


def _diff_attention(qkv, positions, diff_lambda, subln, lam_init, cast=()):
    bsz, s, _ = qkv.shape
    heads = DIFF_HEADS
    dv = 2 * DIFF_HEAD_DIM
    t = _tile(s, 1024)
    tk = _tile(t, 1024)
    rb = _tile(tk, 256)
    nq = s // t
    slopes = 2.0 ** (-8.0 * jnp.arange(1, heads + 1, dtype=F32) / heads) * LOG2_E
    slopes = jnp.broadcast_to(slopes[:, None, None], (heads, 1, LANES))
    posf = positions.astype(F32)
    slabs = [_slabs(w, bsz * heads * nq) for w in cast]
    slab_specs = _slab_specs(slabs, lambda b, h, i: (b * heads + h) * nq + i)
    out = pl.pallas_call(
        functools.partial(_diff_attn_kernel, tq=t, tk=tk, rb=rb, lam_init=lam_init, n_cast=len(cast)),
        grid=(bsz, heads, nq),
        in_specs=[pl.BlockSpec((1, t, dv), lambda b, h, i: (b, i, h)),
                  pl.BlockSpec((1, s, dv), lambda b, h, i: (b, 0, heads + h)),
                  pl.BlockSpec((1, s, dv), lambda b, h, i: (b, 0, 2 * heads + h)),
                  pl.BlockSpec((1, t, 1), lambda b, h, i: (b, i, 0)),
                  pl.BlockSpec((1, s // rb, 1, rb), lambda b, h, i: (b, 0, 0, 0)),
                  pl.BlockSpec((1, 1, LANES), lambda b, h, i: (h, 0, 0)),
                  pl.BlockSpec((4, DIFF_HEAD_DIM), lambda b, h, i: (0, 0)),
                  pl.BlockSpec((1, dv), lambda b, h, i: (0, 0))] + slab_specs,
        out_specs=[pl.BlockSpec((1, t, dv), lambda b, h, i: (b, i, h))] + slab_specs,
        out_shape=[jax.ShapeDtypeStruct((bsz, s, heads * dv), BF16)]
                  + [jax.ShapeDtypeStruct(w.shape, BF16) for w in slabs],
        scratch_shapes=_attn_state_scratch(2 * (t // rb), rb, dv),
        compiler_params=_cparams(("arbitrary", "arbitrary", "arbitrary")),
        name="diff_attn",
    )(qkv, qkv, qkv, posf.reshape(bsz, s, 1), posf.reshape(bsz, s // rb, 1, rb), slopes, diff_lambda,
      subln.reshape(1, dv), *slabs)
    return [out[0]] + [o.reshape(w.shape) for o, w in zip(out[1:], cast)]


def _merge_kernel(od_ref, om_ref, wd_ref, wm_ref, g0_ref, g1_ref, x_ref, gt_ref, o_ref):
    yd = jnp.dot(od_ref[0], wd_ref[...], preferred_element_type=F32)
    ym = jnp.dot(om_ref[0], wm_ref[...], preferred_element_type=F32)
    y = g0_ref[0].astype(F32) * yd + g1_ref[0].astype(F32) * ym
    o_ref[0] = x_ref[0] + gt_ref[0, 0] * y


def _merge(o_d, o_m, w_d, w_m, gates, x, mod, gt_idx):
    bsz, s, d = x.shape
    kd, km = o_d.shape[-1], o_m.shape[-1]
    tm, tn = _tile(s, 1024), _tile(d, 512)
    nj = d // tn
    return pl.pallas_call(
        _merge_kernel,
        grid=(bsz, s // tm, nj),
        in_specs=[pl.BlockSpec((1, tm, kd), lambda b, i, j: (b, i, 0)),
                  pl.BlockSpec((1, tm, km), lambda b, i, j: (b, i, 0)),
                  pl.BlockSpec((kd, tn), lambda b, i, j: (0, j)),
                  pl.BlockSpec((km, tn), lambda b, i, j: (0, j)),
                  pl.BlockSpec((1, tm, tn), lambda b, i, j: (b, i, j)),
                  pl.BlockSpec((1, tm, tn), lambda b, i, j: (b, i, j + nj)),
                  pl.BlockSpec((1, tm, tn), lambda b, i, j: (b, i, j)),
                  pl.BlockSpec((1, 1, 1, tn), lambda b, i, j: (gt_idx, b, 0, j))],
        out_specs=pl.BlockSpec((1, tm, tn), lambda b, i, j: (b, i, j)),
        out_shape=jax.ShapeDtypeStruct((bsz, s, d), F32),
        compiler_params=_cparams(("arbitrary", "arbitrary", "arbitrary")),
        name="out_merge",
    )(o_d, o_m, w_d, w_m, gates, gates, x, mod)


def _pack_rows(y):
    half = y.shape[1] // 2
    bits = lax.bitcast_convert_type(y.astype(BF16).astype(F32), U32)
    return (bits[:, half:] & jnp.uint32(0xFFFF0000)) | (bits[:, :half] >> 16)


def _unpack_rows(w):
    lo = lax.bitcast_convert_type(w << 16, F32)
    hi = lax.bitcast_convert_type(w & jnp.uint32(0xFFFF0000), F32)
    return lo, hi


def _ffn_norm_kernel(x_ref, g_ref, sc_ref, sh_ref, rw_ref, h_ref, hp_ref, lt_ref, *, nch):
    x = x_ref[0]
    y = _rms(x, g_ref[...]) * (1.0 + sc_ref[0, 0]) + sh_ref[0, 0]
    hb = y.astype(BF16)
    h_ref[0] = hb
    lt_ref[...] = lax.dot_general(rw_ref[...], hb, (((1,), (1,)), ((), ())), preferred_element_type=F32)
    packed = _pack_rows(y)
    for c in range(nch):
        hp_ref[pl.ds(c, x.shape[0], stride=nch), :] = packed[:, c * LANES:(c + 1) * LANES]


def _ffn_norm(x, gain, mod, sc_idx, sh_idx, router_wt):
    bsz, s, d = x.shape
    e = router_wt.shape[0]
    nch = d // (2 * LANES)
    tm = _tile(s, 256)
    ns = s // tm
    return pl.pallas_call(
        functools.partial(_ffn_norm_kernel, nch=nch),
        grid=(bsz, ns),
        in_specs=[pl.BlockSpec((1, tm, d), lambda b, i: (b, i, 0)),
                  pl.BlockSpec((1, d), lambda b, i: (0, 0)),
                  pl.BlockSpec((1, 1, 1, d), lambda b, i: (sc_idx, b, 0, 0)),
                  pl.BlockSpec((1, 1, 1, d), lambda b, i: (sh_idx, b, 0, 0)),
                  pl.BlockSpec((e, d), lambda b, i: (0, 0))],
        out_specs=[pl.BlockSpec((1, tm, d), lambda b, i: (b, i, 0)),
                   pl.BlockSpec((tm * nch, LANES), lambda b, i: (b * ns + i, 0)),
                   pl.BlockSpec((e, tm), lambda b, i: (0, b * ns + i))],
        out_shape=[jax.ShapeDtypeStruct((bsz, s, d), BF16),
                   jax.ShapeDtypeStruct((bsz * s * nch, LANES), U32),
                   jax.ShapeDtypeStruct((e, bsz * s), F32)],
        compiler_params=_cparams(("arbitrary", "arbitrary")),
        name="ffn_norm_router",
    )(x, gain.reshape(1, d), mod, mod, router_wt)


def _first_index(hit, iota, axis, size):
    return jnp.min(jnp.where(hit, iota, size), axis=axis, keepdims=True)


def _route_kernel(lt_ref, bias_ref, idx_ref, wt_ref, rank_ref, sizes_ref, cnt_scr, *, n_exp, tn):
    i = pl.program_id(0)
    gsz = n_exp // N_GROUPS

    @pl.when(i == 0)
    def _():
        cnt_scr[...] = jnp.zeros(cnt_scr.shape, F32)

    scores = jax.nn.sigmoid(lt_ref[...])
    sel = scores + bias_ref[...]
    sel3 = sel.reshape(N_GROUPS, gsz, tn)
    j3 = lax.broadcasted_iota(I32, sel3.shape, 1)
    top1 = jnp.max(sel3, axis=1, keepdims=True)
    first = _first_index(sel3 == top1, j3, 1, gsz)
    top2 = jnp.max(jnp.where(j3 == first, -jnp.inf, sel3), axis=1, keepdims=True)
    gscore = (top1 + top2).reshape(N_GROUPS, tn)

    giota = lax.broadcasted_iota(I32, gscore.shape, 0)
    gmask = jnp.zeros(gscore.shape, jnp.bool_)
    for _ in range(TOPK_GROUPS):
        best = jnp.max(gscore, axis=0, keepdims=True)
        gi = _first_index(gscore == best, giota, 0, N_GROUPS)
        hit = giota == gi
        gmask = gmask | hit
        gscore = jnp.where(hit, -jnp.inf, gscore)

    emask = jnp.broadcast_to(gmask.reshape(N_GROUPS, 1, tn), sel3.shape)
    cand = jnp.where(emask, sel3, -jnp.inf).reshape(n_exp, tn)
    eiota = lax.broadcasted_iota(I32, cand.shape, 0)
    hits, idxs, vals = [], [], []
    for _ in range(TOP_K):
        best = jnp.max(cand, axis=0, keepdims=True)
        ei = _first_index(cand == best, eiota, 0, n_exp)
        hit = eiota == ei
        hits.append(hit)
        idxs.append(ei)
        vals.append(jnp.sum(jnp.where(hit, scores, 0.0), axis=0, keepdims=True))
        cand = jnp.where(hit, -jnp.inf, cand)
    total = functools.reduce(lambda a, b: a + b, vals)

    chosen = functools.reduce(lambda a, b: a | b, hits)
    onehot = jnp.where(chosen, 1.0, 0.0)
    r = lax.broadcasted_iota(I32, (tn, tn), 0)
    c = lax.broadcasted_iota(I32, (tn, tn), 1)
    upper = jnp.where(r < c, 1.0, 0.0).astype(BF16)
    before = cnt_scr[...] + jnp.dot(onehot.astype(BF16), upper, preferred_element_type=F32)
    cnt_new = cnt_scr[...] + jnp.sum(onehot, axis=1, keepdims=True)
    cnt_scr[...] = cnt_new

    pad = SUBLANES - TOP_K
    ranks = [jnp.sum(jnp.where(h, before, 0.0), axis=0, keepdims=True).astype(I32) for h in hits]
    zi = [jnp.zeros((pad, tn), I32)]
    idx_ref[...] = jnp.concatenate(idxs + zi, axis=0)
    rank_ref[...] = jnp.concatenate(ranks + zi, axis=0)
    wt_ref[...] = jnp.concatenate([v / total * ROUTED_SCALE for v in vals] + [jnp.zeros((pad, tn), F32)], axis=0)
    sizes_ref[...] = jnp.broadcast_to(cnt_new, sizes_ref.shape).astype(I32)


def _route(logits_t, bias):
    n_exp, n = logits_t.shape
    tn = _tile(n, 512)
    row_spec = pl.BlockSpec((SUBLANES, tn), lambda i: (0, i))
    return pl.pallas_call(
        functools.partial(_route_kernel, n_exp=n_exp, tn=tn),
        grid=(n // tn,),
        in_specs=[pl.BlockSpec((n_exp, tn), lambda i: (0, i)),
                  pl.BlockSpec((n_exp, 1), lambda i: (0, 0))],
        out_specs=[row_spec, row_spec, row_spec, pl.BlockSpec((n_exp, LANES), lambda i: (0, 0))],
        out_shape=[jax.ShapeDtypeStruct((SUBLANES, n), I32),
                   jax.ShapeDtypeStruct((SUBLANES, n), F32),
                   jax.ShapeDtypeStruct((SUBLANES, n), I32),
                   jax.ShapeDtypeStruct((n_exp, LANES), I32)],
        scratch_shapes=[pltpu.VMEM((n_exp, 1), F32)],
        compiler_params=_cparams(("arbitrary",)),
        name="route_topk",
    )(logits_t, bias.reshape(n_exp, 1))


def _slot_rows(ref, slot, blk, nch):
    assert blk & (blk - 1) == 0
    b = jnp.right_shift(slot, blk.bit_length() - 1)
    r = jnp.bitwise_and(slot, blk - 1)
    return ref.at[pl.ds(pl.multiple_of(b * nch, nch), nch), r]


def _dispatch_kernel(dest_ref, seg_ref, hp_ref, xs_ref, zero_scr, sem, zsem, *, n_tok, tc, nch, n_exp, blk, nb):
    i = pl.program_id(0)

    def row_copy(j, slot):
        return pltpu.make_async_copy(hp_ref.at[pl.ds(pl.multiple_of(j * nch, nch), nch)],
                                     _slot_rows(xs_ref, slot, blk, nch), sem)

    def zero_row(slot):
        return pltpu.make_async_copy(zero_scr.at[:, 0], _slot_rows(xs_ref, slot, blk, nch), zsem)

    def zero_block(b):
        return pltpu.make_async_copy(zero_scr, xs_ref.at[pl.ds(pl.multiple_of(b * nch, nch), nch)], zsem)

    def start_then_wait(lo, hi, copy):
        def start(v, c):
            copy(v).start()
            return c
        lax.fori_loop(lo, hi, start, 0)

        def wait(v, c):
            copy(v).wait()
            return c
        lax.fori_loop(lo, hi, wait, 0)

    @pl.when(i == 0)
    def _():
        zero_scr[...] = jnp.zeros(zero_scr.shape, U32)

        def per_expert(e, carry):
            start_then_wait(seg_ref[e], seg_ref[n_exp + e], zero_row)
            return carry
        lax.fori_loop(0, n_exp, per_expert, 0)
        start_then_wait(seg_ref[2 * n_exp], nb, zero_block)

    base = i * tc

    def start(j, c):
        for k in range(TOP_K):
            row_copy(j, dest_ref[k * n_tok + base + j]).start(priority=k % DMA_QUEUES)
        return c
    lax.fori_loop(0, tc, start, 0, unroll=DMA_ISSUE_UNROLL)

    for _ in range(tc * TOP_K):
        row_copy(0, 0).wait()


def _dispatch(dest, seg, hp, nb, blk, nch, n_exp):
    n_tok = dest.shape[0] // TOP_K
    tc = _tile(n_tok, 256)
    return pl.pallas_call(
        functools.partial(_dispatch_kernel, n_tok=n_tok, tc=tc, nch=nch, n_exp=n_exp, blk=blk, nb=nb),
        grid_spec=pltpu.PrefetchScalarGridSpec(
            num_scalar_prefetch=2,
            grid=(n_tok // tc,),
            in_specs=[pl.BlockSpec((tc * nch, LANES), lambda i, dst, sg: (i, 0))],
            out_specs=pl.BlockSpec(memory_space=pl.ANY),
            scratch_shapes=[pltpu.VMEM((nch, blk, LANES), U32), pltpu.SemaphoreType.DMA,
                            pltpu.SemaphoreType.DMA]),
        out_shape=jax.ShapeDtypeStruct((nb * nch, blk, LANES), U32),
        compiler_params=_cparams(("arbitrary",)),
        name="moe_dispatch",
    )(dest, seg, hp)


def _load_rows(ref, nch):
    los, his = [], []
    for c in range(nch):
        lo, hi = _unpack_rows(ref[c])
        los.append(lo.astype(BF16))
        his.append(hi.astype(BF16))
    return jnp.concatenate(los + his, axis=1)


def _expert_kernel(be_ref, nact_ref, xs_ref, w1_ref, w3_ref, w2_ref, ys_ref, *, nch):
    active = pl.program_id(0) < nact_ref[0]

    @pl.when(jnp.logical_not(active))
    def _():
        ys_ref[...] = jnp.zeros(ys_ref.shape, U32)

    @pl.when(active)
    def _():
        x = _load_rows(xs_ref, nch)
        a = jnp.dot(x, w1_ref[0], preferred_element_type=F32)
        b = jnp.dot(x, w3_ref[0], preferred_element_type=F32)
        y = jnp.dot((_silu(a) * b).astype(BF16), w2_ref[0], preferred_element_type=F32)
        packed = _pack_rows(y)
        for c in range(nch):
            ys_ref[c] = packed[:, c * LANES:(c + 1) * LANES]


def _experts(block_expert, nact, xs, w1, w3, w2, nch):
    n_exp, d, f = w1.shape
    tm = xs.shape[1]
    nb = xs.shape[0] // nch
    blk = lambda i, be, na: (jnp.minimum(i, na[0] - 1), 0, 0)
    wsel = lambda i, be, na: (be[jnp.minimum(i, na[0] - 1)], 0, 0)
    return pl.pallas_call(
        functools.partial(_expert_kernel, nch=nch),
        grid_spec=pltpu.PrefetchScalarGridSpec(
            num_scalar_prefetch=2,
            grid=(nb,),
            in_specs=[pl.BlockSpec((nch, tm, LANES), blk),
                      pl.BlockSpec((1, d, f), wsel),
                      pl.BlockSpec((1, d, f), wsel),
                      pl.BlockSpec((1, f, d), wsel)],
            out_specs=pl.BlockSpec((nch, tm, LANES), lambda i, be, na: (i, 0, 0))),
        out_shape=jax.ShapeDtypeStruct(xs.shape, U32),
        compiler_params=_cparams(("arbitrary",)),
        name="moe_experts",
    )(block_expert, nact, xs, w1, w3, w2)


def _shared_ffn_kernel(h_ref, w1_ref, w3_ref, w2_ref, o_ref, *, out_chunk):
    h = h_ref[...]
    a = jnp.dot(h, w1_ref[...], preferred_element_type=F32)
    g = jnp.dot(h, w3_ref[...], preferred_element_type=F32)
    mid = (_silu(a) * g).astype(BF16)
    for c in range(0, o_ref.shape[1], out_chunk):
        o_ref[:, c:c + out_chunk] = jnp.dot(mid, w2_ref[:, c:c + out_chunk],
                                            preferred_element_type=F32).astype(o_ref.dtype)


def _shared_ffn(h, w1, w3, w2):
    n, d = h.shape
    f = w1.shape[1]
    tm = _tile(n, 512)
    return pl.pallas_call(
        functools.partial(_shared_ffn_kernel, out_chunk=_tile(d, 1024)),
        grid=(n // tm,),
        in_specs=[pl.BlockSpec((tm, d), lambda i: (i, 0)),
                  pl.BlockSpec((d, f), lambda i: (0, 0)),
                  pl.BlockSpec((d, f), lambda i: (0, 0)),
                  pl.BlockSpec((f, d), lambda i: (0, 0))],
        out_specs=pl.BlockSpec((tm, d), lambda i: (i, 0)),
        out_shape=jax.ShapeDtypeStruct((n, d), BF16),
        compiler_params=_cparams(("arbitrary",)),
        name="shared_ffn",
    )(h, w1, w3, w2)


def _combine_kernel(dest_ref, ys_ref, wt_ref, shared_ref, x_ref, gt_ref, fn_ref, o_ref,
                    ybuf0, ybuf1, sem0, sem1, *, n_tok, tm, nch, nsteps, final):
    step = pl.program_id(0) * pl.num_programs(1) + pl.program_id(1)
    bufs = ((ybuf0, sem0), (ybuf1, sem1))

    def row_copy(buf, sem, slot, k, j):
        return pltpu.make_async_copy(_slot_rows(ys_ref, slot, ys_ref.shape[1], nch), buf.at[k, :, j], sem)

    def start_gathers(tile, buf, sem, inline=False):
        base = tile * tm

        def start(j, c):
            for k in range(TOP_K):
                row_copy(buf, sem, dest_ref[k * n_tok + base + j], k, j).start(priority=k % DMA_QUEUES)
            return c
        if inline:
            for j in range(tm):
                start(j, 0)
        else:
            lax.fori_loop(0, tm, start, 0, unroll=DMA_ISSUE_UNROLL)

    def wait_gathers(buf, sem):
        for _ in range(tm * TOP_K):
            row_copy(buf, sem, 0, 0, 0).wait()

    @pl.when(step == 0)
    def _():
        start_gathers(0, *bufs[0])

    def consume(cur, nxt):
        wait_gathers(*cur)
        start_gathers(jnp.minimum(step + 1, nsteps - 1), *nxt, inline=True)
        ybuf = cur[0]
        wts = wt_ref[...]
        los = [None] * nch
        his = [None] * nch
        for k in range(TOP_K):
            wk = jnp.broadcast_to(wts[:, k:k + 1], (tm, LANES))
            for c in range(nch):
                lo, hi = _unpack_rows(ybuf[k, c])
                los[c] = wk * lo if k == 0 else los[c] + wk * lo
                his[c] = wk * hi if k == 0 else his[c] + wk * hi
        routed = jnp.concatenate(los + his, axis=1)
        y = x_ref[0] + gt_ref[0, 0] * (routed + shared_ref[0].astype(F32))
        o_ref[0] = _rms(y, fn_ref[...]) if final else y

    pl.when(step % 2 == 0)(lambda: consume(bufs[0], bufs[1]))
    pl.when(step % 2 == 1)(lambda: consume(bufs[1], bufs[0]))

    @pl.when(step == nsteps - 1)
    def _():
        wait_gathers(*bufs[nsteps % 2])


def _combine(dest, ys, wts, shared, x, mod, gt_idx, final_norm, nch, final):
    bsz, s, d = x.shape
    n_tok = bsz * s
    tm = _tile(s, 128)
    ns = s // tm
    return pl.pallas_call(
        functools.partial(_combine_kernel, n_tok=n_tok, tm=tm, nch=nch, nsteps=bsz * ns, final=final),
        grid_spec=pltpu.PrefetchScalarGridSpec(
            num_scalar_prefetch=1,
            grid=(bsz, ns),
            in_specs=[pl.BlockSpec(memory_space=pl.ANY),
                      pl.BlockSpec((tm, SUBLANES), lambda b, i, dst: (b * ns + i, 0)),
                      pl.BlockSpec((1, tm, d), lambda b, i, dst: (b, i, 0)),
                      pl.BlockSpec((1, tm, d), lambda b, i, dst: (b, i, 0)),
                      pl.BlockSpec((1, 1, 1, d), lambda b, i, dst: (gt_idx, b, 0, 0)),
                      pl.BlockSpec((1, d), lambda b, i, dst: (0, 0))],
            out_specs=pl.BlockSpec((1, tm, d), lambda b, i, dst: (b, i, 0)),
            scratch_shapes=[pltpu.VMEM((TOP_K, nch, tm, LANES), U32), pltpu.VMEM((TOP_K, nch, tm, LANES), U32),
                            pltpu.SemaphoreType.DMA, pltpu.SemaphoreType.DMA]),
        out_shape=jax.ShapeDtypeStruct((bsz, s, d), F32),
        compiler_params=_cparams(("arbitrary", "arbitrary")),
        name="moe_combine",
    )(dest, ys, wts, shared, x, mod, final_norm.reshape(1, d))


def _mla_weights(w_uq, w_ukv):
    heads = MLA_HEADS
    qr, kvr = w_uq.shape[0], w_ukv.shape[0]
    wq = w_uq.reshape(qr, heads, MLA_NOPE_DIM + MLA_ROPE_DIM)
    wq = jnp.pad(wq, ((0, 0), (0, 0), (0, MLA_QK_PAD - MLA_NOPE_DIM - MLA_ROPE_DIM)))
    wkv = w_ukv.reshape(kvr, heads, MLA_NOPE_DIM + MLA_V_DIM)
    wk = wkv[:, :, :MLA_NOPE_DIM].reshape(kvr, heads * MLA_NOPE_DIM)
    wv = wkv[:, :, MLA_NOPE_DIM:].reshape(kvr, heads * MLA_V_DIM)
    return (wq.reshape(qr, heads * MLA_QK_PAD).astype(BF16), wk.astype(BF16), wv.astype(BF16))


def _layer(x, mod, positions, tabs, l, norm_attn, w_in, diff_lambda, diff_subln, mla_q_norm, mla_w_uq,
           mla_kv_norm, mla_w_ukv, w_out, norm_ffn, router_w, router_bias, exp_w1, exp_w3, exp_w2,
           shared_w1, shared_w3, shared_w2, final_norm, final):
    bsz, s, d = x.shape
    n_tok = bsz * s
    q_rank, kv_rank = mla_w_uq.shape[0], mla_w_ukv.shape[0]
    qk_cols = 2 * DIFF_HEADS * DIFF_HEAD_DIM
    v_cols = DIFF_HEADS * 2 * DIFF_HEAD_DIM
    qkv_cols = 2 * qk_cols + v_cols
    lat_cols = q_rank + kv_rank + MLA_ROPE_DIM
    lam_init = 0.8 - 0.6 * math.exp(-0.3 * l)

    h = _norm_mod(x, norm_attn, mod, 1, 0).reshape(n_tok, d)
    w_qkv, w_lat, w_gate = _w_in_split(w_in, qkv_cols, lat_cols)
    qkv = _matmul(h, w_qkv, BF16, name="qkv_proj", scaled_cols=qk_cols,
                  col_scale=DIFF_HEAD_DIM ** -0.5 * LOG2_E)[0].reshape(bsz, s, qkv_cols)
    cq, ckv, kpe = _latent_proj(h, w_lat, q_rank, kv_rank)
    gates, exp_w3b = _matmul(h, w_gate, BF16, name="gate_proj", sigmoid=True, cast=(exp_w3,))
    gates = gates.reshape(bsz, s, 2 * d)
    o_d, exp_w2b = _diff_attention(qkv, positions, diff_lambda, diff_subln, lam_init, cast=(exp_w2,))

    wq, wk, wv = _mla_weights(mla_w_uq, mla_w_ukv)
    scale = (MLA_NOPE_DIM + MLA_ROPE_DIM) ** -0.5 * LOG2_E
    q_m = _mla_q(cq.reshape(bsz, s, q_rank), mla_q_norm, wq, tabs, scale)
    k_m, v_m = _mla_kv(ckv.reshape(bsz, s, kv_rank), mla_kv_norm, wk, wv, kpe.reshape(bsz, s, LANES), tabs)
    o_m, exp_w1b = _mla_attention(q_m, k_m, v_m, cast=(exp_w1,))

    w_o = w_out.astype(BF16)
    x = _merge(o_d, o_m, w_o[:v_cols], w_o[v_cols:], gates, x, mod, 2)

    n_exp = router_w.shape[1]
    nch = d // (2 * LANES)
    h2, hp, logits_t = _ffn_norm(x, norm_ffn, mod, 4, 3, router_w.T.astype(BF16))
    idx_t, wts_t, rank_t, sizes = _route(logits_t, router_bias)

    blk = EXPERT_ROWS
    sizes = sizes[:, 0]
    padded = (sizes + blk - 1) // blk * blk
    pad_end = jnp.cumsum(padded)
    pad_start = pad_end - padded
    onehot = idx_t[:TOP_K, :, None] == jnp.arange(n_exp, dtype=I32)
    dest = (jnp.sum(jnp.where(onehot, pad_start, 0), axis=-1) + rank_t[:TOP_K]).astype(I32).reshape(-1)
    n_blocks = -(-n_tok * TOP_K // blk) + n_exp
    block_start = jnp.arange(n_blocks, dtype=I32) * blk
    block_expert = jnp.minimum(jnp.sum(pad_end[None, :] <= block_start[:, None], axis=1), n_exp - 1).astype(I32)
    nact = (pad_end[-1:] // blk).astype(I32)
    seg = jnp.concatenate([pad_start + sizes, pad_end, nact]).astype(I32)

    xs = _dispatch(dest, seg, hp, n_blocks, blk, nch, n_exp)
    ys = _experts(block_expert, nact, xs, exp_w1b, exp_w3b, exp_w2b, nch)
    shared = _shared_ffn(h2.reshape(n_tok, d), shared_w1.astype(BF16), shared_w3.astype(BF16),
                         shared_w2.astype(BF16)).reshape(bsz, s, d)
    return _combine(dest, ys, wts_t.T, shared, x, mod, 5, final_norm, nch, final)


def kernel(x, c, positions, w_ada, b_ada, norm_attn, w_in, diff_lambda, diff_subln, mla_q_norm, mla_w_uq,
           mla_kv_norm, mla_w_ukv, w_out, norm_ffn, router_w, router_bias, exp_w1, exp_w3, exp_w2,
           shared_w1, shared_w3, shared_w2, final_norm):
    bsz, s, d = x.shape
    depth = w_ada.shape[0]
    tabs = _rope_tables(positions)
    for l in range(depth):
        mod = _ada(c, w_ada[l], b_ada[l])
        mod = mod.reshape(bsz, N_MOD, 1, d).transpose(1, 0, 2, 3)
        x = _layer(x, mod, positions, tabs, l, norm_attn[l], w_in[l], diff_lambda[l], diff_subln[l],
                   mla_q_norm[l], mla_w_uq[l], mla_kv_norm[l], mla_w_ukv[l], w_out[l], norm_ffn[l],
                   router_w[l], router_bias[l], exp_w1[l], exp_w3[l], exp_w2[l],
                   shared_w1[l], shared_w3[l], shared_w2[l], final_norm, l == depth - 1)
    return x
```
